```python
import jax, jax.numpy as jnp
from jax import lax
import numpy as np

D_MODEL = 1024
BATCH = 8
SEQ = 2048
DEPTH = 2
DEC_BATCH = 128
DEC_SEQ = 4
PAST_LEN = 16384
PAGE_SIZE = 128

EPS = 1e-6
FFN_DIM = 2816
GLA_HEADS = 4
GLA_DK_TOTAL = D_MODEL // 2
GLA_DV_TOTAL = D_MODEL
GLA_DK = GLA_DK_TOTAL // GLA_HEADS
GLA_DV = GLA_DV_TOTAL // GLA_HEADS
GLA_GATE_RANK = 16
GLA_TAU = 16.0
GLA_CHUNK = 64
SSM_INNER = 2 * D_MODEL
SSM_HEADDIM = 64
SSM_HEADS = SSM_INNER // SSM_HEADDIM
SSM_GROUPS = 4
SSM_DSTATE = 128
SSM_CHUNK = 64
CONV_W = 4
CONV_DIM = SSM_INNER + 2 * SSM_GROUPS * SSM_DSTATE
IN_SPLITS = (GLA_DK_TOTAL, GLA_DK_TOTAL, GLA_DV_TOTAL, GLA_DV_TOTAL, GLA_GATE_RANK,
             SSM_INNER, CONV_DIM, SSM_HEADS, D_MODEL, D_MODEL)
IN_DIM = sum(IN_SPLITS)

kernel_name = "hybrid_gla_ssd_macaron_step"


def rmsnorm(x, w):
    xf = x.astype(jnp.float32)
    y = xf * lax.rsqrt(jnp.mean(xf * xf, axis=-1, keepdims=True) + EPS)
    return (y * w.astype(jnp.float32)).astype(x.dtype)


def swiglu(x, w_gu, w_down):
    g, u = jnp.split(x @ w_gu, 2, axis=-1)
    return (jax.nn.silu(g) * u) @ w_down


def gla_chunked(q, k, v, log_a, s0):
    b, L, H, _ = q.shape
    DV = v.shape[-1]
    C = min(GLA_CHUNK, L)
    n = -(-L // C)
    pad = n * C - L

    def chunks(t):
        t = jnp.pad(t.astype(jnp.float32), ((0, 0), (0, pad), (0, 0), (0, 0)))
        return t.reshape(b, n, C, H, t.shape[-1]).swapaxes(0, 1)

    qc = chunks(q) * (GLA_DK ** -0.5)
    kc, vc, ac = chunks(k), chunks(v), chunks(log_a)
    bcum = jnp.cumsum(ac, axis=2)
    blast = bcum[:, :, -1:]
    q_in = qc * jnp.exp(bcum)
    k_in = kc * jnp.exp(-bcum)
    k_out = kc * jnp.exp(blast - bcum)
    causal = jnp.tril(jnp.ones((C, C), dtype=bool))
    scores = jnp.where(causal, jnp.einsum('nbthd,nbshd->nbhts', q_in, k_in), 0.0)
    o_intra = jnp.einsum('nbhts,nbshv->nbthv', scores, vc)
    dS = jnp.einsum('nbshd,nbshv->nbhdv', k_out, vc)
    decay = jnp.exp(blast[:, :, 0])

    def step(S, inp):
        dS_c, dec_c = inp
        return dec_c[..., None] * S + dS_c, S

    S_fin, S_prev = lax.scan(step, s0.astype(jnp.float32), (dS, decay))
    o_inter = jnp.einsum('nbthd,nbhdv->nbthv', q_in, S_prev)
    o = (o_intra + o_inter).swapaxes(0, 1).reshape(b, n * C, H, DV)[:, :L]
    return o, S_fin


def ssd_chunked(x, dt, A, Bm, Cm, h0):
    b, L, H, P = x.shape
    G, N = Bm.shape[2], Bm.shape[3]
    hpg = H // G
    C = min(SSM_CHUNK, L)
    n = -(-L // C)
    pad = n * C - L

    def chunks(t):
        t = jnp.pad(t.astype(jnp.float32), ((0, 0), (0, pad)) + ((0, 0),) * (t.ndim - 2))
        return t.reshape(b, n, C, *t.shape[2:]).swapaxes(0, 1)

    xg = chunks(x.astype(jnp.float32) * dt[..., None]).reshape(n, b, C, G, hpg, P)
    acum = jnp.cumsum(chunks(dt * A).reshape(n, b, C, G, hpg), axis=2)
    Bc, Cc = chunks(Bm), chunks(Cm)
    causal = jnp.tril(jnp.ones((C, C), dtype=bool))[:, :, None, None]
    seg = jnp.exp(jnp.where(causal, acum[:, :, :, None] - acum[:, :, None], -jnp.inf))
    cb = jnp.einsum('nbtgd,nbsgd->nbtsg', Cc, Bc)
    y_diag = jnp.einsum('nbtsg,nbtsgh,nbsghp->nbtghp', cb, seg, xg)
    dS = jnp.einsum('nbsgd,nbsgh,nbsghp->nbghpd', Bc, jnp.exp(acum[:, :, -1:] - acum), xg)
    decay = jnp.exp(acum[:, :, -1])

    def step(h, inp):
        dS_c, dec_c = inp
        return dec_c[..., None, None] * h + dS_c, h

    h_fin, h_prev = lax.scan(step, h0.astype(jnp.float32).reshape(b, G, hpg, P, N), (dS, decay))
    y_off = jnp.einsum('nbtgd,nbtgh,nbghpd->nbtghp', Cc, jnp.exp(acum), h_prev)
    y = (y_diag + y_off).swapaxes(0, 1).reshape(b, n * C, H, P)[:, :L]
    return y, h_fin.reshape(b, H, P, N)


def causal_dwconv(xbc, conv_state, w, bias):
    L = xbc.shape[1]
    full = jnp.concatenate([conv_state.astype(xbc.dtype), xbc], axis=1)
    y = bias + sum(full[:, i:i + L] * w[i] for i in range(CONV_W))
    return y, full[:, -(CONV_W - 1):]


def mixer(u, gla0, ssm0, conv0, w_in, w_gla_a2, b_gla_a2, gla_norm, conv_w, conv_b,
          dt_bias, a_log, d_skip, ssm_norm, w_proj_gla, w_proj_ssm, w_out):
    b, L, _ = u.shape
    offs = np.cumsum(IN_SPLITS)[:-1].tolist()
    q, k, v, g, a_lr, z, xbc, dt_raw, gate_a, gate_b = jnp.split(u @ w_in, offs, axis=-1)
    log_a = jax.nn.log_sigmoid((a_lr @ w_gla_a2 + b_gla_a2).astype(jnp.float32)) / GLA_TAU
    o_gla, gla_new = gla_chunked(q.reshape(b, L, GLA_HEADS, GLA_DK),
                                 k.reshape(b, L, GLA_HEADS, GLA_DK),
                                 v.reshape(b, L, GLA_HEADS, GLA_DV),
                                 log_a.reshape(b, L, GLA_HEADS, GLA_DK), gla0)
    o_gla = rmsnorm(o_gla, gla_norm.reshape(GLA_HEADS, GLA_DV)).reshape(b, L, GLA_DV_TOTAL)
    o_gla = (o_gla * jax.nn.silu(g.astype(jnp.float32))).astype(u.dtype)
    xbc_c, conv_new = causal_dwconv(xbc, conv0, conv_w, conv_b)
    xbc_c = jax.nn.silu(xbc_c)
    xs, Bm, Cm = jnp.split(xbc_c, [SSM_INNER, SSM_INNER + SSM_GROUPS * SSM_DSTATE], axis=-1)
    xs = xs.reshape(b, L, SSM_HEADS, SSM_HEADDIM)
    dt = jax.nn.softplus((dt_raw + dt_bias).astype(jnp.float32))
    A = -jnp.exp(a_log.astype(jnp.float32))
    y, ssm_new = ssd_chunked(xs, dt, A,
                             Bm.reshape(b, L, SSM_GROUPS, SSM_DSTATE),
                             Cm.reshape(b, L, SSM_GROUPS, SSM_DSTATE), ssm0)
    y = (y + d_skip.astype(jnp.float32)[:, None] * xs.astype(jnp.float32)).reshape(b, L, SSM_INNER)
    y = rmsnorm(y * jax.nn.silu(z.astype(jnp.float32)), ssm_norm).astype(u.dtype)
    m = jax.nn.sigmoid(gate_a) * (o_gla @ w_proj_gla) + jax.nn.sigmoid(gate_b) * (y @ w_proj_ssm)
    return m @ w_out, gla_new, ssm_new, conv_new


def trunk(x, gla0, ssm0, conv0, params):
    (norm_ffn1, w_ffn1_gu, w_ffn1_down, norm_mix, w_in, w_gla_a2, b_gla_a2, gla_norm,
     conv_w, conv_b, dt_bias, a_log, d_skip, ssm_norm, w_proj_gla, w_proj_ssm, w_out,
     norm_ffn2, w_ffn2_gu, w_ffn2_down, norm_final) = params
    glas, ssms, convs = [], [], []
    for l in range(DEPTH):
        x = x + 0.5 * swiglu(rmsnorm(x, norm_ffn1[l]), w_ffn1_gu[l], w_ffn1_down[l])
        m, g_s, s_s, c_s = mixer(rmsnorm(x, norm_mix[l]), gla0[l], ssm0[l], conv0[l],
                                 w_in[l], w_gla_a2[l], b_gla_a2[l], gla_norm[l], conv_w[l], conv_b[l],
                                 dt_bias[l], a_log[l], d_skip[l], ssm_norm[l],
                                 w_proj_gla[l], w_proj_ssm[l], w_out[l])
        x = x + m
        x = x + 0.5 * swiglu(rmsnorm(x, norm_ffn2[l]), w_ffn2_gu[l], w_ffn2_down[l])
        glas.append(g_s)
        ssms.append(s_s)
        convs.append(c_s)
    return rmsnorm(x, norm_final), jnp.stack(glas), jnp.stack(ssms), jnp.stack(convs)


def setup_inputs(seed: int = 0) -> dict:
    key = jax.random.key(seed)
    ks = iter(jax.random.split(key, 32))

    def nrm(shape, scale):
        return jax.random.normal(next(ks), shape, jnp.float32) * scale

    def gain(shape):
        return 1.0 + nrm(shape, 0.02)

    Ld = DEPTH
    dt0 = jnp.exp(jax.random.uniform(next(ks), (Ld, SSM_HEADS), jnp.float32,
                                     np.log(1e-3), np.log(1e-1)))
    dt_bias = dt0 + jnp.log(-jnp.expm1(-dt0))
    a_log = jnp.log(jax.random.uniform(next(ks), (Ld, SSM_HEADS), jnp.float32, 1.0, 16.0))
    return {
        "x_prompt": nrm((BATCH, SEQ, D_MODEL), 1.0),
        "x_sample": nrm((DEC_BATCH, DEC_SEQ, D_MODEL), 1.0),
        "state_gla": nrm((Ld, DEC_BATCH, GLA_HEADS, GLA_DK, GLA_DV), 1.0),
        "state_ssm": nrm((Ld, DEC_BATCH, SSM_HEADS, SSM_HEADDIM, SSM_DSTATE), 0.1),
        "state_conv": nrm((Ld, DEC_BATCH, CONV_W - 1, CONV_DIM), 1.0),
        "norm_ffn1": gain((Ld, D_MODEL)),
        "w_ffn1_gu": nrm((Ld, D_MODEL, 2 * FFN_DIM), D_MODEL ** -0.5),
        "w_ffn1_down": nrm((Ld, FFN_DIM, D_MODEL), FFN_DIM ** -0.5),
        "norm_mix": gain((Ld, D_MODEL)),
        "w_in": nrm((Ld, D_MODEL, IN_DIM), D_MODEL ** -0.5),
        "w_gla_a2": nrm((Ld, GLA_GATE_RANK, GLA_DK_TOTAL), GLA_GATE_RANK ** -0.5),
        "b_gla_a2": nrm((Ld, GLA_DK_TOTAL), 0.1),
        "gla_norm": gain((Ld, GLA_DV_TOTAL)),
        "conv_w": nrm((Ld, CONV_W, CONV_DIM), CONV_W ** -0.5),
        "conv_b": nrm((Ld, CONV_DIM), 0.02),
        "dt_bias": dt_bias,
        "a_log": a_log,
        "d_skip": gain((Ld, SSM_HEADS)),
        "ssm_norm": gain((Ld, SSM_INNER)),
        "w_proj_gla": nrm((Ld, GLA_DV_TOTAL, D_MODEL), GLA_DV_TOTAL ** -0.5),
        "w_proj_ssm": nrm((Ld, SSM_INNER, D_MODEL), SSM_INNER ** -0.5),
        "w_out": nrm((Ld, D_MODEL, D_MODEL), D_MODEL ** -0.5),
        "norm_ffn2": gain((Ld, D_MODEL)),
        "w_ffn2_gu": nrm((Ld, D_MODEL, 2 * FFN_DIM), D_MODEL ** -0.5),
        "w_ffn2_down": nrm((Ld, FFN_DIM, D_MODEL), FFN_DIM ** -0.5),
        "norm_final": gain((D_MODEL,)),
    }


def reference(x_prompt, x_sample, state_gla, state_ssm, state_conv,
              norm_ffn1, w_ffn1_gu, w_ffn1_down, norm_mix, w_in, w_gla_a2, b_gla_a2, gla_norm,
              conv_w, conv_b, dt_bias, a_log, d_skip, ssm_norm, w_proj_gla, w_proj_ssm, w_out,
              norm_ffn2, w_ffn2_gu, w_ffn2_down, norm_final):
    params = (norm_ffn1, w_ffn1_gu, w_ffn1_down, norm_mix, w_in, w_gla_a2, b_gla_a2, gla_norm,
              conv_w, conv_b, dt_bias, a_log, d_skip, ssm_norm, w_proj_gla, w_proj_ssm, w_out,
              norm_ffn2, w_ffn2_gu, w_ffn2_down, norm_final)
    bp = x_prompt.shape[0]
    gla0 = jnp.zeros((DEPTH, bp, GLA_HEADS, GLA_DK, GLA_DV), jnp.float32)
    ssm0 = jnp.zeros((DEPTH, bp, SSM_HEADS, SSM_HEADDIM, SSM_DSTATE), jnp.float32)
    conv0 = jnp.zeros((DEPTH, bp, CONV_W - 1, CONV_DIM), x_prompt.dtype)
    y_prompt, gla_p, ssm_p, conv_p = trunk(x_prompt, gla0, ssm0, conv0, params)
    y_sample, gla_s, ssm_s, conv_s = trunk(x_sample, state_gla, state_ssm, state_conv, params)
    return (y_prompt, y_sample, gla_p, ssm_p, conv_p, gla_s, ssm_s, conv_s)
```

```python
import functools

import numpy as np
import jax
import jax.numpy as jnp
from jax import lax
from jax.experimental import pallas as pl
from jax.experimental.pallas import tpu as pltpu

F32 = jnp.float32
BF16 = jnp.bfloat16
HI = lax.Precision.HIGHEST

D_MODEL = 1024
DEPTH = 2
EPS = 1e-6
FFN_DIM = 2816
GLA_HEADS = 4
GLA_DK = 128
GLA_DV = 256
GLA_DK_TOTAL = GLA_HEADS * GLA_DK
GLA_DV_TOTAL = GLA_HEADS * GLA_DV
GLA_GATE_RANK = 16
GLA_TAU = 16.0
SSM_INNER = 2048
SSM_HEADDIM = 64
SSM_HEADS = 32
SSM_GROUPS = 4
SSM_HPG = SSM_HEADS // SSM_GROUPS
SSM_DSTATE = 128
CONV_W = 4
CONV_DIM = SSM_INNER + 2 * SSM_GROUPS * SSM_DSTATE
CHUNK = 64

LANES = 128
FFN_TILE = 256
GROUP_LANES = SSM_HPG * SSM_HEADDIM
VMEM_LIMIT = 56 * 1024 * 1024

NT_DIMS = (((1,), (1,)), ((), ()))
TN_DIMS = (((0,), (0,)), ((), ()))


def _dot(a, b, precision=None):
    return jnp.dot(a, b, preferred_element_type=F32, precision=precision)


def _dot_nt(a, b):
    return lax.dot_general(a, b, NT_DIMS, preferred_element_type=F32)


def _dot_tn(a, b, precision=None):
    return lax.dot_general(a, b, TN_DIMS, preferred_element_type=F32, precision=precision)


def _rms(x):
    return x * lax.rsqrt(jnp.mean(x * x, axis=-1, keepdims=True) + EPS)


def _sigmoid(x):
    return 1.0 / (1.0 + jnp.exp(-x))


def _silu(x):
    return x * _sigmoid(x)


def _softplus(x):
    return jnp.maximum(x, 0.0) + jnp.log1p(jnp.exp(-jnp.abs(x)))


def _params(n_parallel=0, n_arbitrary=1):
    sem = ("parallel",) * n_parallel + ("arbitrary",) * n_arbitrary
    return pltpu.CompilerParams(dimension_semantics=sem, vmem_limit_bytes=VMEM_LIMIT)


def _const_spec(shape):
    nd = len(shape)
    return pl.BlockSpec(shape, lambda *_: (0,) * nd)


def _ffn_kernel(x_ref, nw_ref, wgu_ref, wd_ref, fw_ref, o_ref, acc_ref, *, final):
    x = x_ref[...]
    xn = (_rms(x) * nw_ref[...]).astype(BF16)
    for c in range(FFN_DIM // FFN_TILE):
        lo = c * FFN_TILE
        g = _dot(xn, wgu_ref[:, lo:lo + FFN_TILE])
        u = _dot(xn, wgu_ref[:, FFN_DIM + lo:FFN_DIM + lo + FFN_TILE])
        a = (_silu(g) * u).astype(BF16)
        part = _dot(a, wd_ref[lo:lo + FFN_TILE, :])
        if c == 0:
            acc_ref[...] = part
        else:
            acc_ref[...] += part
    y = x + 0.5 * acc_ref[...]
    if final:
        y = _rms(y) * fw_ref[...]
    o_ref[...] = y


def _ffn(x, nw, wgu, wd, fw, *, tm, final):
    t = x.shape[0]
    return pl.pallas_call(
        functools.partial(_ffn_kernel, final=final),
        grid=(t // tm,),
        in_specs=[pl.BlockSpec((tm, D_MODEL), lambda i: (i, 0)),
                  _const_spec((1, D_MODEL)),
                  _const_spec((D_MODEL, 2 * FFN_DIM)),
                  _const_spec((FFN_DIM, D_MODEL)),
                  _const_spec((1, D_MODEL))],
        out_specs=pl.BlockSpec((tm, D_MODEL), lambda i: (i, 0)),
        out_shape=jax.ShapeDtypeStruct((t, D_MODEL), F32),
        scratch_shapes=[pltpu.VMEM((tm, D_MODEL), F32)],
        compiler_params=_params(),
        name="ffn_final" if final else "ffn",
    )(x, nw, wgu, wd, fw)


GLA_IN_COLS = 2 * GLA_DK_TOTAL + 4 * GLA_DV_TOTAL + LANES


def _gla_in_kernel(x_ref, nw_ref, w_ref, wa2_ref, ba2_ref,
                   q_ref, k_ref, v_ref, sg_ref, ga_ref, gb_ref, la_ref):
    u = (_rms(x_ref[...]) * nw_ref[...]).astype(BF16)

    def proj(lo, width):
        return _dot(u, w_ref[:, lo:lo + width])

    q_ref[...] = proj(0, 512).astype(BF16)
    k_ref[...] = proj(512, 512).astype(BF16)
    v_ref[...] = proj(1024, 1024).astype(BF16)
    sg_ref[...] = _silu(proj(2048, 1024)).astype(BF16)
    ga_ref[...] = _sigmoid(proj(3072, 1024)).astype(BF16)
    gb_ref[...] = _sigmoid(proj(4096, 1024)).astype(BF16)
    a_lr = proj(5120, LANES).astype(BF16)
    z = _dot(a_lr, wa2_ref[...]) + ba2_ref[...]
    la_ref[...] = -_softplus(-z) / GLA_TAU


def _gla_in(x, nw, w, wa2, ba2, *, tm):
    t = x.shape[0]
    row = lambda width: pl.BlockSpec((tm, width), lambda i: (i, 0))
    out = lambda width, dt: jax.ShapeDtypeStruct((t, width), dt)
    return pl.pallas_call(
        _gla_in_kernel,
        grid=(t // tm,),
        in_specs=[row(D_MODEL), _const_spec((1, D_MODEL)), _const_spec((D_MODEL, GLA_IN_COLS)),
                  _const_spec((LANES, GLA_DK_TOTAL)), _const_spec((1, GLA_DK_TOTAL))],
        out_specs=[row(512), row(512), row(1024), row(1024), row(1024), row(1024), row(512)],
        out_shape=[out(512, BF16), out(512, BF16), out(1024, BF16), out(1024, BF16),
                   out(1024, BF16), out(1024, BF16), out(512, F32)],
        compiler_params=_params(),
        name="gla_in",
    )(x, nw, w, wa2, ba2)


SSM_IN_COLS = SSM_INNER + CONV_DIM + LANES


def _ssm_in_prompt_kernel(x_ref, nw_ref, w_ref, dtb_ref, cw_ref, cb_ref,
                          sz_ref, xa_ref, dt_ref, tail_ref, ext_ref, *, tm):
    j = pl.program_id(1)

    @pl.when(j == 0)
    def _():
        ext_ref[0:8, :] = jnp.zeros((8, CONV_DIM), F32)

    u = (_rms(x_ref[...]) * nw_ref[...]).astype(BF16)
    sz_ref[...] = _silu(_dot(u, w_ref[:, 0:SSM_INNER])).astype(BF16)
    dt_ref[...] = _softplus(_dot(u, w_ref[:, SSM_INNER + CONV_DIM:SSM_IN_COLS]) + dtb_ref[...])
    ext_ref[8:8 + tm, :] = _dot(u, w_ref[:, SSM_INNER:SSM_INNER + CONV_DIM])
    acc = cb_ref[...] + ext_ref[8:8 + tm, :] * cw_ref[CONV_W - 1:CONV_W, :]
    for i in range(CONV_W - 1):
        back = CONV_W - 1 - i
        acc = acc + ext_ref[8 - back:8 - back + tm, :] * cw_ref[i:i + 1, :]
    xa_ref[...] = _silu(acc).astype(BF16)
    tail_ref[0] = ext_ref[tm + 8 - (CONV_W - 1):tm + 8, :]
    ext_ref[0:8, :] = ext_ref[tm:tm + 8, :]


def _ssm_in_prompt(x, nw, w, dtb, cw, cb, *, nb, tm):
    t = x.shape[0]
    nj = t // nb // tm
    row = lambda width: pl.BlockSpec((tm, width), lambda b, j: (b * nj + j, 0))
    return pl.pallas_call(
        functools.partial(_ssm_in_prompt_kernel, tm=tm),
        grid=(nb, nj),
        in_specs=[row(D_MODEL), _const_spec((1, D_MODEL)), _const_spec((D_MODEL, SSM_IN_COLS)),
                  _const_spec((1, LANES)), _const_spec((CONV_W, CONV_DIM)), _const_spec((1, CONV_DIM))],
        out_specs=[row(SSM_INNER), row(CONV_DIM), row(LANES),
                   pl.BlockSpec((1, CONV_W - 1, CONV_DIM), lambda b, j: (b, 0, 0))],
        out_shape=[jax.ShapeDtypeStruct((t, SSM_INNER), BF16),
                   jax.ShapeDtypeStruct((t, CONV_DIM), BF16),
                   jax.ShapeDtypeStruct((t, LANES), F32),
                   jax.ShapeDtypeStruct((nb, CONV_W - 1, CONV_DIM), F32)],
        scratch_shapes=[pltpu.VMEM((tm + 8, CONV_DIM), F32)],
        compiler_params=_params(n_arbitrary=2),
        name="ssm_in_prompt",
    )(x, nw, w, dtb, cw, cb)


def _ssm_in_sample_kernel(x_ref, nw_ref, w_ref, dtb_ref, sz_ref, xr_ref, dt_ref):
    u = (_rms(x_ref[...]) * nw_ref[...]).astype(BF16)
    sz_ref[...] = _silu(_dot(u, w_ref[:, 0:SSM_INNER])).astype(BF16)
    xr_ref[...] = _dot(u, w_ref[:, SSM_INNER:SSM_INNER + CONV_DIM])
    dt_ref[...] = _softplus(_dot(u, w_ref[:, SSM_INNER + CONV_DIM:SSM_IN_COLS]) + dtb_ref[...])


def _ssm_in_sample(x, nw, w, dtb, *, tm):
    t = x.shape[0]
    row = lambda width: pl.BlockSpec((tm, width), lambda i: (i, 0))
    return pl.pallas_call(
        _ssm_in_sample_kernel,
        grid=(t // tm,),
        in_specs=[row(D_MODEL), _const_spec((1, D_MODEL)), _const_spec((D_MODEL, SSM_IN_COLS)),
                  _const_spec((1, LANES))],
        out_specs=[row(SSM_INNER), row(CONV_DIM), row(LANES)],
        out_shape=[jax.ShapeDtypeStruct((t, SSM_INNER), BF16),
                   jax.ShapeDtypeStruct((t, CONV_DIM), F32),
                   jax.ShapeDtypeStruct((t, LANES), F32)],
        compiler_params=_params(),
        name="ssm_in_sample",
    )(x, nw, w, dtb)


def _gla_rows(q, k, la, bcum, blast):
    qf = q.astype(F32) * (GLA_DK ** -0.5)
    kf = k.astype(F32)
    q_in = (qf * jnp.exp(bcum)).astype(BF16)
    k_in = (kf * jnp.exp(-bcum)).astype(BF16)
    k_out = (kf * jnp.exp(blast - bcum)).astype(BF16)
    return q_in, k_in, k_out


def _ssd_group_diag(cg, bg, acg, xgb_g, ones_rr, itile, cmask, bmask):
    arow = _dot(ones_rr, acg * itile, HI)
    seg = jnp.where(cmask > 0.5, jnp.exp(acg - arow), 0.0)
    cbx = _dot_nt(cg, jnp.concatenate([bg] * SSM_HPG, axis=0))
    w = (cbx * seg).astype(BF16)
    xblk = jnp.concatenate([xgb_g] * SSM_HPG, axis=0) * bmask
    return _dot(w, xblk)


def _core_prompt_kernel(q_ref, k_ref, v_ref, la_ref, xa_ref, dt_ref, alog_ref, dskip_ref,
                        tril_ref, ex_ref, itile_ref, cmask_ref, bmask_ref,
                        o_ref, y_ref, gla_ref, ssm_ref, s_ref, h_ref, *, rows, nj):
    j = pl.program_id(1)

    @pl.when(j == 0)
    def _():
        s_ref[...] = jnp.zeros(s_ref.shape, F32)
        h_ref[...] = jnp.zeros(h_ref.shape, F32)

    tril = tril_ref[...]
    causal = tril > 0.5
    ones_cc = jnp.ones((CHUNK, CHUNK), F32)
    ones_cl = jnp.ones((CHUNK, LANES), F32)
    a_x = -jnp.exp(alog_ref[...])
    itile = itile_ref[...]
    cmask = cmask_ref[...]
    bmask = bmask_ref[...]

    for c in range(rows // CHUNK):
        r = slice(c * CHUNK, (c + 1) * CHUNK)
        la = la_ref[r, :]
        bcum = _dot(tril, la, HI)
        blast = bcum[CHUNK - 1:CHUNK, :]
        q_in, k_in, k_out = _gla_rows(q_ref[r, :], k_ref[r, :], la, bcum, blast)
        v = v_ref[r, :]
        for h in range(GLA_HEADS):
            kl = slice(h * GLA_DK, (h + 1) * GLA_DK)
            vl = slice(h * GLA_DV, (h + 1) * GLA_DV)
            sc = jnp.where(causal, _dot_nt(q_in[:, kl], k_in[:, kl]), 0.0).astype(BF16)
            s_old = s_ref[h]
            o_h = _dot(sc, v[:, vl]) + _dot(q_in[:, kl], s_old.astype(BF16))
            o_ref[r, vl] = o_h.astype(BF16)
            ds = _dot_tn(k_out[:, kl], v[:, vl])
            dec = jnp.exp(_dot_tn(la[:, kl], ones_cl, HI))
            s_ref[h] = s_old * jnp.concatenate([dec, dec], axis=1) + ds
        dtx = _dot(dt_ref[r, :], ex_ref[...], HI)
        dax = dtx * a_x
        acx = _dot(tril, dax, HI)
        alast = acx[CHUNK - 1:CHUNK, :]
        xa = xa_ref[r, :]
        xs = xa[:, 0:SSM_INNER].astype(F32)
        xg = xs * dtx
        xgb = xg.astype(BF16)
        xw = (xg * jnp.exp(alast - acx)).astype(BF16)
        eac = jnp.exp(acx)
        for g in range(SSM_GROUPS):
            gl = slice(g * GROUP_LANES, (g + 1) * GROUP_LANES)
            bg = xa[:, SSM_INNER + g * SSM_DSTATE:SSM_INNER + (g + 1) * SSM_DSTATE]
            cg = xa[:, SSM_INNER + (SSM_GROUPS + g) * SSM_DSTATE:SSM_INNER + (SSM_GROUPS + g + 1) * SSM_DSTATE]
            yd = _ssd_group_diag(cg, bg, acx[:, gl], xgb[:, gl], ones_cc, itile, cmask, bmask)
            h_old = h_ref[gl, :]
            yo = _dot_nt(cg, h_old.astype(BF16)) * eac[:, gl]
            y = yd + yo + dskip_ref[:, gl] * xs[:, gl]
            y_ref[r, gl] = y.astype(BF16)
            ds = _dot_tn(xw[:, gl], bg)
            dec = jnp.exp(_dot_tn(dax[:, gl], ones_cl, HI))
            h_ref[gl, :] = h_old * dec + ds

    @pl.when(j == nj - 1)
    def _():
        gla_ref[0] = s_ref[...]
        ssm_ref[0] = h_ref[...]


def _core_prompt(q, k, v, la, xa, dt, alog_x, dskip_x, consts, *, nb, rows):
    t = q.shape[0]
    nj = t // nb // rows
    row = lambda width: pl.BlockSpec((rows, width), lambda b, j: (b * nj + j, 0))
    tril, ex, itile, cmask, bmask = consts
    return pl.pallas_call(
        functools.partial(_core_prompt_kernel, rows=rows, nj=nj),
        grid=(nb, nj),
        in_specs=[row(512), row(512), row(1024), row(512), row(CONV_DIM), row(LANES),
                  _const_spec((1, SSM_INNER)), _const_spec((1, SSM_INNER)),
                  _const_spec(tril.shape), _const_spec(ex.shape), _const_spec(itile.shape),
                  _const_spec(cmask.shape), _const_spec(bmask.shape)],
        out_specs=[row(GLA_DV_TOTAL), row(SSM_INNER),
                   pl.BlockSpec((1, GLA_HEADS, GLA_DK, GLA_DV), lambda b, j: (b, 0, 0, 0)),
                   pl.BlockSpec((1, SSM_INNER, SSM_DSTATE), lambda b, j: (b, 0, 0))],
        out_shape=[jax.ShapeDtypeStruct((t, GLA_DV_TOTAL), BF16),
                   jax.ShapeDtypeStruct((t, SSM_INNER), BF16),
                   jax.ShapeDtypeStruct((nb, GLA_HEADS, GLA_DK, GLA_DV), F32),
                   jax.ShapeDtypeStruct((nb, SSM_INNER, SSM_DSTATE), F32)],
        scratch_shapes=[pltpu.VMEM((GLA_HEADS, GLA_DK, GLA_DV), F32),
                        pltpu.VMEM((SSM_INNER, SSM_DSTATE), F32)],
        compiler_params=_params(n_arbitrary=2),
        name="core_prompt",
    )(q, k, v, la, xa, dt, alog_x, dskip_x, tril, ex, itile, cmask, bmask)


SEQ_PER_STEP = 4
SAMPLE_LEN = 4


def _core_sample_kernel(q_ref, k_ref, v_ref, la_ref, xr_ref, c4_ref, dt_ref, s0_ref, h0_ref,
                        cw_ref, cb_ref, alog_ref, alogp_ref, dskip_ref,
                        btril_ref, same_ref, shp_ref, shq_ref, ex_ref, es_ref,
                        itile_ref, cmask_ref, bmask_ref, qmask_ref, smask_ref, rowseq_ref,
                        o_ref, y_ref, s1_ref, h1_ref):
    rs = SEQ_PER_STEP * SAMPLE_LEN
    btril = btril_ref[...]
    same = same_ref[...]
    ones_rr = jnp.ones((rs, rs), F32)
    ones_rl = jnp.ones((rs, LANES), F32)
    xr = xr_ref[...]
    c4 = c4_ref[...]
    acc = cb_ref[...] + xr * cw_ref[CONV_W - 1:CONV_W, :]
    for i in range(CONV_W - 1):
        back = CONV_W - 1 - i
        shifted = _dot(shp_ref[back - 1], xr, HI) + _dot(shq_ref[back - 1], c4, HI)
        acc = acc + shifted * cw_ref[i:i + 1, :]
    xa = _silu(acc)
    la = la_ref[...]
    bcum = _dot(btril, la, HI)
    blast = _dot(same, la, HI)
    q_in, k_in, k_out = _gla_rows(q_ref[...], k_ref[...], la, bcum, blast)
    v = v_ref[...]
    qmask = qmask_ref[...]
    for h in range(GLA_HEADS):
        kl = slice(h * GLA_DK, (h + 1) * GLA_DK)
        vl = slice(h * GLA_DV, (h + 1) * GLA_DV)
        sc = jnp.where(btril > 0.5, _dot_nt(q_in[:, kl], k_in[:, kl]), 0.0).astype(BF16)
        s_old = s0_ref[:, h].reshape(SEQ_PER_STEP * GLA_DK, GLA_DV)
        q_blk = jnp.concatenate([q_in[:, kl]] * SEQ_PER_STEP, axis=1) * qmask.astype(BF16)
        o_h = _dot(sc, v[:, vl]) + _dot(q_blk, s_old.astype(BF16))
        o_ref[:, vl] = o_h.astype(BF16)
        k_blk = jnp.concatenate([k_out[:, kl]] * SEQ_PER_STEP, axis=1) * qmask.astype(BF16)
        ds = _dot_tn(k_blk, v[:, vl])
        la_blk = jnp.concatenate([la[:, kl]] * SEQ_PER_STEP, axis=1) * qmask
        dec = jnp.exp(_dot_tn(la_blk, ones_rl, HI))
        s_new = s_old * jnp.concatenate([dec, dec], axis=1) + ds
        s1_ref[:, h] = s_new.reshape(SEQ_PER_STEP, GLA_DK, GLA_DV)
    dt = dt_ref[...]
    dtx = _dot(dt, ex_ref[...], HI)
    dax = dtx * (-jnp.exp(alog_ref[...]))
    acx = _dot(btril, dax, HI)
    alast = _dot(same, dax, HI)
    xs = xa[:, 0:SSM_INNER]
    xg = xs * dtx
    xgb = xg.astype(BF16)
    xw = xg * jnp.exp(alast - acx)
    eac = jnp.exp(acx)
    acs = _dot(_dot(btril, dt * (-jnp.exp(alogp_ref[...])), HI), es_ref[...], HI)
    smask = smask_ref[...]
    rowseq = rowseq_ref[...]
    sl = SSM_HPG * rs
    for g in range(SSM_GROUPS):
        gl = slice(g * GROUP_LANES, (g + 1) * GROUP_LANES)
        bg = xa[:, SSM_INNER + g * SSM_DSTATE:SSM_INNER + (g + 1) * SSM_DSTATE].astype(BF16)
        cg = xa[:, SSM_INNER + (SSM_GROUPS + g) * SSM_DSTATE:
                SSM_INNER + (SSM_GROUPS + g + 1) * SSM_DSTATE].astype(BF16)
        yd = _ssd_group_diag(cg, bg, acs[:, g * sl:(g + 1) * sl], xgb[:, gl], ones_rr,
                             itile_ref[...], cmask_ref[...], bmask_ref[...])
        yo = jnp.zeros((rs, GROUP_LANES), F32)
        for s in range(SEQ_PER_STEP):
            c_s = jnp.where(rowseq == s, cg, jnp.zeros_like(cg))
            yo = yo + _dot_nt(c_s, h0_ref[s, gl, :].astype(BF16))
        y = yd + yo * eac[:, gl] + dskip_ref[:, gl] * xs[:, gl]
        y_ref[:, gl] = y.astype(BF16)
        xw_blk = (jnp.concatenate([xw[:, gl]] * SEQ_PER_STEP, axis=1) * smask).astype(BF16)
        ds = _dot_tn(xw_blk, bg)
        da_blk = jnp.concatenate([dax[:, gl]] * SEQ_PER_STEP, axis=1) * smask
        dec = jnp.exp(_dot_tn(da_blk, ones_rl, HI))
        for s in range(SEQ_PER_STEP):
            sr = slice(s * GROUP_LANES, (s + 1) * GROUP_LANES)
            h1_ref[s, gl, :] = h0_ref[s, gl, :] * dec[sr, :] + ds[sr, :]


def _core_sample(q, k, v, la, xr, c4, dt, s0, h0, cw, cb, alog_x, alog_p, dskip_x, consts):
    t = q.shape[0]
    rs = SEQ_PER_STEP * SAMPLE_LEN
    nseq = t // SAMPLE_LEN
    row = lambda width: pl.BlockSpec((rs, width), lambda i: (i, 0))
    s_spec = pl.BlockSpec((SEQ_PER_STEP, GLA_HEADS, GLA_DK, GLA_DV), lambda i: (i, 0, 0, 0))
    h_spec = pl.BlockSpec((SEQ_PER_STEP, SSM_INNER, SSM_DSTATE), lambda i: (i, 0, 0))
    return pl.pallas_call(
        _core_sample_kernel,
        grid=(nseq // SEQ_PER_STEP,),
        in_specs=[row(512), row(512), row(1024), row(512), row(CONV_DIM), row(CONV_DIM), row(LANES),
                  s_spec, h_spec,
                  _const_spec((CONV_W, CONV_DIM)), _const_spec((1, CONV_DIM)),
                  _const_spec((1, SSM_INNER)), _const_spec((1, LANES)), _const_spec((1, SSM_INNER))]
                 + [_const_spec(c.shape) for c in consts],
        out_specs=[row(GLA_DV_TOTAL), row(SSM_INNER), s_spec, h_spec],
        out_shape=[jax.ShapeDtypeStruct((t, GLA_DV_TOTAL), BF16),
                   jax.ShapeDtypeStruct((t, SSM_INNER), BF16),
                   jax.ShapeDtypeStruct(s0.shape, F32),
                   jax.ShapeDtypeStruct(h0.shape, F32)],
        compiler_params=_params(),
        name="core_sample",
    )(q, k, v, la, xr, c4, dt, s0, h0, cw, cb, alog_x, alog_p, dskip_x, *consts)


def _merge_kernel(x_ref, o_ref, sg_ref, y_ref, sz_ref, ga_ref, gb_ref,
                  gn_ref, sn_ref, wpg_ref, wps_ref, wo_ref, out_ref):
    o = o_ref[...].astype(F32)
    parts = []
    for h in range(GLA_HEADS):
        vl = slice(h * GLA_DV, (h + 1) * GLA_DV)
        parts.append(_rms(o[:, vl]) * gn_ref[:, vl])
    og = (jnp.concatenate(parts, axis=1) * sg_ref[...].astype(F32)).astype(BF16)
    yz = y_ref[...].astype(F32) * sz_ref[...].astype(F32)
    yn = (_rms(yz) * sn_ref[...]).astype(BF16)
    m = (ga_ref[...].astype(F32) * _dot(og, wpg_ref[...])
         + gb_ref[...].astype(F32) * _dot(yn, wps_ref[...]))
    out_ref[...] = x_ref[...] + _dot(m.astype(BF16), wo_ref[...])


def _merge(x, o, sg, y, sz, ga, gb, gn, sn, wpg, wps, wo, *, tm):
    t = x.shape[0]
    row = lambda width: pl.BlockSpec((tm, width), lambda i: (i, 0))
    return pl.pallas_call(
        _merge_kernel,
        grid=(t // tm,),
        in_specs=[row(D_MODEL), row(1024), row(1024), row(2048), row(2048), row(1024), row(1024),
                  _const_spec((1, GLA_DV_TOTAL)), _const_spec((1, SSM_INNER)),
                  _const_spec((GLA_DV_TOTAL, D_MODEL)), _const_spec((SSM_INNER, D_MODEL)),
                  _const_spec((D_MODEL, D_MODEL))],
        out_specs=row(D_MODEL),
        out_shape=jax.ShapeDtypeStruct((t, D_MODEL), F32),
        compiler_params=_params(),
        name="merge",
    )(x, o, sg, y, sz, ga, gb, gn, sn, wpg, wps, wo)


def _expand_table(width):
    t = np.zeros((LANES, SSM_HEADS * width), np.float32)
    for h in range(SSM_HEADS):
        t[h, h * width:(h + 1) * width] = 1.0
    return t


def _prompt_consts():
    c = CHUNK
    tril = np.tril(np.ones((c, c), np.float32))
    s_of_lane = np.arange(SSM_HPG * c) % c
    h_of_lane = np.arange(SSM_HPG * c) // c
    itile = (np.arange(c)[:, None] == s_of_lane[None, :]).astype(np.float32)
    cmask = (s_of_lane[None, :] <= np.arange(c)[:, None]).astype(np.float32)
    hp = np.arange(GROUP_LANES) // SSM_HEADDIM
    bmask = (h_of_lane[:, None] == hp[None, :]).astype(np.float32)
    return (jnp.asarray(tril), jnp.asarray(_expand_table(SSM_HEADDIM)), jnp.asarray(itile),
            jnp.asarray(cmask), jnp.asarray(bmask, dtype=BF16))


def _sample_consts():
    rs = SEQ_PER_STEP * SAMPLE_LEN
    seq = np.arange(rs) // SAMPLE_LEN
    pos = np.arange(rs) % SAMPLE_LEN
    same = (seq[:, None] == seq[None, :]).astype(np.float32)
    btril = same * (pos[None, :] <= pos[:, None])
    shp = np.zeros((CONV_W - 1, rs, rs), np.float32)
    shq = np.zeros((CONV_W - 1, rs, rs), np.float32)
    for back in range(1, CONV_W):
        for r in range(rs):
            if pos[r] >= back:
                shp[back - 1, r, r - back] = 1.0
            else:
                shq[back - 1, r, seq[r] * SAMPLE_LEN + SAMPLE_LEN + pos[r] - back] = 1.0
    s_of_lane = np.arange(SSM_HPG * rs) % rs
    h_of_lane = np.arange(SSM_HPG * rs) // rs
    itile = (np.arange(rs)[:, None] == s_of_lane[None, :]).astype(np.float32)
    cmask = btril[:, s_of_lane]
    hp = np.arange(GROUP_LANES) // SSM_HEADDIM
    bmask = (h_of_lane[:, None] == hp[None, :]).astype(np.float32)
    qmask = (seq[:, None] == (np.arange(SEQ_PER_STEP * GLA_DK) // GLA_DK)[None, :]).astype(np.float32)
    smask = (seq[:, None] == (np.arange(SEQ_PER_STEP * GROUP_LANES) // GROUP_LANES)[None, :]).astype(np.float32)
    rowseq = np.broadcast_to(seq[:, None], (rs, SSM_DSTATE)).astype(np.int32)
    return (jnp.asarray(btril), jnp.asarray(same), jnp.asarray(shp), jnp.asarray(shq),
            jnp.asarray(_expand_table(SSM_HEADDIM)), jnp.asarray(_expand_table(rs)),
            jnp.asarray(itile), jnp.asarray(cmask), jnp.asarray(bmask, dtype=BF16),
            jnp.asarray(qmask), jnp.asarray(smask), jnp.asarray(rowseq))


def _layer_weights(l, p):
    w_in = p["w_in"][l]
    offs = np.cumsum([0, GLA_DK_TOTAL, GLA_DK_TOTAL, GLA_DV_TOTAL, GLA_DV_TOTAL, GLA_GATE_RANK,
                      SSM_INNER, CONV_DIM, SSM_HEADS, D_MODEL, D_MODEL])
    q, k, v, g, a, z, xbc, dt, gate_a, gate_b = [w_in[:, offs[i]:offs[i + 1]] for i in range(10)]
    pad = lambda w: jnp.pad(w, ((0, 0), (0, LANES - w.shape[1])))
    row = lambda w: w.reshape(1, -1)
    return dict(
        w_gla=jnp.concatenate([q, k, v, g, gate_a, gate_b, pad(a)], axis=1).astype(BF16),
        w_ssm=jnp.concatenate([z, xbc, pad(dt)], axis=1).astype(BF16),
        wa2=jnp.pad(p["w_gla_a2"][l], ((0, LANES - GLA_GATE_RANK), (0, 0))).astype(BF16),
        ba2=row(p["b_gla_a2"][l]),
        dtb=jnp.pad(row(p["dt_bias"][l]), ((0, 0), (0, LANES - SSM_HEADS))),
        alog_x=row(jnp.repeat(p["a_log"][l], SSM_HEADDIM)),
        alog_p=jnp.pad(row(p["a_log"][l]), ((0, 0), (0, LANES - SSM_HEADS))),
        dskip_x=row(jnp.repeat(p["d_skip"][l], SSM_HEADDIM)),
        cw=p["conv_w"][l], cb=row(p["conv_b"][l]),
        gn=row(p["gla_norm"][l]), sn=row(p["ssm_norm"][l]),
        wpg=p["w_proj_gla"][l].astype(BF16), wps=p["w_proj_ssm"][l].astype(BF16),
        wo=p["w_out"][l].astype(BF16),
        n1=row(p["norm_ffn1"][l]), n2=row(p["norm_ffn2"][l]), nm=row(p["norm_mix"][l]),
        gu1=p["w_ffn1_gu"][l].astype(BF16), d1=p["w_ffn1_down"][l].astype(BF16),
        gu2=p["w_ffn2_gu"][l].astype(BF16), d2=p["w_ffn2_down"][l].astype(BF16),
    )


PROMPT_TM = 512
CORE_ROWS = 256


def _trunk_prompt(x, weights, fw, nb):
    pc = _prompt_consts()
    glas, ssms, convs = [], [], []
    for l, w in enumerate(weights):
        last = l == len(weights) - 1
        x = _ffn(x, w["n1"], w["gu1"], w["d1"], fw, tm=PROMPT_TM, final=False)
        q, k, v, sg, ga, gb, la = _gla_in(x, w["nm"], w["w_gla"], w["wa2"], w["ba2"], tm=PROMPT_TM)
        sz, xa, dt, tail = _ssm_in_prompt(x, w["nm"], w["w_ssm"], w["dtb"], w["cw"], w["cb"],
                                          nb=nb, tm=PROMPT_TM)
        o, y, s_fin, h_fin = _core_prompt(q, k, v, la, xa, dt, w["alog_x"], w["dskip_x"], pc,
                                          nb=nb, rows=CORE_ROWS)
        x = _merge(x, o, sg, y, sz, ga, gb, w["gn"], w["sn"], w["wpg"], w["wps"], w["wo"], tm=PROMPT_TM)
        x = _ffn(x, w["n2"], w["gu2"], w["d2"], fw, tm=PROMPT_TM, final=last)
        glas.append(s_fin)
        ssms.append(h_fin.reshape(nb, SSM_HEADS, SSM_HEADDIM, SSM_DSTATE))
        convs.append(tail)
    return x, jnp.stack(glas), jnp.stack(ssms), jnp.stack(convs)


def _trunk_sample(x, state_gla, state_ssm, state_conv, weights, fw):
    sc = _sample_consts()
    t = x.shape[0]
    nseq = t // SAMPLE_LEN
    glas, ssms, convs = [], [], []
    for l, w in enumerate(weights):
        last = l == len(weights) - 1
        x = _ffn(x, w["n1"], w["gu1"], w["d1"], fw, tm=t, final=False)
        q, k, v, sg, ga, gb, la = _gla_in(x, w["nm"], w["w_gla"], w["wa2"], w["ba2"], tm=t)
        sz, xr, dt = _ssm_in_sample(x, w["nm"], w["w_ssm"], w["dtb"], tm=t)
        c4 = jnp.pad(state_conv[l], ((0, 0), (SAMPLE_LEN - (CONV_W - 1), 0), (0, 0))).reshape(t, CONV_DIM)
        h0 = state_ssm[l].reshape(nseq, SSM_INNER, SSM_DSTATE)
        o, y, s1, h1 = _core_sample(q, k, v, la, xr, c4, dt, state_gla[l], h0, w["cw"], w["cb"],
                                    w["alog_x"], w["alog_p"], w["dskip_x"], sc)
        x = _merge(x, o, sg, y, sz, ga, gb, w["gn"], w["sn"], w["wpg"], w["wps"], w["wo"], tm=t)
        x = _ffn(x, w["n2"], w["gu2"], w["d2"], fw, tm=t, final=last)
        glas.append(s1)
        ssms.append(h1.reshape(nseq, SSM_HEADS, SSM_HEADDIM, SSM_DSTATE))
        convs.append(xr.reshape(nseq, SAMPLE_LEN, CONV_DIM)[:, SAMPLE_LEN - (CONV_W - 1):])
    return x, jnp.stack(glas), jnp.stack(ssms), jnp.stack(convs)


def kernel(x_prompt, x_sample, state_gla, state_ssm, state_conv, norm_ffn1, w_ffn1_gu, w_ffn1_down, norm_mix, w_in, w_gla_a2, b_gla_a2, gla_norm, conv_w, conv_b, dt_bias, a_log, d_skip, ssm_norm, w_proj_gla, w_proj_ssm, w_out, norm_ffn2, w_ffn2_gu, w_ffn2_down, norm_final):
    p = dict(norm_ffn1=norm_ffn1, w_ffn1_gu=w_ffn1_gu, w_ffn1_down=w_ffn1_down, norm_mix=norm_mix,
             w_in=w_in, w_gla_a2=w_gla_a2, b_gla_a2=b_gla_a2, gla_norm=gla_norm, conv_w=conv_w,
             conv_b=conv_b, dt_bias=dt_bias, a_log=a_log, d_skip=d_skip, ssm_norm=ssm_norm,
             w_proj_gla=w_proj_gla, w_proj_ssm=w_proj_ssm, w_out=w_out, norm_ffn2=norm_ffn2,
             w_ffn2_gu=w_ffn2_gu, w_ffn2_down=w_ffn2_down)
    depth = w_in.shape[0]
    weights = [_layer_weights(l, p) for l in range(depth)]
    fw = norm_final.reshape(1, D_MODEL)
    bp, lp, _ = x_prompt.shape
    bs, ls, _ = x_sample.shape
    assert ls == SAMPLE_LEN and lp % CORE_ROWS == 0 and lp % PROMPT_TM == 0 and bs % SEQ_PER_STEP == 0
    yp, gla_p, ssm_p, conv_p = _trunk_prompt(x_prompt.reshape(bp * lp, D_MODEL), weights, fw, bp)
    ys, gla_s, ssm_s, conv_s = _trunk_sample(x_sample.reshape(bs * ls, D_MODEL), state_gla, state_ssm,
                                             state_conv, weights, fw)
    return (yp.reshape(bp, lp, D_MODEL), ys.reshape(bs, ls, D_MODEL),
            gla_p, ssm_p, conv_p, gla_s, ssm_s, conv_s)
```

```python
import functools

import numpy as np
import jax
import jax.numpy as jnp
from jax import lax
from jax.experimental import pallas as pl
from jax.experimental.pallas import tpu as pltpu

F32 = jnp.float32
BF16 = jnp.bfloat16

D_MODEL = 1024
EPS = 1e-6
FFN_DIM = 2816
GLA_HEADS = 4
GLA_DK = 128
GLA_DV = 256
GLA_DK_TOTAL = GLA_HEADS * GLA_DK
GLA_DV_TOTAL = GLA_HEADS * GLA_DV
GLA_GATE_RANK = 16
GLA_TAU = 16.0
SSM_INNER = 2048
SSM_HEADDIM = 64
SSM_HEADS = 32
SSM_GROUPS = 4
SSM_HPG = SSM_HEADS // SSM_GROUPS
SSM_DSTATE = 128
CONV_W = 4
CONV_DIM = SSM_INNER + 2 * SSM_GROUPS * SSM_DSTATE
CHUNK = 64
IN_SPLITS = (GLA_DK_TOTAL, GLA_DK_TOTAL, GLA_DV_TOTAL, GLA_DV_TOTAL, GLA_GATE_RANK,
             SSM_INNER, CONV_DIM, SSM_HEADS, D_MODEL, D_MODEL)

LANES = 128
BF16_ROWS = 16
FFN_TILE = 256
GROUP_LANES = SSM_HPG * SSM_HEADDIM
PIECES = 3
VMEM_LIMIT = 56 * 1024 * 1024

NT_DIMS = (((1,), (1,)), ((), ()))
TN_DIMS = (((0,), (0,)), ((), ()))


def _dot(a, b):
    return jnp.dot(a, b, preferred_element_type=F32)


def _dot_nt(a, b):
    return lax.dot_general(a, b, NT_DIMS, preferred_element_type=F32)


def _dot_tn(a, b):
    return lax.dot_general(a, b, TN_DIMS, preferred_element_type=F32)


def _rms(x):
    return x * lax.rsqrt(jnp.mean(x * x, axis=-1, keepdims=True) + EPS)


def _sigmoid(x):
    return 1.0 / (1.0 + jnp.exp(-x))


def _silu(x):
    return x * _sigmoid(x)


def _softplus(x):
    return jnp.maximum(x, 0.0) + jnp.log1p(jnp.exp(-jnp.abs(x)))


def _split3(x):
    hi = x.astype(BF16)
    r1 = x - hi.astype(F32)
    mid = r1.astype(BF16)
    lo = (r1 - mid.astype(F32)).astype(BF16)
    return hi, mid, lo


def _stack3(x):
    return jnp.concatenate(_split3(x), axis=0)


def _pieces_by_lane(x):
    hi, mid, lo = (piece.astype(F32) for piece in _split3(x))
    lane = lax.broadcasted_iota(jnp.int32, x.shape, 1)
    picked = jnp.where(lane < SSM_HEADS, hi, jnp.where(lane < 2 * SSM_HEADS, mid, lo))
    return picked.astype(BF16)


def _pieces_by_row(x, piece_of_row):
    hi, mid, lo = (piece.astype(F32) for piece in _split3(x))
    idx = jnp.broadcast_to(piece_of_row, x.shape)
    picked = jnp.where(idx == 0, hi, jnp.where(idx == 1, mid, jnp.where(idx == 2, lo, 0.0)))
    return picked.astype(BF16)


def _params(n_arbitrary=1):
    return pltpu.CompilerParams(dimension_semantics=("arbitrary",) * n_arbitrary,
                                vmem_limit_bytes=VMEM_LIMIT)


def _const_spec(shape):
    nd = len(shape)
    return pl.BlockSpec(shape, lambda *_: (0,) * nd)


def _layer_spec(l, shape):
    return pl.BlockSpec((None,) + shape, lambda *_: (l, 0, 0))


def _ffn_kernel(x_ref, nw_ref, wgu_ref, wd_ref, fw_ref, o_ref, acc_ref, *, final):
    x = x_ref[...]
    xn = (_rms(x) * nw_ref[...]).astype(BF16)
    for c in range(FFN_DIM // FFN_TILE):
        lo = c * FFN_TILE
        g = _dot(xn, wgu_ref[:, lo:lo + FFN_TILE])
        u = _dot(xn, wgu_ref[:, FFN_DIM + lo:FFN_DIM + lo + FFN_TILE])
        a = (_silu(g) * u).astype(BF16)
        part = _dot(a, wd_ref[lo:lo + FFN_TILE, :])
        if c == 0:
            acc_ref[...] = part
        else:
            acc_ref[...] += part
    y = x + 0.5 * acc_ref[...]
    if final:
        y = _rms(y) * fw_ref[...]
    o_ref[...] = y


def _ffn(x, nw, wgu, wd, fw, *, l, tm, final):
    t = x.shape[0]
    return pl.pallas_call(
        functools.partial(_ffn_kernel, final=final),
        grid=(t // tm,),
        in_specs=[pl.BlockSpec((tm, D_MODEL), lambda i: (i, 0)),
                  _const_spec((1, D_MODEL)),
                  _layer_spec(l, (D_MODEL, 2 * FFN_DIM)),
                  _layer_spec(l, (FFN_DIM, D_MODEL)),
                  _const_spec((1, D_MODEL))],
        out_specs=pl.BlockSpec((tm, D_MODEL), lambda i: (i, 0)),
        out_shape=jax.ShapeDtypeStruct((t, D_MODEL), F32),
        scratch_shapes=[pltpu.VMEM((tm, D_MODEL), F32)],
        compiler_params=_params(),
        name="ffn_final" if final else "ffn",
    )(x, nw, wgu, wd, fw)


def _gla_in_kernel(x_ref, nw_ref, wqkvg_ref, wgate_ref, wa1_ref, wa2_ref, ba2_ref,
                   q_ref, k_ref, v_ref, sg_ref, ga_ref, gb_ref, la_ref):
    u = (_rms(x_ref[...]) * nw_ref[...]).astype(BF16)
    q_ref[...] = _dot(u, wqkvg_ref[:, 0:512]).astype(BF16)
    k_ref[...] = _dot(u, wqkvg_ref[:, 512:1024]).astype(BF16)
    v_ref[...] = _dot(u, wqkvg_ref[:, 1024:2048]).astype(BF16)
    sg_ref[...] = _silu(_dot(u, wqkvg_ref[:, 2048:3072])).astype(BF16)
    ga_ref[...] = _sigmoid(_dot(u, wgate_ref[:, 0:1024])).astype(BF16)
    gb_ref[...] = _sigmoid(_dot(u, wgate_ref[:, 1024:2048])).astype(BF16)
    a_lr = _dot(u, wa1_ref[...]).astype(BF16)
    z = _dot(a_lr, wa2_ref[...]) + ba2_ref[...]
    la_ref[...] = -_softplus(-z) / GLA_TAU


def _gla_in(x, nw, wqkvg, wgate, wa1, wa2, ba2, *, l, tm):
    t = x.shape[0]
    row = lambda width: pl.BlockSpec((tm, width), lambda i: (i, 0))
    out = lambda width, dt: jax.ShapeDtypeStruct((t, width), dt)
    return pl.pallas_call(
        _gla_in_kernel,
        grid=(t // tm,),
        in_specs=[row(D_MODEL), _const_spec((1, D_MODEL)),
                  _layer_spec(l, (D_MODEL, 3072)), _layer_spec(l, (D_MODEL, 2048)),
                  _layer_spec(l, (D_MODEL, LANES)), _layer_spec(l, (LANES, GLA_DK_TOTAL)),
                  _const_spec((1, GLA_DK_TOTAL))],
        out_specs=[row(512), row(512), row(1024), row(1024), row(1024), row(1024), row(512)],
        out_shape=[out(512, BF16), out(512, BF16), out(1024, BF16), out(1024, BF16),
                   out(1024, BF16), out(1024, BF16), out(512, F32)],
        compiler_params=_params(),
        name="gla_in",
    )(x, nw, wqkvg, wgate, wa1, wa2, ba2)


ZX_COLS = SSM_INNER + CONV_DIM


def _ssm_in_prompt_kernel(x_ref, nw_ref, wzx_ref, wdt_ref, dtb_ref, cw_ref, cb_ref,
                          sz_ref, xa_ref, dt_ref, tail_ref, ext_ref, *, tm):
    j = pl.program_id(1)

    @pl.when(j == 0)
    def _():
        ext_ref[0:8, :] = jnp.zeros((8, CONV_DIM), F32)

    u = (_rms(x_ref[...]) * nw_ref[...]).astype(BF16)
    sz_ref[...] = _silu(_dot(u, wzx_ref[:, 0:SSM_INNER])).astype(BF16)
    dt_ref[...] = _softplus(_dot(u, wdt_ref[...]) + dtb_ref[...])
    ext_ref[8:8 + tm, :] = _dot(u, wzx_ref[:, SSM_INNER:ZX_COLS])
    acc = cb_ref[...] + ext_ref[8:8 + tm, :] * cw_ref[CONV_W - 1:CONV_W, :]
    for i in range(CONV_W - 1):
        back = CONV_W - 1 - i
        acc = acc + ext_ref[8 - back:8 - back + tm, :] * cw_ref[i:i + 1, :]
    xa_ref[...] = _silu(acc).astype(BF16)
    tail_ref[0] = ext_ref[tm + 8 - (CONV_W - 1):tm + 8, :]
    ext_ref[0:8, :] = ext_ref[tm:tm + 8, :]


def _ssm_in_prompt(x, nw, wzx, wdt, dtb, cw, cb, *, l, nb, tm):
    t = x.shape[0]
    nj = t // nb // tm
    row = lambda width: pl.BlockSpec((tm, width), lambda b, j: (b * nj + j, 0))
    return pl.pallas_call(
        functools.partial(_ssm_in_prompt_kernel, tm=tm),
        grid=(nb, nj),
        in_specs=[row(D_MODEL), _const_spec((1, D_MODEL)),
                  _layer_spec(l, (D_MODEL, ZX_COLS)), _layer_spec(l, (D_MODEL, LANES)),
                  _const_spec((1, LANES)), _const_spec((CONV_W, CONV_DIM)), _const_spec((1, CONV_DIM))],
        out_specs=[row(SSM_INNER), row(CONV_DIM), row(LANES),
                   pl.BlockSpec((1, CONV_W - 1, CONV_DIM), lambda b, j: (b, 0, 0))],
        out_shape=[jax.ShapeDtypeStruct((t, SSM_INNER), BF16),
                   jax.ShapeDtypeStruct((t, CONV_DIM), BF16),
                   jax.ShapeDtypeStruct((t, LANES), F32),
                   jax.ShapeDtypeStruct((nb, CONV_W - 1, CONV_DIM), F32)],
        scratch_shapes=[pltpu.VMEM((tm + 8, CONV_DIM), F32)],
        compiler_params=_params(n_arbitrary=2),
        name="ssm_in_prompt",
    )(x, nw, wzx, wdt, dtb, cw, cb)


def _ssm_in_sample_kernel(x_ref, nw_ref, wzx_ref, wdt_ref, dtb_ref, sz_ref, xr_ref, dt_ref):
    u = (_rms(x_ref[...]) * nw_ref[...]).astype(BF16)
    sz_ref[...] = _silu(_dot(u, wzx_ref[:, 0:SSM_INNER])).astype(BF16)
    xr_ref[...] = _dot(u, wzx_ref[:, SSM_INNER:ZX_COLS])
    dt_ref[...] = _softplus(_dot(u, wdt_ref[...]) + dtb_ref[...])


def _ssm_in_sample(x, nw, wzx, wdt, dtb, *, l, tm):
    t = x.shape[0]
    row = lambda width: pl.BlockSpec((tm, width), lambda i: (i, 0))
    return pl.pallas_call(
        _ssm_in_sample_kernel,
        grid=(t // tm,),
        in_specs=[row(D_MODEL), _const_spec((1, D_MODEL)),
                  _layer_spec(l, (D_MODEL, ZX_COLS)), _layer_spec(l, (D_MODEL, LANES)),
                  _const_spec((1, LANES))],
        out_specs=[row(SSM_INNER), row(CONV_DIM), row(LANES)],
        out_shape=[jax.ShapeDtypeStruct((t, SSM_INNER), BF16),
                   jax.ShapeDtypeStruct((t, CONV_DIM), F32),
                   jax.ShapeDtypeStruct((t, LANES), F32)],
        compiler_params=_params(),
        name="ssm_in_sample",
    )(x, nw, wzx, wdt, dtb)


def _gla_rows(q, k, bcum, blast):
    qf = q.astype(F32) * (GLA_DK ** -0.5)
    kf = k.astype(F32)
    q_in = (qf * jnp.exp(bcum)).astype(BF16)
    k_in = (kf * jnp.exp(-bcum)).astype(BF16)
    k_out = (kf * jnp.exp(blast - bcum)).astype(BF16)
    return q_in, k_in, k_out


def _segment_weights(ac_s, itile, cmask):
    arow = jnp.sum(ac_s * itile, axis=0, keepdims=True)
    return jnp.where(cmask > 0.5, jnp.exp(ac_s - arow), 0.0)


PAIR = 2


def _core_prompt_kernel(q_ref, k_ref, v_ref, la_ref, xa_ref, dt_ref, alog3_ref, dskip_ref,
                        btril3_ref, e3_ref, itile_ref, cmask_ref, bmask_ref,
                        o_ref, y_ref, gla_ref, ssm_ref, s_ref, ht_ref, *, rows, nj):
    j = pl.program_id(1)

    @pl.when(j == 0)
    def _():
        s_ref[...] = jnp.zeros(s_ref.shape, F32)
        ht_ref[...] = jnp.zeros(ht_ref.shape, F32)

    btril3 = btril3_ref[...]
    causal = btril3[0:CHUNK, 0:CHUNK] > 0.5
    ones16 = jnp.ones((BF16_ROWS, LANES), BF16)
    row16 = lax.broadcasted_iota(jnp.int32, (BF16_ROWS, 1), 0)
    itile = itile_ref[...]
    cmask = cmask_ref[...]
    bmask = bmask_ref[...]
    e3 = e3_ref[...]

    bcum_all = _dot(btril3, _stack3(la_ref[...]))
    dt3 = dt_ref[...]
    acum3 = _dot(btril3, _stack3(dt3 * (-jnp.exp(alog3_ref[...]))))
    dtx_all = _dot(_pieces_by_lane(dt3), e3)
    acx_all = _dot(_pieces_by_lane(acum3), e3)

    for c in range(rows // CHUNK):
        r = slice(c * CHUNK, (c + 1) * CHUNK)
        bcum = bcum_all[r, :]
        blast = bcum[CHUNK - 1:CHUNK, :]
        q_in, k_in, k_out = _gla_rows(q_ref[r, :], k_ref[r, :], bcum, blast)
        bl_pieces = _pieces_by_row(jnp.broadcast_to(blast, (BF16_ROWS, GLA_DK_TOTAL)), row16)
        v = v_ref[r, :]
        for h in range(GLA_HEADS):
            kl = slice(h * GLA_DK, (h + 1) * GLA_DK)
            vl = slice(h * GLA_DV, (h + 1) * GLA_DV)
            sc = jnp.where(causal, _dot_nt(q_in[:, kl], k_in[:, kl]), 0.0).astype(BF16)
            s_old = s_ref[h]
            o_h = _dot(sc, v[:, vl]) + _dot(q_in[:, kl], s_old.astype(BF16))
            o_ref[r, vl] = o_h.astype(BF16)
            ds = _dot_tn(k_out[:, kl], v[:, vl])
            dec = jnp.exp(_dot_tn(bl_pieces[:, kl], ones16))
            s_ref[h] = s_old * jnp.concatenate([dec, dec], axis=1) + ds
        acx = acx_all[r, :]
        alast = acx[CHUNK - 1:CHUNK, :]
        xa = xa_ref[r, :]
        xs = xa[:, 0:SSM_INNER].astype(F32)
        xg = xs * dtx_all[r, :]
        xgb = xg.astype(BF16)
        xw = (xg * jnp.exp(alast - acx)).astype(BF16)
        eac = jnp.exp(acx)
        seg = _segment_weights(acx, itile, cmask)
        dec_row = jnp.exp(alast)
        for g in range(SSM_GROUPS):
            gl = slice(g * GROUP_LANES, (g + 1) * GROUP_LANES)
            bg = xa[:, SSM_INNER + g * SSM_DSTATE:SSM_INNER + (g + 1) * SSM_DSTATE]
            cg = xa[:, SSM_INNER + (SSM_GROUPS + g) * SSM_DSTATE:SSM_INNER + (SSM_GROUPS + g + 1) * SSM_DSTATE]
            cbx = _dot_nt(cg, jnp.concatenate([bg] * SSM_HPG, axis=0))
            w = (cbx * seg[:, gl]).astype(BF16)
            yd = []
            for pr in range(SSM_HPG // PAIR):
                lo = g * GROUP_LANES + pr * LANES
                xblk = jnp.concatenate([xgb[:, lo:lo + LANES]] * PAIR, axis=0) * bmask
                yd.append(_dot(w[:, pr * LANES:(pr + 1) * LANES], xblk))
            h_old = ht_ref[:, gl]
            yo = _dot(cg, h_old.astype(BF16)) * eac[:, gl]
            y = jnp.concatenate(yd, axis=1) + yo + dskip_ref[:, gl] * xs[:, gl]
            y_ref[r, gl] = y.astype(BF16)
            ht_ref[:, gl] = h_old * dec_row[:, gl] + _dot_tn(bg, xw[:, gl])

    @pl.when(j == nj - 1)
    def _():
        gla_ref[0] = s_ref[...]
        ssm_ref[0] = ht_ref[...].T


def _core_prompt(q, k, v, la, xa, dt, alog3, dskip_x, consts, *, nb, rows):
    t = q.shape[0]
    nj = t // nb // rows
    row = lambda width: pl.BlockSpec((rows, width), lambda b, j: (b * nj + j, 0))
    return pl.pallas_call(
        functools.partial(_core_prompt_kernel, rows=rows, nj=nj),
        grid=(nb, nj),
        in_specs=[row(512), row(512), row(1024), row(512), row(CONV_DIM), row(LANES),
                  _const_spec((1, LANES)), _const_spec((1, SSM_INNER))]
                 + [_const_spec(c.shape) for c in consts],
        out_specs=[row(GLA_DV_TOTAL), row(SSM_INNER),
                   pl.BlockSpec((1, GLA_HEADS, GLA_DK, GLA_DV), lambda b, j: (b, 0, 0, 0)),
                   pl.BlockSpec((1, SSM_INNER, SSM_DSTATE), lambda b, j: (b, 0, 0))],
        out_shape=[jax.ShapeDtypeStruct((t, GLA_DV_TOTAL), BF16),
                   jax.ShapeDtypeStruct((t, SSM_INNER), BF16),
                   jax.ShapeDtypeStruct((nb, GLA_HEADS, GLA_DK, GLA_DV), F32),
                   jax.ShapeDtypeStruct((nb, SSM_INNER, SSM_DSTATE), F32)],
        scratch_shapes=[pltpu.VMEM((GLA_HEADS, GLA_DK, GLA_DV), F32),
                        pltpu.VMEM((SSM_DSTATE, SSM_INNER), F32)],
        compiler_params=_params(n_arbitrary=2),
        name="core_prompt",
    )(q, k, v, la, xa, dt, alog3, dskip_x, *consts)


SEQ_PER_STEP = 4
SAMPLE_LEN = 4
SAMPLE_ROWS = SEQ_PER_STEP * SAMPLE_LEN


def _core_sample_kernel(q_ref, k_ref, v_ref, la_ref, xr_ref, c4_ref, dt_ref, s0_ref, h0_ref,
                        cw_ref, cb_ref, alog3_ref, dskip_ref,
                        bs3_ref, shmat_ref, e3_ref, es3_ref, itile_ref, cmask_ref, bmask_ref,
                        qmask_ref, smask_ref, rowseq_ref, pos_ref,
                        o_ref, y_ref, s1_ref, h1_ref):
    rs = SAMPLE_ROWS
    bs3 = bs3_ref[...]
    same_causal = bs3[0:rs, 0:rs] > 0.5
    ones16 = jnp.ones((BF16_ROWS, LANES), BF16)
    pos = pos_ref[...]
    xr = xr_ref[...]
    shifted = _dot(shmat_ref[...], jnp.concatenate(_split3(xr) + _split3(c4_ref[...]), axis=0))
    acc = cb_ref[...] + xr * cw_ref[CONV_W - 1:CONV_W, :]
    for i in range(CONV_W - 1):
        back = CONV_W - 1 - i
        acc = acc + shifted[(back - 1) * rs:back * rs, :] * cw_ref[i:i + 1, :]
    xa = _silu(acc)
    sums = _dot(bs3, _stack3(la_ref[...]))
    bcum, blast = sums[0:rs, :], sums[rs:2 * rs, :]
    q_in, k_in, k_out = _gla_rows(q_ref[...], k_ref[...], bcum, blast)
    bl_pieces = _pieces_by_row(blast, pos)
    v = v_ref[...]
    qmask = qmask_ref[...]
    for h in range(GLA_HEADS):
        kl = slice(h * GLA_DK, (h + 1) * GLA_DK)
        vl = slice(h * GLA_DV, (h + 1) * GLA_DV)
        sc = jnp.where(same_causal, _dot_nt(q_in[:, kl], k_in[:, kl]), 0.0).astype(BF16)
        s_old = s0_ref[:, h].reshape(SEQ_PER_STEP * GLA_DK, GLA_DV)
        q_blk = jnp.concatenate([q_in[:, kl]] * SEQ_PER_STEP, axis=1) * qmask
        o_h = _dot(sc, v[:, vl]) + _dot(q_blk, s_old.astype(BF16))
        o_ref[:, vl] = o_h.astype(BF16)
        k_blk = jnp.concatenate([k_out[:, kl]] * SEQ_PER_STEP, axis=1) * qmask
        ds = _dot_tn(k_blk, v[:, vl])
        bl_blk = jnp.concatenate([bl_pieces[:, kl]] * SEQ_PER_STEP, axis=1) * qmask
        dec = jnp.exp(_dot_tn(bl_blk, ones16))
        s_new = s_old * jnp.concatenate([dec, dec], axis=1) + ds
        s1_ref[:, h] = s_new.reshape(SEQ_PER_STEP, GLA_DK, GLA_DV)
    dt3 = dt_ref[...]
    sums = _dot(bs3, _stack3(dt3 * (-jnp.exp(alog3_ref[...]))))
    acum3, alast3 = sums[0:rs, :], sums[rs:2 * rs, :]
    p_ac = _pieces_by_lane(acum3)
    wide = _dot(jnp.concatenate([_pieces_by_lane(dt3), p_ac, _pieces_by_lane(alast3)], axis=0), e3_ref[...])
    dtx, acx, alx = wide[0:rs, :], wide[rs:2 * rs, :], wide[2 * rs:3 * rs, :]
    xs = xa[:, 0:SSM_INNER]
    xg = xs * dtx
    xgb = xg.astype(BF16)
    xw = xg * jnp.exp(alx - acx)
    eac = jnp.exp(acx)
    seg = _segment_weights(_dot(p_ac, es3_ref[...]), itile_ref[...], cmask_ref[...])
    al_pieces = _pieces_by_row(alx, pos)
    smask = smask_ref[...]
    rowseq = rowseq_ref[...]
    sl = SSM_HPG * rs
    for g in range(SSM_GROUPS):
        gl = slice(g * GROUP_LANES, (g + 1) * GROUP_LANES)
        bg = xa[:, SSM_INNER + g * SSM_DSTATE:SSM_INNER + (g + 1) * SSM_DSTATE].astype(BF16)
        cg = xa[:, SSM_INNER + (SSM_GROUPS + g) * SSM_DSTATE:
                SSM_INNER + (SSM_GROUPS + g + 1) * SSM_DSTATE].astype(BF16)
        cbx = _dot_nt(cg, jnp.concatenate([bg] * SSM_HPG, axis=0))
        w = (cbx * seg[:, g * sl:(g + 1) * sl]).astype(BF16)
        xblk = jnp.concatenate([xgb[:, gl]] * SSM_HPG, axis=0) * bmask_ref[...]
        yd = _dot(w, xblk)
        yo = jnp.zeros((rs, GROUP_LANES), F32)
        for s in range(SEQ_PER_STEP):
            c_s = jnp.where(rowseq == s, cg, jnp.zeros_like(cg))
            yo = yo + _dot_nt(c_s, h0_ref[s, gl, :].astype(BF16))
        y = yd + yo * eac[:, gl] + dskip_ref[:, gl] * xs[:, gl]
        y_ref[:, gl] = y.astype(BF16)
        xw_blk = (jnp.concatenate([xw[:, gl]] * SEQ_PER_STEP, axis=1)).astype(BF16) * smask
        ds = _dot_tn(xw_blk, bg)
        al_blk = jnp.concatenate([al_pieces[:, gl]] * SEQ_PER_STEP, axis=1) * smask
        dec = jnp.exp(_dot_tn(al_blk, ones16))
        for s in range(SEQ_PER_STEP):
            sr = slice(s * GROUP_LANES, (s + 1) * GROUP_LANES)
            h1_ref[s, gl, :] = h0_ref[s, gl, :] * dec[sr, :] + ds[sr, :]


def _core_sample(q, k, v, la, xr, c4, dt, s0, h0, cw, cb, alog3, dskip_x, consts):
    t = q.shape[0]
    rs = SAMPLE_ROWS
    nseq = t // SAMPLE_LEN
    row = lambda width: pl.BlockSpec((rs, width), lambda i: (i, 0))
    s_spec = pl.BlockSpec((SEQ_PER_STEP, GLA_HEADS, GLA_DK, GLA_DV), lambda i: (i, 0, 0, 0))
    h_spec = pl.BlockSpec((SEQ_PER_STEP, SSM_INNER, SSM_DSTATE), lambda i: (i, 0, 0))
    return pl.pallas_call(
        _core_sample_kernel,
        grid=(nseq // SEQ_PER_STEP,),
        in_specs=[row(512), row(512), row(1024), row(512), row(CONV_DIM), row(CONV_DIM), row(LANES),
                  s_spec, h_spec,
                  _const_spec((CONV_W, CONV_DIM)), _const_spec((1, CONV_DIM)),
                  _const_spec((1, LANES)), _const_spec((1, SSM_INNER))]
                 + [_const_spec(c.shape) for c in consts],
        out_specs=[row(GLA_DV_TOTAL), row(SSM_INNER), s_spec, h_spec],
        out_shape=[jax.ShapeDtypeStruct((t, GLA_DV_TOTAL), BF16),
                   jax.ShapeDtypeStruct((t, SSM_INNER), BF16),
                   jax.ShapeDtypeStruct(s0.shape, F32),
                   jax.ShapeDtypeStruct(h0.shape, F32)],
        compiler_params=_params(),
        name="core_sample",
    )(q, k, v, la, xr, c4, dt, s0, h0, cw, cb, alog3, dskip_x, *consts)


def _merge_kernel(x_ref, o_ref, sg_ref, y_ref, sz_ref, ga_ref, gb_ref,
                  gn_ref, sn_ref, wpg_ref, wps_ref, wo_ref, out_ref):
    o = o_ref[...].astype(F32)
    parts = []
    for h in range(GLA_HEADS):
        vl = slice(h * GLA_DV, (h + 1) * GLA_DV)
        parts.append(_rms(o[:, vl]) * gn_ref[:, vl])
    og = (jnp.concatenate(parts, axis=1) * sg_ref[...].astype(F32)).astype(BF16)
    yz = y_ref[...].astype(F32) * sz_ref[...].astype(F32)
    yn = (_rms(yz) * sn_ref[...]).astype(BF16)
    m = (ga_ref[...].astype(F32) * _dot(og, wpg_ref[...])
         + gb_ref[...].astype(F32) * _dot(yn, wps_ref[...]))
    out_ref[...] = x_ref[...] + _dot(m.astype(BF16), wo_ref[...])


def _merge(x, o, sg, y, sz, ga, gb, gn, sn, wpg, wps, wo, *, l, tm):
    t = x.shape[0]
    row = lambda width: pl.BlockSpec((tm, width), lambda i: (i, 0))
    return pl.pallas_call(
        _merge_kernel,
        grid=(t // tm,),
        in_specs=[row(D_MODEL), row(1024), row(1024), row(2048), row(2048), row(1024), row(1024),
                  _const_spec((1, GLA_DV_TOTAL)), _const_spec((1, SSM_INNER)),
                  _layer_spec(l, (GLA_DV_TOTAL, D_MODEL)), _layer_spec(l, (SSM_INNER, D_MODEL)),
                  _layer_spec(l, (D_MODEL, D_MODEL))],
        out_specs=row(D_MODEL),
        out_shape=jax.ShapeDtypeStruct((t, D_MODEL), F32),
        compiler_params=_params(),
        name="merge",
    )(x, o, sg, y, sz, ga, gb, gn, sn, wpg, wps, wo)


def _expand_table(width):
    t = np.zeros((LANES, SSM_HEADS * width), np.float32)
    for k in range(PIECES):
        for h in range(SSM_HEADS):
            t[k * SSM_HEADS + h, h * width:(h + 1) * width] = 1.0
    return jnp.asarray(t, dtype=BF16)


def _segment_tables(rows, allowed):
    s_of_lane = np.arange(SSM_HPG * rows) % rows
    itile = (np.arange(rows)[:, None] == s_of_lane[None, :]).astype(np.float32)
    cmask = allowed[:, s_of_lane].astype(np.float32)
    return itile, cmask


def _head_block_mask(heads, rows):
    hr = np.arange(heads * rows) // rows
    hc = np.arange(heads * SSM_HEADDIM) // SSM_HEADDIM
    return jnp.asarray((hr[:, None] == hc[None, :]).astype(np.float32), dtype=BF16)


def _prompt_consts(rows):
    idx = np.arange(rows)
    btril = ((idx[:, None] // CHUNK == idx[None, :] // CHUNK) & (idx[None, :] <= idx[:, None])).astype(np.float32)
    tril = np.tril(np.ones((CHUNK, CHUNK), np.float32))
    itile, cmask = _segment_tables(CHUNK, tril)
    tile4 = lambda a: jnp.asarray(np.tile(a, (1, SSM_GROUPS)))
    return (jnp.asarray(np.tile(btril, (1, PIECES)), dtype=BF16), _expand_table(SSM_HEADDIM),
            tile4(itile), tile4(cmask), _head_block_mask(PAIR, CHUNK))


def _sample_consts():
    rs = SAMPLE_ROWS
    seq = np.arange(rs) // SAMPLE_LEN
    pos = np.arange(rs) % SAMPLE_LEN
    same = (seq[:, None] == seq[None, :]).astype(np.float32)
    btril = same * (pos[None, :] <= pos[:, None])
    bs3 = np.tile(np.concatenate([btril, same], axis=0), (1, PIECES))
    shmat = np.zeros((CONV_W - 1, rs, 2, rs), np.float32)
    for back in range(1, CONV_W):
        for r in range(rs):
            if pos[r] >= back:
                shmat[back - 1, r, 0, r - back] = 1.0
            else:
                shmat[back - 1, r, 1, seq[r] * SAMPLE_LEN + SAMPLE_LEN + pos[r] - back] = 1.0
    shmat = np.repeat(shmat[:, :, :, None, :], PIECES, axis=3).reshape((CONV_W - 1) * rs, 2 * PIECES * rs)
    itile, cmask = _segment_tables(rs, btril)
    tile4 = lambda a: jnp.asarray(np.tile(a, (1, SSM_GROUPS)))
    qmask = (seq[:, None] == (np.arange(SEQ_PER_STEP * GLA_DK) // GLA_DK)[None, :]).astype(np.float32)
    smask = (seq[:, None] == (np.arange(SEQ_PER_STEP * GROUP_LANES) // GROUP_LANES)[None, :]).astype(np.float32)
    rowseq = np.broadcast_to(seq[:, None], (rs, SSM_DSTATE)).astype(np.int32)
    return (jnp.asarray(bs3, dtype=BF16), jnp.asarray(shmat, dtype=BF16),
            _expand_table(SSM_HEADDIM), _expand_table(rs), tile4(itile), tile4(cmask),
            _head_block_mask(SSM_HPG, rs), jnp.asarray(qmask, dtype=BF16), jnp.asarray(smask, dtype=BF16),
            jnp.asarray(rowseq), jnp.asarray(pos.reshape(rs, 1).astype(np.int32)))


def _prep_weights(p):
    offs = np.cumsum((0,) + IN_SPLITS)
    w_in = p["w_in"]
    piece = lambda i, j: w_in[:, :, offs[i]:offs[j]].astype(BF16)
    lane_pad = lambda w: jnp.pad(w, ((0, 0), (0, 0), (0, LANES - w.shape[-1])))
    rep3 = lambda a: lane_pad(jnp.concatenate([a] * PIECES, axis=-1))
    return dict(
        wqkvg=piece(0, 4), wa1=lane_pad(piece(4, 5)), wzx=piece(5, 7), wdt3=rep3(piece(7, 8)), wgate=piece(8, 10),
        wa2=jnp.pad(p["w_gla_a2"], ((0, 0), (0, LANES - GLA_GATE_RANK), (0, 0))).astype(BF16),
        dtb3=rep3(p["dt_bias"][:, None, :]), alog3=rep3(p["a_log"][:, None, :]),
        dskip_x=jnp.repeat(p["d_skip"], SSM_HEADDIM, axis=-1)[:, None, :],
        wpg=p["w_proj_gla"].astype(BF16), wps=p["w_proj_ssm"].astype(BF16), wo=p["w_out"].astype(BF16),
        gu1=p["w_ffn1_gu"].astype(BF16), d1=p["w_ffn1_down"].astype(BF16),
        gu2=p["w_ffn2_gu"].astype(BF16), d2=p["w_ffn2_down"].astype(BF16),
    )


PROMPT_TM = 512
CORE_ROWS = 256


def _trunk(x, w, p, fw, *, tm, mixer):
    depth = p["w_in"].shape[0]
    row = lambda name, l: p[name][l].reshape(1, -1)
    glas, ssms, convs = [], [], []
    for l in range(depth):
        x = _ffn(x, row("norm_ffn1", l), w["gu1"], w["d1"], fw, l=l, tm=tm, final=False)
        o, sg, y, sz, ga, gb, g_new, s_new, c_new = mixer(l, x)
        x = _merge(x, o, sg, y, sz, ga, gb, row("gla_norm", l), row("ssm_norm", l),
                   w["wpg"], w["wps"], w["wo"], l=l, tm=tm)
        x = _ffn(x, row("norm_ffn2", l), w["gu2"], w["d2"], fw, l=l, tm=tm, final=l == depth - 1)
        glas.append(g_new)
        ssms.append(s_new)
        convs.append(c_new)
    return x, jnp.stack(glas), jnp.stack(ssms), jnp.stack(convs)


def _trunk_prompt(x, w, p, fw, nb):
    consts = _prompt_consts(CORE_ROWS)
    row = lambda name, l: p[name][l].reshape(1, -1)

    def mixer(l, x):
        q, k, v, sg, ga, gb, la = _gla_in(x, row("norm_mix", l), w["wqkvg"], w["wgate"], w["wa1"], w["wa2"],
                                          row("b_gla_a2", l), l=l, tm=PROMPT_TM)
        sz, xa, dt, tail = _ssm_in_prompt(x, row("norm_mix", l), w["wzx"], w["wdt3"], w["dtb3"][l],
                                          p["conv_w"][l], row("conv_b", l), l=l, nb=nb, tm=PROMPT_TM)
        o, y, s_fin, h_fin = _core_prompt(q, k, v, la, xa, dt, w["alog3"][l], w["dskip_x"][l], consts,
                                          nb=nb, rows=CORE_ROWS)
        return o, sg, y, sz, ga, gb, s_fin, h_fin.reshape(nb, SSM_HEADS, SSM_HEADDIM, SSM_DSTATE), tail

    return _trunk(x, w, p, fw, tm=PROMPT_TM, mixer=mixer)


def _trunk_sample(x, state_gla, state_ssm, state_conv, w, p, fw):
    consts = _sample_consts()
    t = x.shape[0]
    nseq = t // SAMPLE_LEN
    row = lambda name, l: p[name][l].reshape(1, -1)

    def mixer(l, x):
        q, k, v, sg, ga, gb, la = _gla_in(x, row("norm_mix", l), w["wqkvg"], w["wgate"], w["wa1"], w["wa2"],
                                          row("b_gla_a2", l), l=l, tm=t)
        sz, xr, dt = _ssm_in_sample(x, row("norm_mix", l), w["wzx"], w["wdt3"], w["dtb3"][l], l=l, tm=t)
        c4 = jnp.pad(state_conv[l], ((0, 0), (SAMPLE_LEN - (CONV_W - 1), 0), (0, 0))).reshape(t, CONV_DIM)
        h0 = state_ssm[l].reshape(nseq, SSM_INNER, SSM_DSTATE)
        o, y, s1, h1 = _core_sample(q, k, v, la, xr, c4, dt, state_gla[l], h0, p["conv_w"][l],
                                    row("conv_b", l), w["alog3"][l], w["dskip_x"][l], consts)
        conv_new = xr.reshape(nseq, SAMPLE_LEN, CONV_DIM)[:, SAMPLE_LEN - (CONV_W - 1):]
        return o, sg, y, sz, ga, gb, s1, h1.reshape(nseq, SSM_HEADS, SSM_HEADDIM, SSM_DSTATE), conv_new

    return _trunk(x, w, p, fw, tm=t, mixer=mixer)


def kernel(x_prompt, x_sample, state_gla, state_ssm, state_conv, norm_ffn1, w_ffn1_gu, w_ffn1_down, norm_mix, w_in, w_gla_a2, b_gla_a2, gla_norm, conv_w, conv_b, dt_bias, a_log, d_skip, ssm_norm, w_proj_gla, w_proj_ssm, w_out, norm_ffn2, w_ffn2_gu, w_ffn2_down, norm_final):
    p = dict(norm_ffn1=norm_ffn1, w_ffn1_gu=w_ffn1_gu, w_ffn1_down=w_ffn1_down, norm_mix=norm_mix,
             w_in=w_in, w_gla_a2=w_gla_a2, b_gla_a2=b_gla_a2, gla_norm=gla_norm, conv_w=conv_w,
             conv_b=conv_b, dt_bias=dt_bias, a_log=a_log, d_skip=d_skip, ssm_norm=ssm_norm,
             w_proj_gla=w_proj_gla, w_proj_ssm=w_proj_ssm, w_out=w_out, norm_ffn2=norm_ffn2,
             w_ffn2_gu=w_ffn2_gu, w_ffn2_down=w_ffn2_down)
    w = _prep_weights(p)
    fw = norm_final.reshape(1, D_MODEL)
    bp, lp, _ = x_prompt.shape
    bs, ls, _ = x_sample.shape
    assert ls == SAMPLE_LEN and lp % CORE_ROWS == 0 and lp % PROMPT_TM == 0 and bs % SEQ_PER_STEP == 0
    yp, gla_p, ssm_p, conv_p = _trunk_prompt(x_prompt.reshape(bp * lp, D_MODEL), w, p, fw, bp)
    ys, gla_s, ssm_s, conv_s = _trunk_sample(x_sample.reshape(bs * ls, D_MODEL), state_gla, state_ssm,
                                             state_conv, w, p, fw)
    return (yp.reshape(bp, lp, D_MODEL), ys.reshape(bs, ls, D_MODEL),
            gla_p, ssm_p, conv_p, gla_s, ssm_s, conv_s)
```

```python
import functools

import numpy as np
import jax
import jax.numpy as jnp
from jax import lax
from jax.experimental import pallas as pl
from jax.experimental.pallas import tpu as pltpu

F32 = jnp.float32
BF16 = jnp.bfloat16

D_MODEL = 1024
EPS = 1e-6
FFN_DIM = 2816
GLA_HEADS = 4
GLA_DK = 128
GLA_DV = 256
GLA_DK_TOTAL = GLA_HEADS * GLA_DK
GLA_DV_TOTAL = GLA_HEADS * GLA_DV
GLA_GATE_RANK = 16
GLA_TAU = 16.0
SSM_INNER = 2048
SSM_HEADDIM = 64
SSM_HEADS = 32
SSM_GROUPS = 4
SSM_HPG = SSM_HEADS // SSM_GROUPS
SSM_DSTATE = 128
CONV_W = 4
CONV_DIM = SSM_INNER + 2 * SSM_GROUPS * SSM_DSTATE
CHUNK = 64
IN_SPLITS = (GLA_DK_TOTAL, GLA_DK_TOTAL, GLA_DV_TOTAL, GLA_DV_TOTAL, GLA_GATE_RANK,
             SSM_INNER, CONV_DIM, SSM_HEADS, D_MODEL, D_MODEL)

LANES = 128
BF16_ROWS = 16
FFN_TILE = 256
GROUP_LANES = SSM_HPG * SSM_HEADDIM
PIECES = 3
VMEM_LIMIT = 56 * 1024 * 1024

NT_DIMS = (((1,), (1,)), ((), ()))
TN_DIMS = (((0,), (0,)), ((), ()))


def _dot(a, b):
    return jnp.dot(a, b, preferred_element_type=F32)


def _dot_nt(a, b):
    return lax.dot_general(a, b, NT_DIMS, preferred_element_type=F32)


def _dot_tn(a, b):
    return lax.dot_general(a, b, TN_DIMS, preferred_element_type=F32)


def _rms(x):
    return x * lax.rsqrt(jnp.mean(x * x, axis=-1, keepdims=True) + EPS)


def _sigmoid(x):
    return 0.5 * jnp.tanh(0.5 * x) + 0.5


def _silu(x):
    h = 0.5 * x
    return h * jnp.tanh(h) + h


def _softplus(x):
    return jnp.maximum(x, 0.0) + jnp.log1p(jnp.exp(-jnp.abs(x)))


def _split3(x):
    hi = x.astype(BF16)
    r1 = x - hi.astype(F32)
    mid = r1.astype(BF16)
    lo = (r1 - mid.astype(F32)).astype(BF16)
    return hi, mid, lo


def _stack3(x):
    return jnp.concatenate(_split3(x), axis=0)


def _pieces_by_lane(x):
    hi, mid, lo = (piece.astype(F32) for piece in _split3(x))
    lane = lax.broadcasted_iota(jnp.int32, x.shape, 1)
    picked = jnp.where(lane < SSM_HEADS, hi, jnp.where(lane < 2 * SSM_HEADS, mid, lo))
    return picked.astype(BF16)


def _pieces_by_row(x, piece_of_row):
    hi, mid, lo = (piece.astype(F32) for piece in _split3(x))
    idx = jnp.broadcast_to(piece_of_row, x.shape)
    picked = jnp.where(idx == 0, hi, jnp.where(idx == 1, mid, jnp.where(idx == 2, lo, 0.0)))
    return picked.astype(BF16)


def _params(n_arbitrary=1):
    return pltpu.CompilerParams(dimension_semantics=("arbitrary",) * n_arbitrary,
                                vmem_limit_bytes=VMEM_LIMIT)


def _const_spec(shape):
    nd = len(shape)
    return pl.BlockSpec(shape, lambda *_: (0,) * nd)


def _layer_spec(l, shape):
    return pl.BlockSpec((None,) + shape, lambda *_: (l, 0, 0))


def _ffn_kernel(x_ref, nw_ref, wgu_ref, wd_ref, fw_ref, o_ref, acc_ref, *, final):
    x = x_ref[...]
    xn = (_rms(x) * nw_ref[...]).astype(BF16)
    for c in range(FFN_DIM // FFN_TILE):
        lo = c * FFN_TILE
        g = _dot(xn, wgu_ref[:, lo:lo + FFN_TILE])
        u = _dot(xn, wgu_ref[:, FFN_DIM + lo:FFN_DIM + lo + FFN_TILE])
        a = (_silu(g) * u).astype(BF16)
        part = _dot(a, wd_ref[lo:lo + FFN_TILE, :])
        if c == 0:
            acc_ref[...] = part
        else:
            acc_ref[...] += part
    y = x + 0.5 * acc_ref[...]
    if final:
        y = _rms(y) * fw_ref[...]
    o_ref[...] = y


def _ffn(x, nw, wgu, wd, fw, *, l, tm, final):
    t = x.shape[0]
    return pl.pallas_call(
        functools.partial(_ffn_kernel, final=final),
        grid=(t // tm,),
        in_specs=[pl.BlockSpec((tm, D_MODEL), lambda i: (i, 0)),
                  _const_spec((1, D_MODEL)),
                  _layer_spec(l, (D_MODEL, 2 * FFN_DIM)),
                  _layer_spec(l, (FFN_DIM, D_MODEL)),
                  _const_spec((1, D_MODEL))],
        out_specs=pl.BlockSpec((tm, D_MODEL), lambda i: (i, 0)),
        out_shape=jax.ShapeDtypeStruct((t, D_MODEL), F32),
        scratch_shapes=[pltpu.VMEM((tm, D_MODEL), F32)],
        compiler_params=_params(),
        name="ffn_final" if final else "ffn",
    )(x, nw, wgu, wd, fw)


def _gla_in_kernel(x_ref, nw_ref, wqkvg_ref, wgate_ref, wa1_ref, wa2_ref, ba2_ref,
                   q_ref, k_ref, v_ref, sg_ref, ga_ref, gb_ref, la_ref):
    u = (_rms(x_ref[...]) * nw_ref[...]).astype(BF16)
    q_ref[...] = _dot(u, wqkvg_ref[:, 0:512]).astype(BF16)
    k_ref[...] = _dot(u, wqkvg_ref[:, 512:1024]).astype(BF16)
    v_ref[...] = _dot(u, wqkvg_ref[:, 1024:2048]).astype(BF16)
    sg_ref[...] = _silu(_dot(u, wqkvg_ref[:, 2048:3072])).astype(BF16)
    ga_ref[...] = _sigmoid(_dot(u, wgate_ref[:, 0:1024])).astype(BF16)
    gb_ref[...] = _sigmoid(_dot(u, wgate_ref[:, 1024:2048])).astype(BF16)
    a_lr = _dot(u, wa1_ref[...]).astype(BF16)
    z = _dot(a_lr, wa2_ref[...]) + ba2_ref[...]
    la_ref[...] = -_softplus(-z) / GLA_TAU


def _gla_in(x, nw, wqkvg, wgate, wa1, wa2, ba2, *, l, tm):
    t = x.shape[0]
    row = lambda width: pl.BlockSpec((tm, width), lambda i: (i, 0))
    out = lambda width, dt: jax.ShapeDtypeStruct((t, width), dt)
    return pl.pallas_call(
        _gla_in_kernel,
        grid=(t // tm,),
        in_specs=[row(D_MODEL), _const_spec((1, D_MODEL)),
                  _layer_spec(l, (D_MODEL, 3072)), _layer_spec(l, (D_MODEL, 2048)),
                  _layer_spec(l, (D_MODEL, LANES)), _layer_spec(l, (LANES, GLA_DK_TOTAL)),
                  _const_spec((1, GLA_DK_TOTAL))],
        out_specs=[row(512), row(512), row(1024), row(1024), row(1024), row(1024), row(512)],
        out_shape=[out(512, BF16), out(512, BF16), out(1024, BF16), out(1024, BF16),
                   out(1024, BF16), out(1024, BF16), out(512, F32)],
        compiler_params=_params(),
        name="gla_in",
    )(x, nw, wqkvg, wgate, wa1, wa2, ba2)


ZX_COLS = SSM_INNER + CONV_DIM


CONV_COLS = 512
SUBLANES = 8


def _ssm_in_prompt_kernel(x_ref, nw_ref, wzx_ref, wdt_ref, dtb_ref, cw_ref, cb_ref,
                          sz_ref, xa_ref, dt_ref, tail_ref, carry_ref, *, tm):
    j = pl.program_id(1)

    @pl.when(j == 0)
    def _():
        carry_ref[...] = jnp.zeros(carry_ref.shape, F32)

    u = (_rms(x_ref[...]) * nw_ref[...]).astype(BF16)
    sz_ref[...] = _silu(_dot(u, wzx_ref[:, 0:SSM_INNER])).astype(BF16)
    dt_ref[...] = _softplus(_dot(u, wdt_ref[...]) + dtb_ref[...])
    tiles = (tm // SUBLANES, SUBLANES, CONV_COLS)
    sub = lax.broadcasted_iota(jnp.int32, tiles, 1)
    for cc in range(CONV_DIM // CONV_COLS):
        cols = slice(cc * CONV_COLS, (cc + 1) * CONV_COLS)
        xr = _dot(u, wzx_ref[:, SSM_INNER + cc * CONV_COLS:SSM_INNER + (cc + 1) * CONV_COLS])
        x3 = xr.reshape(tiles)
        p3 = jnp.concatenate([carry_ref[:, cols], xr[0:tm - SUBLANES, :]], axis=0).reshape(tiles)
        acc = cb_ref[:, cols] + xr * cw_ref[CONV_W - 1:CONV_W, cols]
        for i in range(CONV_W - 1):
            back = CONV_W - 1 - i
            sh = pltpu.roll(jnp.where(sub >= SUBLANES - back, p3, x3), back, axis=1)
            acc = acc + sh.reshape(tm, CONV_COLS) * cw_ref[i:i + 1, cols]
        xa_ref[:, cols] = _silu(acc).astype(BF16)
        carry_ref[:, cols] = xr[tm - SUBLANES:tm, :]
    tail_ref[0] = carry_ref[SUBLANES - (CONV_W - 1):SUBLANES, :]


def _ssm_in_prompt(x, nw, wzx, wdt, dtb, cw, cb, *, l, nb, tm):
    t = x.shape[0]
    nj = t // nb // tm
    row = lambda width: pl.BlockSpec((tm, width), lambda b, j: (b * nj + j, 0))
    return pl.pallas_call(
        functools.partial(_ssm_in_prompt_kernel, tm=tm),
        grid=(nb, nj),
        in_specs=[row(D_MODEL), _const_spec((1, D_MODEL)),
                  _layer_spec(l, (D_MODEL, ZX_COLS)), _layer_spec(l, (D_MODEL, LANES)),
                  _const_spec((1, LANES)), _const_spec((CONV_W, CONV_DIM)), _const_spec((1, CONV_DIM))],
        out_specs=[row(SSM_INNER), row(CONV_DIM), row(LANES),
                   pl.BlockSpec((1, CONV_W - 1, CONV_DIM), lambda b, j: (b, 0, 0))],
        out_shape=[jax.ShapeDtypeStruct((t, SSM_INNER), BF16),
                   jax.ShapeDtypeStruct((t, CONV_DIM), BF16),
                   jax.ShapeDtypeStruct((t, LANES), F32),
                   jax.ShapeDtypeStruct((nb, CONV_W - 1, CONV_DIM), F32)],
        scratch_shapes=[pltpu.VMEM((SUBLANES, CONV_DIM), F32)],
        compiler_params=_params(n_arbitrary=2),
        name="ssm_in_prompt",
    )(x, nw, wzx, wdt, dtb, cw, cb)


def _ssm_in_sample_kernel(x_ref, nw_ref, wzx_ref, wdt_ref, dtb_ref, sz_ref, xr_ref, dt_ref):
    u = (_rms(x_ref[...]) * nw_ref[...]).astype(BF16)
    sz_ref[...] = _silu(_dot(u, wzx_ref[:, 0:SSM_INNER])).astype(BF16)
    xr_ref[...] = _dot(u, wzx_ref[:, SSM_INNER:ZX_COLS])
    dt_ref[...] = _softplus(_dot(u, wdt_ref[...]) + dtb_ref[...])


def _ssm_in_sample(x, nw, wzx, wdt, dtb, *, l, tm):
    t = x.shape[0]
    row = lambda width: pl.BlockSpec((tm, width), lambda i: (i, 0))
    return pl.pallas_call(
        _ssm_in_sample_kernel,
        grid=(t // tm,),
        in_specs=[row(D_MODEL), _const_spec((1, D_MODEL)),
                  _layer_spec(l, (D_MODEL, ZX_COLS)), _layer_spec(l, (D_MODEL, LANES)),
                  _const_spec((1, LANES))],
        out_specs=[row(SSM_INNER), row(CONV_DIM), row(LANES)],
        out_shape=[jax.ShapeDtypeStruct((t, SSM_INNER), BF16),
                   jax.ShapeDtypeStruct((t, CONV_DIM), F32),
                   jax.ShapeDtypeStruct((t, LANES), F32)],
        compiler_params=_params(),
        name="ssm_in_sample",
    )(x, nw, wzx, wdt, dtb)


def _gla_rows(q, k, bcum, blast):
    qf = q.astype(F32) * (GLA_DK ** -0.5)
    kf = k.astype(F32)
    q_in = (qf * jnp.exp(bcum)).astype(BF16)
    k_in = (kf * jnp.exp(-bcum)).astype(BF16)
    k_out = (kf * jnp.exp(blast - bcum)).astype(BF16)
    return q_in, k_in, k_out


def _segment_weights(ac_s, itile, cmask):
    arow = jnp.sum(ac_s * itile, axis=0, keepdims=True)
    return jnp.where(cmask > 0.5, jnp.exp(ac_s - arow), 0.0)


PAIR = 2


SEQS_PER_STEP = 1


def _core_prompt_kernel(q_ref, k_ref, v_ref, la_ref, xa_ref, dt_ref, alog3_ref, dskip_ref,
                        btril3_ref, e3_ref, itile_ref, cmask_ref, bmask_ref,
                        o_ref, y_ref, gla_ref, ssm_ref, s_ref, ht_ref, *, rows, nj):
    j = pl.program_id(1)

    @pl.when(j == 0)
    def _():
        s_ref[...] = jnp.zeros(s_ref.shape, F32)
        ht_ref[...] = jnp.zeros(ht_ref.shape, F32)

    btril3 = btril3_ref[...]
    causal = btril3[0:CHUNK, 0:CHUNK] > 0.5
    ones16 = jnp.ones((BF16_ROWS, LANES), BF16)
    row16 = lax.broadcasted_iota(jnp.int32, (BF16_ROWS, 1), 0)
    itile = itile_ref[...]
    cmask = cmask_ref[...]
    bmask = bmask_ref[...]
    e3 = e3_ref[...]
    a3 = -jnp.exp(alog3_ref[...])

    bcum_all, dtx_all, acx_all = [], [], []
    for sq in range(SEQS_PER_STEP):
        bcum_all.append(_dot(btril3, _stack3(la_ref[sq])))
        dt3 = dt_ref[sq]
        acum3 = _dot(btril3, _stack3(dt3 * a3))
        dtx_all.append(_dot(_pieces_by_lane(dt3), e3))
        acx_all.append(_dot(_pieces_by_lane(acum3), e3))

    chunks = [slice(c * CHUNK, (c + 1) * CHUNK) for c in range(rows // CHUNK)]
    kls = [slice(h * GLA_DK, (h + 1) * GLA_DK) for h in range(GLA_HEADS)]
    vls = [slice(h * GLA_DV, (h + 1) * GLA_DV) for h in range(GLA_HEADS)]
    gls = [slice(g * GROUP_LANES, (g + 1) * GROUP_LANES) for g in range(SSM_GROUPS)]

    for sq in range(SEQS_PER_STEP):
        prep = []
        for r in chunks:
            bcum = bcum_all[sq][r, :]
            blast = bcum[CHUNK - 1:CHUNK, :]
            q_in, k_in, k_out = _gla_rows(q_ref[sq, r, :], k_ref[sq, r, :], bcum, blast)
            bl_pieces = _pieces_by_row(jnp.broadcast_to(blast, (BF16_ROWS, GLA_DK_TOTAL)), row16)
            prep.append((q_in, k_in, k_out, bl_pieces, v_ref[sq, r, :]))
        pairs = [(c, h) for h in range(GLA_HEADS) for c in range(len(chunks))]
        qk = {(c, h): _dot_nt(prep[c][0][:, kls[h]], prep[c][1][:, kls[h]]) for c, h in pairs}
        sc = {ch: jnp.where(causal, qk[ch], 0.0).astype(BF16) for ch in pairs}
        intra = {}
        for c, h in pairs:
            _, _, k_out, bl_pieces, v = prep[c]
            dec = jnp.exp(_dot_tn(bl_pieces[:, kls[h]], ones16))
            intra[c, h] = (_dot(sc[c, h], v[:, vls[h]]), _dot_tn(k_out[:, kls[h]], v[:, vls[h]]), dec)
        for c, r in enumerate(chunks):
            for h, (kl, vl) in enumerate(zip(kls, vls)):
                o_intra, ds, dec = intra[c, h]
                s_old = s_ref[sq, h]
                o_ref[sq, r, vl] = (o_intra + _dot(prep[c][0][:, kl], s_old.astype(BF16))).astype(BF16)
                s_ref[sq, h] = s_old * jnp.concatenate([dec, dec], axis=1) + ds
        prep = []
        for r in chunks:
            acx = acx_all[sq][r, :]
            alast = acx[CHUNK - 1:CHUNK, :]
            xa = xa_ref[sq, r, :]
            xs = xa[:, 0:SSM_INNER].astype(F32)
            xg = xs * dtx_all[sq][r, :]
            xw = (xg * jnp.exp(alast - acx)).astype(BF16)
            seg = _segment_weights(acx, itile, cmask)
            prep.append((xa, xs, xg.astype(BF16), xw, jnp.exp(acx), seg, jnp.exp(alast)))
        pairs = [(c, g) for g in range(SSM_GROUPS) for c in range(len(chunks))]
        b_of = lambda c, g: prep[c][0][:, SSM_INNER + g * SSM_DSTATE:SSM_INNER + (g + 1) * SSM_DSTATE]
        c_of = lambda c, g: prep[c][0][:, SSM_INNER + (SSM_GROUPS + g) * SSM_DSTATE:
                                       SSM_INNER + (SSM_GROUPS + g + 1) * SSM_DSTATE]
        cbx = {(c, g): _dot_nt(c_of(c, g), jnp.concatenate([b_of(c, g)] * SSM_HPG, axis=0)) for c, g in pairs}
        w = {(c, g): (cbx[c, g] * prep[c][5][:, gls[g]]).astype(BF16) for c, g in pairs}
        intra = {}
        for c, g in pairs:
            xs, xgb, xw = prep[c][1], prep[c][2], prep[c][3]
            yd = []
            for pr in range(SSM_HPG // PAIR):
                lo = g * GROUP_LANES + pr * LANES
                xblk = jnp.concatenate([xgb[:, lo:lo + LANES]] * PAIR, axis=0) * bmask
                yd.append(_dot(w[c, g][:, pr * LANES:(pr + 1) * LANES], xblk))
            y_part = jnp.concatenate(yd, axis=1) + dskip_ref[:, gls[g]] * xs[:, gls[g]]
            intra[c, g] = (c_of(c, g), y_part, _dot_tn(b_of(c, g), xw[:, gls[g]]))
        for c, r in enumerate(chunks):
            eac, dec_row = prep[c][4], prep[c][6]
            for g, gl in enumerate(gls):
                cg, y_part, ds_t = intra[c, g]
                h_old = ht_ref[sq, :, gl]
                y_ref[sq, r, gl] = (y_part + _dot(cg, h_old.astype(BF16)) * eac[:, gl]).astype(BF16)
                ht_ref[sq, :, gl] = h_old * dec_row[:, gl] + ds_t

    @pl.when(j == nj - 1)
    def _():
        for sq in range(SEQS_PER_STEP):
            gla_ref[sq] = s_ref[sq]
            ssm_ref[sq] = ht_ref[sq].T


def _core_prompt(q, k, v, la, xa, dt, alog3, dskip_x, consts, *, nb, rows):
    t = q.shape[0]
    nq = SEQS_PER_STEP
    half, length = nb // nq, t // nb
    nj = length // rows
    split = lambda a: a.reshape(nq, half, length, a.shape[-1])
    row = lambda width: pl.BlockSpec((nq, None, rows, width), lambda b, j: (0, b, j, 0))
    o, y, s_fin, h_fin = pl.pallas_call(
        functools.partial(_core_prompt_kernel, rows=rows, nj=nj),
        grid=(half, nj),
        in_specs=[row(512), row(512), row(1024), row(512), row(CONV_DIM), row(LANES),
                  _const_spec((1, LANES)), _const_spec((1, SSM_INNER))]
                 + [_const_spec(c.shape) for c in consts],
        out_specs=[row(GLA_DV_TOTAL), row(SSM_INNER),
                   pl.BlockSpec((nq, None, GLA_HEADS, GLA_DK, GLA_DV), lambda b, j: (0, b, 0, 0, 0)),
                   pl.BlockSpec((nq, None, SSM_INNER, SSM_DSTATE), lambda b, j: (0, b, 0, 0))],
        out_shape=[jax.ShapeDtypeStruct((nq, half, length, GLA_DV_TOTAL), BF16),
                   jax.ShapeDtypeStruct((nq, half, length, SSM_INNER), BF16),
                   jax.ShapeDtypeStruct((nq, half, GLA_HEADS, GLA_DK, GLA_DV), F32),
                   jax.ShapeDtypeStruct((nq, half, SSM_INNER, SSM_DSTATE), F32)],
        scratch_shapes=[pltpu.VMEM((nq, GLA_HEADS, GLA_DK, GLA_DV), F32),
                        pltpu.VMEM((nq, SSM_DSTATE, SSM_INNER), F32)],
        compiler_params=_params(n_arbitrary=2),
        name="core_prompt",
    )(split(q), split(k), split(v), split(la), split(xa), split(dt), alog3, dskip_x, *consts)
    return (o.reshape(t, GLA_DV_TOTAL), y.reshape(t, SSM_INNER),
            s_fin.reshape(nb, GLA_HEADS, GLA_DK, GLA_DV), h_fin.reshape(nb, SSM_INNER, SSM_DSTATE))


SEQ_PER_STEP = 4
SAMPLE_LEN = 4
SAMPLE_ROWS = SEQ_PER_STEP * SAMPLE_LEN


def _core_sample_kernel(q_ref, k_ref, v_ref, la_ref, xr_ref, c4_ref, dt_ref, s0_ref, h0_ref,
                        cw_ref, cb_ref, alog3_ref, dskip_ref,
                        bs3_ref, shmat_ref, e3_ref, es3_ref, itile_ref, cmask_ref, bmask_ref,
                        qmask_ref, smask_ref, rowseq_ref, pos_ref, *rest):
    o_ref, y_ref, s1_ref, h1_ref = rest[-4:]
    rs = SAMPLE_ROWS
    bs3 = bs3_ref[...]
    same_causal = bs3[0:rs, 0:rs] > 0.5
    ones16 = jnp.ones((BF16_ROWS, LANES), BF16)
    pos = pos_ref[...]
    xr = xr_ref[...]
    shifted = _dot(shmat_ref[...], jnp.concatenate(_split3(xr) + _split3(c4_ref[...]), axis=0))
    acc = cb_ref[...] + xr * cw_ref[CONV_W - 1:CONV_W, :]
    for i in range(CONV_W - 1):
        back = CONV_W - 1 - i
        acc = acc + shifted[(back - 1) * rs:back * rs, :] * cw_ref[i:i + 1, :]
    xa = _silu(acc)
    sums = _dot(bs3, _stack3(la_ref[...]))
    bcum, blast = sums[0:rs, :], sums[rs:2 * rs, :]
    q_in, k_in, k_out = _gla_rows(q_ref[...], k_ref[...], bcum, blast)
    bl_pieces = _pieces_by_row(blast, pos)
    v = v_ref[...]
    qmask = qmask_ref[...]
    for h in range(GLA_HEADS):
        kl = slice(h * GLA_DK, (h + 1) * GLA_DK)
        vl = slice(h * GLA_DV, (h + 1) * GLA_DV)
        sc = jnp.where(same_causal, _dot_nt(q_in[:, kl], k_in[:, kl]), 0.0).astype(BF16)
        s_old = s0_ref[:, h].reshape(SEQ_PER_STEP * GLA_DK, GLA_DV)
        q_blk = jnp.concatenate([q_in[:, kl]] * SEQ_PER_STEP, axis=1) * qmask
        o_h = _dot(sc, v[:, vl]) + _dot(q_blk, s_old.astype(BF16))
        o_ref[:, vl] = o_h.astype(BF16)
        k_blk = jnp.concatenate([k_out[:, kl]] * SEQ_PER_STEP, axis=1) * qmask
        ds = _dot_tn(k_blk, v[:, vl])
        bl_blk = jnp.concatenate([bl_pieces[:, kl]] * SEQ_PER_STEP, axis=1) * qmask
        dec = jnp.exp(_dot_tn(bl_blk, ones16))
        s_new = s_old * jnp.concatenate([dec, dec], axis=1) + ds
        s1_ref[:, h] = s_new.reshape(SEQ_PER_STEP, GLA_DK, GLA_DV)
    dt3 = dt_ref[...]
    sums = _dot(bs3, _stack3(dt3 * (-jnp.exp(alog3_ref[...]))))
    acum3, alast3 = sums[0:rs, :], sums[rs:2 * rs, :]
    p_ac = _pieces_by_lane(acum3)
    wide = _dot(jnp.concatenate([_pieces_by_lane(dt3), p_ac, _pieces_by_lane(alast3)], axis=0), e3_ref[...])
    dtx, acx, alx = wide[0:rs, :], wide[rs:2 * rs, :], wide[2 * rs:3 * rs, :]
    xs = xa[:, 0:SSM_INNER]
    xg = xs * dtx
    xgb = xg.astype(BF16)
    xw = xg * jnp.exp(alx - acx)
    eac = jnp.exp(acx)
    seg = _segment_weights(_dot(p_ac, es3_ref[...]), itile_ref[...], cmask_ref[...])
    al_pieces = _pieces_by_row(alx, pos)
    smask = smask_ref[...]
    rowseq = rowseq_ref[...]
    sl = SSM_HPG * rs
    for g in range(SSM_GROUPS):
        gl = slice(g * GROUP_LANES, (g + 1) * GROUP_LANES)
        bg = xa[:, SSM_INNER + g * SSM_DSTATE:SSM_INNER + (g + 1) * SSM_DSTATE].astype(BF16)
        cg = xa[:, SSM_INNER + (SSM_GROUPS + g) * SSM_DSTATE:
                SSM_INNER + (SSM_GROUPS + g + 1) * SSM_DSTATE].astype(BF16)
        cbx = _dot_nt(cg, jnp.concatenate([bg] * SSM_HPG, axis=0))
        w = (cbx * seg[:, g * sl:(g + 1) * sl]).astype(BF16)
        xblk = jnp.concatenate([xgb[:, gl]] * SSM_HPG, axis=0) * bmask_ref[...]
        yd = _dot(w, xblk)
        yo = jnp.zeros((rs, GROUP_LANES), F32)
        for s in range(SEQ_PER_STEP):
            c_s = jnp.where(rowseq == s, cg, jnp.zeros_like(cg))
            yo = yo + _dot_nt(c_s, h0_ref[s, gl, :].astype(BF16))
        y = yd + yo * eac[:, gl] + dskip_ref[:, gl] * xs[:, gl]
        y_ref[:, gl] = y.astype(BF16)
        xw_blk = (jnp.concatenate([xw[:, gl]] * SEQ_PER_STEP, axis=1)).astype(BF16) * smask
        ds = _dot_tn(xw_blk, bg)
        al_blk = jnp.concatenate([al_pieces[:, gl]] * SEQ_PER_STEP, axis=1) * smask
        dec = jnp.exp(_dot_tn(al_blk, ones16))
        for s in range(SEQ_PER_STEP):
            sr = slice(s * GROUP_LANES, (s + 1) * GROUP_LANES)
            h1_ref[s, gl, :] = h0_ref[s, gl, :] * dec[sr, :] + ds[sr, :]


def _core_sample(q, k, v, la, xr, c4, dt, s_all, h_all, cw, cb, alog3, dskip_x, consts, *, l, new_states):
    t = q.shape[0]
    rs = SAMPLE_ROWS
    nseq = t // SAMPLE_LEN
    row = lambda width: pl.BlockSpec((rs, width), lambda i: (i, 0))
    s_spec = pl.BlockSpec((None, SEQ_PER_STEP, GLA_HEADS, GLA_DK, GLA_DV), lambda i: (l, i, 0, 0, 0))
    h_spec = pl.BlockSpec((None, SEQ_PER_STEP, SSM_INNER, SSM_DSTATE), lambda i: (l, i, 0, 0))
    inputs = [q, k, v, la, xr, c4, dt, s_all, h_all, cw, cb, alog3, dskip_x, *consts]
    in_specs = ([row(512), row(512), row(1024), row(512), row(CONV_DIM), row(CONV_DIM), row(LANES),
                 s_spec, h_spec,
                 _const_spec((CONV_W, CONV_DIM)), _const_spec((1, CONV_DIM)),
                 _const_spec((1, LANES)), _const_spec((1, SSM_INNER))]
                + [_const_spec(c.shape) for c in consts])
    aliases = {}
    if new_states is not None:
        aliases = {len(inputs): 2, len(inputs) + 1: 3}
        inputs += list(new_states)
        in_specs += [pl.BlockSpec(memory_space=pl.ANY)] * 2
    return pl.pallas_call(
        _core_sample_kernel,
        grid=(nseq // SEQ_PER_STEP,),
        in_specs=in_specs,
        out_specs=[row(GLA_DV_TOTAL), row(SSM_INNER), s_spec, h_spec],
        out_shape=[jax.ShapeDtypeStruct((t, GLA_DV_TOTAL), BF16),
                   jax.ShapeDtypeStruct((t, SSM_INNER), BF16),
                   jax.ShapeDtypeStruct(s_all.shape, F32),
                   jax.ShapeDtypeStruct(h_all.shape, F32)],
        input_output_aliases=aliases,
        compiler_params=_params(),
        name="core_sample",
    )(*inputs)


def _merge_kernel(x_ref, o_ref, sg_ref, y_ref, sz_ref, ga_ref, gb_ref,
                  gn_ref, sn_ref, wpg_ref, wps_ref, wo_ref, out_ref):
    o = o_ref[...].astype(F32)
    parts = []
    for h in range(GLA_HEADS):
        vl = slice(h * GLA_DV, (h + 1) * GLA_DV)
        parts.append(_rms(o[:, vl]) * gn_ref[:, vl])
    og = (jnp.concatenate(parts, axis=1) * sg_ref[...].astype(F32)).astype(BF16)
    yz = y_ref[...].astype(F32) * sz_ref[...].astype(F32)
    yn = (_rms(yz) * sn_ref[...]).astype(BF16)
    m = (ga_ref[...].astype(F32) * _dot(og, wpg_ref[...])
         + gb_ref[...].astype(F32) * _dot(yn, wps_ref[...]))
    out_ref[...] = x_ref[...] + _dot(m.astype(BF16), wo_ref[...])


def _merge(x, o, sg, y, sz, ga, gb, gn, sn, wpg, wps, wo, *, l, tm):
    t = x.shape[0]
    row = lambda width: pl.BlockSpec((tm, width), lambda i: (i, 0))
    return pl.pallas_call(
        _merge_kernel,
        grid=(t // tm,),
        in_specs=[row(D_MODEL), row(1024), row(1024), row(2048), row(2048), row(1024), row(1024),
                  _const_spec((1, GLA_DV_TOTAL)), _const_spec((1, SSM_INNER)),
                  _layer_spec(l, (GLA_DV_TOTAL, D_MODEL)), _layer_spec(l, (SSM_INNER, D_MODEL)),
                  _layer_spec(l, (D_MODEL, D_MODEL))],
        out_specs=row(D_MODEL),
        out_shape=jax.ShapeDtypeStruct((t, D_MODEL), F32),
        compiler_params=_params(),
        name="merge",
    )(x, o, sg, y, sz, ga, gb, gn, sn, wpg, wps, wo)


def _expand_table(width):
    t = np.zeros((LANES, SSM_HEADS * width), np.float32)
    for k in range(PIECES):
        for h in range(SSM_HEADS):
            t[k * SSM_HEADS + h, h * width:(h + 1) * width] = 1.0
    return jnp.asarray(t, dtype=BF16)


def _segment_tables(rows, allowed):
    s_of_lane = np.arange(SSM_HPG * rows) % rows
    itile = (np.arange(rows)[:, None] == s_of_lane[None, :]).astype(np.float32)
    cmask = allowed[:, s_of_lane].astype(np.float32)
    return itile, cmask


def _head_block_mask(heads, rows):
    hr = np.arange(heads * rows) // rows
    hc = np.arange(heads * SSM_HEADDIM) // SSM_HEADDIM
    return jnp.asarray((hr[:, None] == hc[None, :]).astype(np.float32), dtype=BF16)


def _prompt_consts(rows):
    idx = np.arange(rows)
    btril = ((idx[:, None] // CHUNK == idx[None, :] // CHUNK) & (idx[None, :] <= idx[:, None])).astype(np.float32)
    tril = np.tril(np.ones((CHUNK, CHUNK), np.float32))
    itile, cmask = _segment_tables(CHUNK, tril)
    tile4 = lambda a: jnp.asarray(np.tile(a, (1, SSM_GROUPS)))
    return (jnp.asarray(np.tile(btril, (1, PIECES)), dtype=BF16), _expand_table(SSM_HEADDIM),
            tile4(itile), tile4(cmask), _head_block_mask(PAIR, CHUNK))


def _sample_consts():
    rs = SAMPLE_ROWS
    seq = np.arange(rs) // SAMPLE_LEN
    pos = np.arange(rs) % SAMPLE_LEN
    same = (seq[:, None] == seq[None, :]).astype(np.float32)
    btril = same * (pos[None, :] <= pos[:, None])
    bs3 = np.tile(np.concatenate([btril, same], axis=0), (1, PIECES))
    shmat = np.zeros((CONV_W - 1, rs, 2, rs), np.float32)
    for back in range(1, CONV_W):
        for r in range(rs):
            if pos[r] >= back:
                shmat[back - 1, r, 0, r - back] = 1.0
            else:
                shmat[back - 1, r, 1, seq[r] * SAMPLE_LEN + SAMPLE_LEN + pos[r] - back] = 1.0
    shmat = np.repeat(shmat[:, :, :, None, :], PIECES, axis=3).reshape((CONV_W - 1) * rs, 2 * PIECES * rs)
    itile, cmask = _segment_tables(rs, btril)
    tile4 = lambda a: jnp.asarray(np.tile(a, (1, SSM_GROUPS)))
    qmask = (seq[:, None] == (np.arange(SEQ_PER_STEP * GLA_DK) // GLA_DK)[None, :]).astype(np.float32)
    smask = (seq[:, None] == (np.arange(SEQ_PER_STEP * GROUP_LANES) // GROUP_LANES)[None, :]).astype(np.float32)
    rowseq = np.broadcast_to(seq[:, None], (rs, SSM_DSTATE)).astype(np.int32)
    return (jnp.asarray(bs3, dtype=BF16), jnp.asarray(shmat, dtype=BF16),
            _expand_table(SSM_HEADDIM), _expand_table(rs), tile4(itile), tile4(cmask),
            _head_block_mask(SSM_HPG, rs), jnp.asarray(qmask, dtype=BF16), jnp.asarray(smask, dtype=BF16),
            jnp.asarray(rowseq), jnp.asarray(pos.reshape(rs, 1).astype(np.int32)))


def _prep_weights(p):
    offs = np.cumsum((0,) + IN_SPLITS)
    w_in = p["w_in"]
    piece = lambda i, j: w_in[:, :, offs[i]:offs[j]].astype(BF16)
    lane_pad = lambda w: jnp.pad(w, ((0, 0), (0, 0), (0, LANES - w.shape[-1])))
    rep3 = lambda a: lane_pad(jnp.concatenate([a] * PIECES, axis=-1))
    return dict(
        wqkvg=piece(0, 4), wa1=lane_pad(piece(4, 5)), wzx=piece(5, 7), wdt3=rep3(piece(7, 8)), wgate=piece(8, 10),
        wa2=jnp.pad(p["w_gla_a2"], ((0, 0), (0, LANES - GLA_GATE_RANK), (0, 0))).astype(BF16),
        dtb3=rep3(p["dt_bias"][:, None, :]), alog3=rep3(p["a_log"][:, None, :]),
        dskip_x=jnp.repeat(p["d_skip"], SSM_HEADDIM, axis=-1)[:, None, :],
        wpg=p["w_proj_gla"].astype(BF16), wps=p["w_proj_ssm"].astype(BF16), wo=p["w_out"].astype(BF16),
        gu1=p["w_ffn1_gu"].astype(BF16), d1=p["w_ffn1_down"].astype(BF16),
        gu2=p["w_ffn2_gu"].astype(BF16), d2=p["w_ffn2_down"].astype(BF16),
    )


PROMPT_TM = 512
SSM_IN_TM = 512
CORE_ROWS = 256


def _trunk(x, w, p, fw, *, tm, mixer):
    depth = p["w_in"].shape[0]
    row = lambda name, l: p[name][l].reshape(1, -1)
    for l in range(depth):
        x = _ffn(x, row("norm_ffn1", l), w["gu1"], w["d1"], fw, l=l, tm=tm, final=False)
        o, sg, y, sz, ga, gb = mixer(l, x)
        x = _merge(x, o, sg, y, sz, ga, gb, row("gla_norm", l), row("ssm_norm", l),
                   w["wpg"], w["wps"], w["wo"], l=l, tm=tm)
        x = _ffn(x, row("norm_ffn2", l), w["gu2"], w["d2"], fw, l=l, tm=tm, final=l == depth - 1)
    return x


def _trunk_prompt(x, w, p, fw, nb):
    consts = _prompt_consts(CORE_ROWS)
    row = lambda name, l: p[name][l].reshape(1, -1)
    glas, ssms, convs = [], [], []

    def mixer(l, x):
        q, k, v, sg, ga, gb, la = _gla_in(x, row("norm_mix", l), w["wqkvg"], w["wgate"], w["wa1"], w["wa2"],
                                          row("b_gla_a2", l), l=l, tm=PROMPT_TM)
        sz, xa, dt, tail = _ssm_in_prompt(x, row("norm_mix", l), w["wzx"], w["wdt3"], w["dtb3"][l],
                                          p["conv_w"][l], row("conv_b", l), l=l, nb=nb, tm=SSM_IN_TM)
        o, y, s_fin, h_fin = _core_prompt(q, k, v, la, xa, dt, w["alog3"][l], w["dskip_x"][l], consts,
                                          nb=nb, rows=CORE_ROWS)
        glas.append(s_fin)
        ssms.append(h_fin.reshape(nb, SSM_HEADS, SSM_HEADDIM, SSM_DSTATE))
        convs.append(tail)
        return o, sg, y, sz, ga, gb

    y = _trunk(x, w, p, fw, tm=PROMPT_TM, mixer=mixer)
    return y, jnp.stack(glas), jnp.stack(ssms), jnp.stack(convs)


def _trunk_sample(x, state_gla, state_ssm, state_conv, w, p, fw):
    consts = _sample_consts()
    t = x.shape[0]
    nseq = t // SAMPLE_LEN
    depth = state_gla.shape[0]
    row = lambda name, l: p[name][l].reshape(1, -1)
    h_all = state_ssm.reshape(depth, nseq, SSM_INNER, SSM_DSTATE)
    new_states = [None]
    convs = []

    def mixer(l, x):
        q, k, v, sg, ga, gb, la = _gla_in(x, row("norm_mix", l), w["wqkvg"], w["wgate"], w["wa1"], w["wa2"],
                                          row("b_gla_a2", l), l=l, tm=t)
        sz, xr, dt = _ssm_in_sample(x, row("norm_mix", l), w["wzx"], w["wdt3"], w["dtb3"][l], l=l, tm=t)
        c4 = jnp.pad(state_conv[l], ((0, 0), (SAMPLE_LEN - (CONV_W - 1), 0), (0, 0))).reshape(t, CONV_DIM)
        o, y, s1, h1 = _core_sample(q, k, v, la, xr, c4, dt, state_gla, h_all, p["conv_w"][l],
                                    row("conv_b", l), w["alog3"][l], w["dskip_x"][l], consts,
                                    l=l, new_states=new_states[0])
        new_states[0] = (s1, h1)
        convs.append(xr.reshape(nseq, SAMPLE_LEN, CONV_DIM)[:, SAMPLE_LEN - (CONV_W - 1):])
        return o, sg, y, sz, ga, gb

    y = _trunk(x, w, p, fw, tm=t, mixer=mixer)
    s1, h1 = new_states[0]
    return y, s1, h1.reshape(state_ssm.shape), jnp.stack(convs)


def kernel(x_prompt, x_sample, state_gla, state_ssm, state_conv, norm_ffn1, w_ffn1_gu, w_ffn1_down, norm_mix, w_in, w_gla_a2, b_gla_a2, gla_norm, conv_w, conv_b, dt_bias, a_log, d_skip, ssm_norm, w_proj_gla, w_proj_ssm, w_out, norm_ffn2, w_ffn2_gu, w_ffn2_down, norm_final):
    p = dict(norm_ffn1=norm_ffn1, w_ffn1_gu=w_ffn1_gu, w_ffn1_down=w_ffn1_down, norm_mix=norm_mix,
             w_in=w_in, w_gla_a2=w_gla_a2, b_gla_a2=b_gla_a2, gla_norm=gla_norm, conv_w=conv_w,
             conv_b=conv_b, dt_bias=dt_bias, a_log=a_log, d_skip=d_skip, ssm_norm=ssm_norm,
             w_proj_gla=w_proj_gla, w_proj_ssm=w_proj_ssm, w_out=w_out, norm_ffn2=norm_ffn2,
             w_ffn2_gu=w_ffn2_gu, w_ffn2_down=w_ffn2_down)
    w = _prep_weights(p)
    fw = norm_final.reshape(1, D_MODEL)
    bp, lp, _ = x_prompt.shape
    bs, ls, _ = x_sample.shape
    assert ls == SAMPLE_LEN and lp % CORE_ROWS == 0 and lp % PROMPT_TM == 0
    assert bs % SEQ_PER_STEP == 0 and bp % SEQS_PER_STEP == 0
    yp, gla_p, ssm_p, conv_p = _trunk_prompt(x_prompt.reshape(bp * lp, D_MODEL), w, p, fw, bp)
    ys, gla_s, ssm_s, conv_s = _trunk_sample(x_sample.reshape(bs * ls, D_MODEL), state_gla, state_ssm,
                                             state_conv, w, p, fw)
    return (yp.reshape(bp, lp, D_MODEL), ys.reshape(bs, ls, D_MODEL),
            gla_p, ssm_p, conv_p, gla_s, ssm_s, conv_s)
```

```python
import functools

import numpy as np
import jax
import jax.numpy as jnp
from jax import lax
from jax.experimental import pallas as pl
from jax.experimental.pallas import tpu as pltpu

F32 = jnp.float32
BF16 = jnp.bfloat16

D_MODEL = 1024
EPS = 1e-6
FFN_DIM = 2816
GLA_HEADS = 4
GLA_DK = 128
GLA_DV = 256
GLA_DK_TOTAL = GLA_HEADS * GLA_DK
GLA_DV_TOTAL = GLA_HEADS * GLA_DV
GLA_GATE_RANK = 16
GLA_TAU = 16.0
SSM_INNER = 2048
SSM_HEADDIM = 64
SSM_HEADS = 32
SSM_GROUPS = 4
SSM_HPG = SSM_HEADS // SSM_GROUPS
SSM_DSTATE = 128
CONV_W = 4
CONV_DIM = SSM_INNER + 2 * SSM_GROUPS * SSM_DSTATE
CHUNK = 64
IN_SPLITS = (GLA_DK_TOTAL, GLA_DK_TOTAL, GLA_DV_TOTAL, GLA_DV_TOTAL, GLA_GATE_RANK,
             SSM_INNER, CONV_DIM, SSM_HEADS, D_MODEL, D_MODEL)

LANES = 128
BF16_ROWS = 16
FFN_TILE = 256
GROUP_LANES = SSM_HPG * SSM_HEADDIM
PIECES = 3
VMEM_LIMIT = 56 * 1024 * 1024

NT_DIMS = (((1,), (1,)), ((), ()))
TN_DIMS = (((0,), (0,)), ((), ()))


def _dot(a, b):
    return jnp.dot(a, b, preferred_element_type=F32)


def _dot_nt(a, b):
    return lax.dot_general(a, b, NT_DIMS, preferred_element_type=F32)


def _dot_tn(a, b):
    return lax.dot_general(a, b, TN_DIMS, preferred_element_type=F32)


def _rms(x):
    return x * lax.rsqrt(jnp.mean(x * x, axis=-1, keepdims=True) + EPS)


def _sigmoid(x):
    return 0.5 * jnp.tanh(0.5 * x) + 0.5


def _silu(x):
    h = 0.5 * x
    return h * jnp.tanh(h) + h


def _softplus(x):
    return jnp.maximum(x, 0.0) + jnp.log1p(jnp.exp(-jnp.abs(x)))


def _split3(x):
    hi = x.astype(BF16)
    r1 = x - hi.astype(F32)
    mid = r1.astype(BF16)
    lo = (r1 - mid.astype(F32)).astype(BF16)
    return hi, mid, lo


def _stack3(x):
    return jnp.concatenate(_split3(x), axis=0)


def _pieces_by_lane(x):
    hi, mid, lo = (piece.astype(F32) for piece in _split3(x))
    lane = lax.broadcasted_iota(jnp.int32, x.shape, 1)
    picked = jnp.where(lane < SSM_HEADS, hi, jnp.where(lane < 2 * SSM_HEADS, mid, lo))
    return picked.astype(BF16)


def _pieces_by_row(x, piece_of_row):
    hi, mid, lo = (piece.astype(F32) for piece in _split3(x))
    idx = jnp.broadcast_to(piece_of_row, x.shape)
    picked = jnp.where(idx == 0, hi, jnp.where(idx == 1, mid, jnp.where(idx == 2, lo, 0.0)))
    return picked.astype(BF16)


def _params(n_arbitrary=1):
    return pltpu.CompilerParams(dimension_semantics=("arbitrary",) * n_arbitrary,
                                vmem_limit_bytes=VMEM_LIMIT)


def _const_spec(shape):
    nd = len(shape)
    return pl.BlockSpec(shape, lambda *_: (0,) * nd)


def _layer_spec(l, shape):
    return pl.BlockSpec((None,) + shape, lambda *_: (l, 0, 0), pipeline_mode=pl.Buffered(1))


def _ffn_kernel(x_ref, nw_ref, wgu_ref, wd_ref, fw_ref, o_ref, acc_ref, *, final):
    x = x_ref[...]
    xn = (_rms(x) * nw_ref[...]).astype(BF16)
    for c in range(FFN_DIM // FFN_TILE):
        lo = c * FFN_TILE
        g = _dot(xn, wgu_ref[:, lo:lo + FFN_TILE])
        u = _dot(xn, wgu_ref[:, FFN_DIM + lo:FFN_DIM + lo + FFN_TILE])
        a = (_silu(g) * u).astype(BF16)
        part = _dot(a, wd_ref[lo:lo + FFN_TILE, :])
        if c == 0:
            acc_ref[...] = part
        else:
            acc_ref[...] += part
    y = x + 0.5 * acc_ref[...]
    if final:
        y = _rms(y) * fw_ref[...]
    o_ref[...] = y


def _ffn(x, nw, wgu, wd, fw, *, l, tm, final):
    t = x.shape[0]
    return pl.pallas_call(
        functools.partial(_ffn_kernel, final=final),
        grid=(t // tm,),
        in_specs=[pl.BlockSpec((tm, D_MODEL), lambda i: (i, 0)),
                  _const_spec((1, D_MODEL)),
                  _layer_spec(l, (D_MODEL, 2 * FFN_DIM)),
                  _layer_spec(l, (FFN_DIM, D_MODEL)),
                  _const_spec((1, D_MODEL))],
        out_specs=pl.BlockSpec((tm, D_MODEL), lambda i: (i, 0)),
        out_shape=jax.ShapeDtypeStruct((t, D_MODEL), F32),
        scratch_shapes=[pltpu.VMEM((tm, D_MODEL), F32)],
        compiler_params=_params(),
        name="ffn_final" if final else "ffn",
    )(x, nw, wgu, wd, fw)


def _gla_in_kernel(x_ref, nw_ref, wqkvg_ref, wgate_ref, wa1_ref, wa2_ref, ba2_ref,
                   q_ref, k_ref, v_ref, sg_ref, ga_ref, gb_ref, la_ref):
    u = (_rms(x_ref[...]) * nw_ref[...]).astype(BF16)
    q_ref[...] = _dot_nt(u, wqkvg_ref[0:512, :]).astype(BF16)
    k_ref[...] = _dot_nt(u, wqkvg_ref[512:1024, :]).astype(BF16)
    v_ref[...] = _dot_nt(u, wqkvg_ref[1024:2048, :]).astype(BF16)
    sg_ref[...] = _silu(_dot_nt(u, wqkvg_ref[2048:3072, :])).astype(BF16)
    ga_ref[...] = _sigmoid(_dot_nt(u, wgate_ref[0:1024, :])).astype(BF16)
    gb_ref[...] = _sigmoid(_dot_nt(u, wgate_ref[1024:2048, :])).astype(BF16)
    a_lr = _dot_nt(u, wa1_ref[...]).astype(BF16)
    z = _dot(a_lr, wa2_ref[...]) + ba2_ref[...]
    la_ref[...] = -_softplus(-z) / GLA_TAU


def _gla_in(x, nw, wqkvg, wgate, wa1, wa2, ba2, *, l, tm):
    t = x.shape[0]
    row = lambda width: pl.BlockSpec((tm, width), lambda i: (i, 0))
    out = lambda width, dt: jax.ShapeDtypeStruct((t, width), dt)
    return pl.pallas_call(
        _gla_in_kernel,
        grid=(t // tm,),
        in_specs=[row(D_MODEL), _const_spec((1, D_MODEL)),
                  _layer_spec(l, (3072, D_MODEL)), _layer_spec(l, (2048, D_MODEL)),
                  _layer_spec(l, (LANES, D_MODEL)), _layer_spec(l, (LANES, GLA_DK_TOTAL)),
                  _const_spec((1, GLA_DK_TOTAL))],
        out_specs=[row(512), row(512), row(1024), row(1024), row(1024), row(1024), row(512)],
        out_shape=[out(512, BF16), out(512, BF16), out(1024, BF16), out(1024, BF16),
                   out(1024, BF16), out(1024, BF16), out(512, F32)],
        compiler_params=_params(),
        name="gla_in",
    )(x, nw, wqkvg, wgate, wa1, wa2, ba2)


ZX_COLS = SSM_INNER + CONV_DIM


CONV_COLS = 512
SUBLANES = 8


def _ssm_in_prompt_kernel(x_ref, nw_ref, wzx_ref, wdt_ref, dtb_ref, cw_ref, cb_ref,
                          sz_ref, xa_ref, dt_ref, tail_ref, carry_ref, *, tm):
    j = pl.program_id(1)

    @pl.when(j == 0)
    def _():
        carry_ref[...] = jnp.zeros(carry_ref.shape, F32)

    u = (_rms(x_ref[...]) * nw_ref[...]).astype(BF16)
    sz_ref[...] = _silu(_dot_nt(u, wzx_ref[0:SSM_INNER, :])).astype(BF16)
    dt_ref[...] = _softplus(_dot_nt(u, wdt_ref[...]) + dtb_ref[...])
    tiles = (tm // SUBLANES, SUBLANES, CONV_COLS)
    sub = lax.broadcasted_iota(jnp.int32, tiles, 1)
    for cc in range(CONV_DIM // CONV_COLS):
        cols = slice(cc * CONV_COLS, (cc + 1) * CONV_COLS)
        xr = _dot_nt(u, wzx_ref[SSM_INNER + cc * CONV_COLS:SSM_INNER + (cc + 1) * CONV_COLS, :])
        x3 = xr.reshape(tiles)
        p3 = jnp.concatenate([carry_ref[:, cols], xr[0:tm - SUBLANES, :]], axis=0).reshape(tiles)
        acc = cb_ref[:, cols] + xr * cw_ref[CONV_W - 1:CONV_W, cols]
        for i in range(CONV_W - 1):
            back = CONV_W - 1 - i
            sh = pltpu.roll(jnp.where(sub >= SUBLANES - back, p3, x3), back, axis=1)
            acc = acc + sh.reshape(tm, CONV_COLS) * cw_ref[i:i + 1, cols]
        xa_ref[:, cols] = _silu(acc).astype(BF16)
        carry_ref[:, cols] = xr[tm - SUBLANES:tm, :]
    tail_ref[0] = carry_ref[SUBLANES - (CONV_W - 1):SUBLANES, :]


def _ssm_in_prompt(x, nw, wzx, wdt, dtb, cw, cb, *, l, nb, tm):
    t = x.shape[0]
    nj = t // nb // tm
    row = lambda width: pl.BlockSpec((tm, width), lambda b, j: (b * nj + j, 0))
    return pl.pallas_call(
        functools.partial(_ssm_in_prompt_kernel, tm=tm),
        grid=(nb, nj),
        in_specs=[row(D_MODEL), _const_spec((1, D_MODEL)),
                  _layer_spec(l, (ZX_COLS, D_MODEL)), _layer_spec(l, (LANES, D_MODEL)),
                  _const_spec((1, LANES)), _const_spec((CONV_W, CONV_DIM)), _const_spec((1, CONV_DIM))],
        out_specs=[row(SSM_INNER), row(CONV_DIM), row(LANES),
                   pl.BlockSpec((1, CONV_W - 1, CONV_DIM), lambda b, j: (b, 0, 0))],
        out_shape=[jax.ShapeDtypeStruct((t, SSM_INNER), BF16),
                   jax.ShapeDtypeStruct((t, CONV_DIM), BF16),
                   jax.ShapeDtypeStruct((t, LANES), F32),
                   jax.ShapeDtypeStruct((nb, CONV_W - 1, CONV_DIM), F32)],
        scratch_shapes=[pltpu.VMEM((SUBLANES, CONV_DIM), F32)],
        compiler_params=_params(n_arbitrary=2),
        name="ssm_in_prompt",
    )(x, nw, wzx, wdt, dtb, cw, cb)


def _ssm_in_sample_kernel(x_ref, nw_ref, wzx_ref, wdt_ref, dtb_ref, sz_ref, xr_ref, dt_ref):
    u = (_rms(x_ref[...]) * nw_ref[...]).astype(BF16)
    sz_ref[...] = _silu(_dot_nt(u, wzx_ref[0:SSM_INNER, :])).astype(BF16)
    xr_ref[...] = _dot_nt(u, wzx_ref[SSM_INNER:ZX_COLS, :])
    dt_ref[...] = _softplus(_dot_nt(u, wdt_ref[...]) + dtb_ref[...])


def _ssm_in_sample(x, nw, wzx, wdt, dtb, *, l, tm):
    t = x.shape[0]
    row = lambda width: pl.BlockSpec((tm, width), lambda i: (i, 0))
    return pl.pallas_call(
        _ssm_in_sample_kernel,
        grid=(t // tm,),
        in_specs=[row(D_MODEL), _const_spec((1, D_MODEL)),
                  _layer_spec(l, (ZX_COLS, D_MODEL)), _layer_spec(l, (LANES, D_MODEL)),
                  _const_spec((1, LANES))],
        out_specs=[row(SSM_INNER), row(CONV_DIM), row(LANES)],
        out_shape=[jax.ShapeDtypeStruct((t, SSM_INNER), BF16),
                   jax.ShapeDtypeStruct((t, CONV_DIM), F32),
                   jax.ShapeDtypeStruct((t, LANES), F32)],
        compiler_params=_params(),
        name="ssm_in_sample",
    )(x, nw, wzx, wdt, dtb)


def _gla_rows(q, k, bcum, blast):
    qf = q.astype(F32) * (GLA_DK ** -0.5)
    kf = k.astype(F32)
    q_in = (qf * jnp.exp(bcum)).astype(BF16)
    k_in = (kf * jnp.exp(-bcum)).astype(BF16)
    k_out = (kf * jnp.exp(blast - bcum)).astype(BF16)
    return q_in, k_in, k_out


def _segment_weights(ac_s, itile, cmask):
    arow = jnp.sum(ac_s * itile, axis=0, keepdims=True)
    return jnp.where(cmask > 0.5, jnp.exp(ac_s - arow), 0.0)


PAIR = 2


SEQS_PER_STEP = 1


def _core_prompt_kernel(q_ref, k_ref, v_ref, la_ref, xa_ref, dt_ref, alog3_ref, dskip_ref,
                        btril3_ref, e3_ref, itile_ref, cmask_ref, bmask_ref,
                        o_ref, y_ref, gla_ref, ssm_ref, s_ref, ht_ref, *, rows, nj):
    j = pl.program_id(1)

    @pl.when(j == 0)
    def _():
        s_ref[...] = jnp.zeros(s_ref.shape, F32)
        ht_ref[...] = jnp.zeros(ht_ref.shape, F32)

    btril3 = btril3_ref[...]
    causal = btril3[0:CHUNK, 0:CHUNK] > 0.5
    ones16 = jnp.ones((BF16_ROWS, LANES), BF16)
    row16 = lax.broadcasted_iota(jnp.int32, (BF16_ROWS, 1), 0)
    itile = itile_ref[...]
    cmask = cmask_ref[...]
    bmask = bmask_ref[...]
    e3 = e3_ref[...]
    a3 = -jnp.exp(alog3_ref[...])

    bcum_all, dtx_all, acx_all = [], [], []
    for sq in range(SEQS_PER_STEP):
        bcum_all.append(_dot(btril3, _stack3(la_ref[sq])))
        dt3 = dt_ref[sq]
        acum3 = _dot(btril3, _stack3(dt3 * a3))
        dtx_all.append(_dot(_pieces_by_lane(dt3), e3))
        acx_all.append(_dot(_pieces_by_lane(acum3), e3))

    chunks = [slice(c * CHUNK, (c + 1) * CHUNK) for c in range(rows // CHUNK)]
    kls = [slice(h * GLA_DK, (h + 1) * GLA_DK) for h in range(GLA_HEADS)]
    vls = [slice(h * GLA_DV, (h + 1) * GLA_DV) for h in range(GLA_HEADS)]
    gls = [slice(g * GROUP_LANES, (g + 1) * GROUP_LANES) for g in range(SSM_GROUPS)]

    for sq in range(SEQS_PER_STEP):
        prep = []
        for r in chunks:
            bcum = bcum_all[sq][r, :]
            blast = bcum[CHUNK - 1:CHUNK, :]
            q_in, k_in, k_out = _gla_rows(q_ref[sq, r, :], k_ref[sq, r, :], bcum, blast)
            bl_pieces = _pieces_by_row(jnp.broadcast_to(blast, (BF16_ROWS, GLA_DK_TOTAL)), row16)
            prep.append((q_in, k_in, k_out, bl_pieces, v_ref[sq, r, :]))
        pairs = [(c, h) for h in range(GLA_HEADS) for c in range(len(chunks))]
        qk = {(c, h): _dot_nt(prep[c][0][:, kls[h]], prep[c][1][:, kls[h]]) for c, h in pairs}
        sc = {ch: jnp.where(causal, qk[ch], 0.0).astype(BF16) for ch in pairs}
        intra = {}
        for c, h in pairs:
            _, _, k_out, bl_pieces, v = prep[c]
            dec = jnp.exp(_dot_tn(bl_pieces[:, kls[h]], ones16))
            intra[c, h] = (_dot(sc[c, h], v[:, vls[h]]), _dot_tn(k_out[:, kls[h]], v[:, vls[h]]), dec)
        for c, r in enumerate(chunks):
            for h, (kl, vl) in enumerate(zip(kls, vls)):
                o_intra, ds, dec = intra[c, h]
                s_old = s_ref[sq, h]
                o_ref[sq, r, vl] = (o_intra + _dot(prep[c][0][:, kl], s_old.astype(BF16))).astype(BF16)
                s_ref[sq, h] = s_old * jnp.concatenate([dec, dec], axis=1) + ds
        prep = []
        for r in chunks:
            acx = acx_all[sq][r, :]
            alast = acx[CHUNK - 1:CHUNK, :]
            xa = xa_ref[sq, r, :]
            xs = xa[:, 0:SSM_INNER].astype(F32)
            xg = xs * dtx_all[sq][r, :]
            xw = (xg * jnp.exp(alast - acx)).astype(BF16)
            seg = _segment_weights(acx, itile, cmask)
            prep.append((xa, xs, xg.astype(BF16), xw, jnp.exp(acx), seg, jnp.exp(alast)))
        pairs = [(c, g) for g in range(SSM_GROUPS) for c in range(len(chunks))]
        b_of = lambda c, g: prep[c][0][:, SSM_INNER + g * SSM_DSTATE:SSM_INNER + (g + 1) * SSM_DSTATE]
        c_of = lambda c, g: prep[c][0][:, SSM_INNER + (SSM_GROUPS + g) * SSM_DSTATE:
                                       SSM_INNER + (SSM_GROUPS + g + 1) * SSM_DSTATE]
        cbx = {(c, g): _dot_nt(c_of(c, g), jnp.concatenate([b_of(c, g)] * SSM_HPG, axis=0)) for c, g in pairs}
        w = {(c, g): (cbx[c, g] * prep[c][5][:, gls[g]]).astype(BF16) for c, g in pairs}
        intra = {}
        for c, g in pairs:
            xs, xgb, xw = prep[c][1], prep[c][2], prep[c][3]
            yd = []
            for pr in range(SSM_HPG // PAIR):
                lo = g * GROUP_LANES + pr * LANES
                xblk = jnp.concatenate([xgb[:, lo:lo + LANES]] * PAIR, axis=0) * bmask
                yd.append(_dot(w[c, g][:, pr * LANES:(pr + 1) * LANES], xblk))
            y_part = jnp.concatenate(yd, axis=1) + dskip_ref[:, gls[g]] * xs[:, gls[g]]
            intra[c, g] = (c_of(c, g), y_part, _dot_tn(b_of(c, g), xw[:, gls[g]]))
        for c, r in enumerate(chunks):
            eac, dec_row = prep[c][4], prep[c][6]
            for g, gl in enumerate(gls):
                cg, y_part, ds_t = intra[c, g]
                h_old = ht_ref[sq, :, gl]
                y_ref[sq, r, gl] = (y_part + _dot(cg, h_old.astype(BF16)) * eac[:, gl]).astype(BF16)
                ht_ref[sq, :, gl] = h_old * dec_row[:, gl] + ds_t

    @pl.when(j == nj - 1)
    def _():
        for sq in range(SEQS_PER_STEP):
            gla_ref[sq] = s_ref[sq]
            ssm_ref[sq] = ht_ref[sq].T


def _core_prompt(q, k, v, la, xa, dt, alog3, dskip_x, consts, *, nb, rows):
    t = q.shape[0]
    nq = SEQS_PER_STEP
    half, length = nb // nq, t // nb
    nj = length // rows
    split = lambda a: a.reshape(nq, half, length, a.shape[-1])
    row = lambda width: pl.BlockSpec((nq, None, rows, width), lambda b, j: (0, b, j, 0))
    o, y, s_fin, h_fin = pl.pallas_call(
        functools.partial(_core_prompt_kernel, rows=rows, nj=nj),
        grid=(half, nj),
        in_specs=[row(512), row(512), row(1024), row(512), row(CONV_DIM), row(LANES),
                  _const_spec((1, LANES)), _const_spec((1, SSM_INNER))]
                 + [_const_spec(c.shape) for c in consts],
        out_specs=[row(GLA_DV_TOTAL), row(SSM_INNER),
                   pl.BlockSpec((nq, None, GLA_HEADS, GLA_DK, GLA_DV), lambda b, j: (0, b, 0, 0, 0)),
                   pl.BlockSpec((nq, None, SSM_INNER, SSM_DSTATE), lambda b, j: (0, b, 0, 0))],
        out_shape=[jax.ShapeDtypeStruct((nq, half, length, GLA_DV_TOTAL), BF16),
                   jax.ShapeDtypeStruct((nq, half, length, SSM_INNER), BF16),
                   jax.ShapeDtypeStruct((nq, half, GLA_HEADS, GLA_DK, GLA_DV), F32),
                   jax.ShapeDtypeStruct((nq, half, SSM_INNER, SSM_DSTATE), F32)],
        scratch_shapes=[pltpu.VMEM((nq, GLA_HEADS, GLA_DK, GLA_DV), F32),
                        pltpu.VMEM((nq, SSM_DSTATE, SSM_INNER), F32)],
        compiler_params=_params(n_arbitrary=2),
        name="core_prompt",
    )(split(q), split(k), split(v), split(la), split(xa), split(dt), alog3, dskip_x, *consts)
    return (o.reshape(t, GLA_DV_TOTAL), y.reshape(t, SSM_INNER),
            s_fin.reshape(nb, GLA_HEADS, GLA_DK, GLA_DV), h_fin.reshape(nb, SSM_INNER, SSM_DSTATE))


SEQ_PER_STEP = 4
SAMPLE_LEN = 4
SAMPLE_ROWS = SEQ_PER_STEP * SAMPLE_LEN


def _core_sample_kernel(q_ref, k_ref, v_ref, la_ref, xr_ref, c4_ref, dt_ref, s0_ref, h0_ref,
                        cw_ref, cb_ref, alog3_ref, dskip_ref,
                        bs3_ref, shmat_ref, e3_ref, es3_ref, itile_ref, cmask_ref, bmask_ref,
                        qmask_ref, smask_ref, pos_ref, *rest):
    o_ref, y_ref, s1_ref, h1_ref = rest[-4:]
    rs = SAMPLE_ROWS
    bs3 = bs3_ref[...]
    same_causal = bs3[0:rs, 0:rs] > 0.5
    ones16 = jnp.ones((BF16_ROWS, LANES), BF16)
    pos = pos_ref[...]
    xr = xr_ref[...]
    shifted = _dot(shmat_ref[...], jnp.concatenate(_split3(xr) + _split3(c4_ref[...]), axis=0))
    acc = cb_ref[...] + xr * cw_ref[CONV_W - 1:CONV_W, :]
    for i in range(CONV_W - 1):
        back = CONV_W - 1 - i
        acc = acc + shifted[(back - 1) * rs:back * rs, :] * cw_ref[i:i + 1, :]
    xa = _silu(acc)
    sums = _dot(bs3, _stack3(la_ref[...]))
    bcum, blast = sums[0:rs, :], sums[rs:2 * rs, :]
    q_in, k_in, k_out = _gla_rows(q_ref[...], k_ref[...], bcum, blast)
    bl_pieces = _pieces_by_row(blast, pos)
    v = v_ref[...]
    qmask = qmask_ref[...]
    seq_blk = lambda a: jnp.concatenate([a] * SEQ_PER_STEP, axis=1)
    for h in range(GLA_HEADS):
        kl = slice(h * GLA_DK, (h + 1) * GLA_DK)
        vl = slice(h * GLA_DV, (h + 1) * GLA_DV)
        sc = jnp.where(same_causal, _dot_nt(q_in[:, kl], k_in[:, kl]), 0.0).astype(BF16)
        s_old = s0_ref[:, h].reshape(SEQ_PER_STEP * GLA_DK, GLA_DV)
        o_h = _dot(sc, v[:, vl]) + _dot(seq_blk(q_in[:, kl]) * qmask, s_old.astype(BF16))
        o_ref[:, vl] = o_h.astype(BF16)
        ds = _dot_tn(seq_blk(k_out[:, kl]) * qmask, v[:, vl])
        dec = jnp.exp(_dot_tn(seq_blk(bl_pieces[:, kl]) * qmask, ones16))
        s_new = s_old * jnp.concatenate([dec, dec], axis=1) + ds
        s1_ref[:, h] = s_new.reshape(SEQ_PER_STEP, GLA_DK, GLA_DV)
    dt3 = dt_ref[...]
    sums = _dot(bs3, _stack3(dt3 * (-jnp.exp(alog3_ref[...]))))
    acum3, alast3 = sums[0:rs, :], sums[rs:2 * rs, :]
    p_ac = _pieces_by_lane(acum3)
    wide = _dot(jnp.concatenate([_pieces_by_lane(dt3), p_ac, _pieces_by_lane(alast3)], axis=0), e3_ref[...])
    dtx, acx, alx = wide[0:rs, :], wide[rs:2 * rs, :], wide[2 * rs:3 * rs, :]
    xs = xa[:, 0:SSM_INNER]
    xg = xs * dtx
    xgb = xg.astype(BF16)
    xw = xg * jnp.exp(alx - acx)
    eac = jnp.exp(acx)
    seg = _segment_weights(_dot(p_ac, es3_ref[...]), itile_ref[...], cmask_ref[...])
    al_pieces = _pieces_by_row(alx, pos)
    smask = smask_ref[...]
    sl = SSM_HPG * rs
    for g in range(SSM_GROUPS):
        gl = slice(g * GROUP_LANES, (g + 1) * GROUP_LANES)
        bg = xa[:, SSM_INNER + g * SSM_DSTATE:SSM_INNER + (g + 1) * SSM_DSTATE].astype(BF16)
        cg = xa[:, SSM_INNER + (SSM_GROUPS + g) * SSM_DSTATE:
                SSM_INNER + (SSM_GROUPS + g + 1) * SSM_DSTATE].astype(BF16)
        cbx = _dot_nt(cg, jnp.concatenate([bg] * SSM_HPG, axis=0))
        w = (cbx * seg[:, g * sl:(g + 1) * sl]).astype(BF16)
        xblk = jnp.concatenate([xgb[:, gl]] * SSM_HPG, axis=0) * bmask_ref[...]
        yd = _dot(w, xblk)
        h_old = [h0_ref[s, gl, :] for s in range(SEQ_PER_STEP)]
        yo = _dot_nt(seq_blk(cg) * qmask, jnp.concatenate([hs.astype(BF16) for hs in h_old], axis=1))
        y = yd + yo * eac[:, gl] + dskip_ref[:, gl] * xs[:, gl]
        y_ref[:, gl] = y.astype(BF16)
        ds = _dot_tn(seq_blk(xw[:, gl]).astype(BF16) * smask, bg)
        dec = jnp.exp(_dot_tn(seq_blk(al_pieces[:, gl]) * smask, ones16))
        for s in range(SEQ_PER_STEP):
            sr = slice(s * GROUP_LANES, (s + 1) * GROUP_LANES)
            h1_ref[s, gl, :] = h_old[s] * dec[sr, :] + ds[sr, :]


def _core_sample(q, k, v, la, xr, c4, dt, s_all, h_all, cw, cb, alog3, dskip_x, consts, *, l, new_states):
    t = q.shape[0]
    rs = SAMPLE_ROWS
    nseq = t // SAMPLE_LEN
    row = lambda width: pl.BlockSpec((rs, width), lambda i: (i, 0))
    s_spec = pl.BlockSpec((None, SEQ_PER_STEP, GLA_HEADS, GLA_DK, GLA_DV), lambda i: (l, i, 0, 0, 0))
    h_spec = pl.BlockSpec((None, SEQ_PER_STEP, SSM_INNER, SSM_DSTATE), lambda i: (l, i, 0, 0))
    inputs = [q, k, v, la, xr, c4, dt, s_all, h_all, cw, cb, alog3, dskip_x, *consts]
    in_specs = ([row(512), row(512), row(1024), row(512), row(CONV_DIM), row(CONV_DIM), row(LANES),
                 s_spec, h_spec,
                 _const_spec((CONV_W, CONV_DIM)), _const_spec((1, CONV_DIM)),
                 _const_spec((1, LANES)), _const_spec((1, SSM_INNER))]
                + [_const_spec(c.shape) for c in consts])
    aliases = {}
    if new_states is not None:
        aliases = {len(inputs): 2, len(inputs) + 1: 3}
        inputs += list(new_states)
        in_specs += [pl.BlockSpec(memory_space=pl.ANY)] * 2
    return pl.pallas_call(
        _core_sample_kernel,
        grid=(nseq // SEQ_PER_STEP,),
        in_specs=in_specs,
        out_specs=[row(GLA_DV_TOTAL), row(SSM_INNER), s_spec, h_spec],
        out_shape=[jax.ShapeDtypeStruct((t, GLA_DV_TOTAL), BF16),
                   jax.ShapeDtypeStruct((t, SSM_INNER), BF16),
                   jax.ShapeDtypeStruct(s_all.shape, F32),
                   jax.ShapeDtypeStruct(h_all.shape, F32)],
        input_output_aliases=aliases,
        compiler_params=_params(),
        name="core_sample",
    )(*inputs)


def _merge_kernel(x_ref, o_ref, sg_ref, y_ref, sz_ref, ga_ref, gb_ref,
                  gn_ref, sn_ref, wpg_ref, wps_ref, wo_ref, out_ref):
    o = o_ref[...].astype(F32)
    parts = []
    for h in range(GLA_HEADS):
        vl = slice(h * GLA_DV, (h + 1) * GLA_DV)
        parts.append(_rms(o[:, vl]) * gn_ref[:, vl])
    og = (jnp.concatenate(parts, axis=1) * sg_ref[...].astype(F32)).astype(BF16)
    yz = y_ref[...].astype(F32) * sz_ref[...].astype(F32)
    yn = (_rms(yz) * sn_ref[...]).astype(BF16)
    m = (ga_ref[...].astype(F32) * _dot(og, wpg_ref[...])
         + gb_ref[...].astype(F32) * _dot(yn, wps_ref[...]))
    out_ref[...] = x_ref[...] + _dot(m.astype(BF16), wo_ref[...])


def _merge(x, o, sg, y, sz, ga, gb, gn, sn, wpg, wps, wo, *, l, tm):
    t = x.shape[0]
    row = lambda width: pl.BlockSpec((tm, width), lambda i: (i, 0))
    return pl.pallas_call(
        _merge_kernel,
        grid=(t // tm,),
        in_specs=[row(D_MODEL), row(1024), row(1024), row(2048), row(2048), row(1024), row(1024),
                  _const_spec((1, GLA_DV_TOTAL)), _const_spec((1, SSM_INNER)),
                  _layer_spec(l, (GLA_DV_TOTAL, D_MODEL)), _layer_spec(l, (SSM_INNER, D_MODEL)),
                  _layer_spec(l, (D_MODEL, D_MODEL))],
        out_specs=row(D_MODEL),
        out_shape=jax.ShapeDtypeStruct((t, D_MODEL), F32),
        compiler_params=_params(),
        name="merge",
    )(x, o, sg, y, sz, ga, gb, gn, sn, wpg, wps, wo)


def _expand_table(width):
    t = np.zeros((LANES, SSM_HEADS * width), np.float32)
    for k in range(PIECES):
        for h in range(SSM_HEADS):
            t[k * SSM_HEADS + h, h * width:(h + 1) * width] = 1.0
    return jnp.asarray(t, dtype=BF16)


def _segment_tables(rows, allowed):
    s_of_lane = np.arange(SSM_HPG * rows) % rows
    itile = (np.arange(rows)[:, None] == s_of_lane[None, :]).astype(np.float32)
    cmask = allowed[:, s_of_lane].astype(np.float32)
    return itile, cmask


def _head_block_mask(heads, rows):
    hr = np.arange(heads * rows) // rows
    hc = np.arange(heads * SSM_HEADDIM) // SSM_HEADDIM
    return jnp.asarray((hr[:, None] == hc[None, :]).astype(np.float32), dtype=BF16)


def _prompt_consts(rows):
    idx = np.arange(rows)
    btril = ((idx[:, None] // CHUNK == idx[None, :] // CHUNK) & (idx[None, :] <= idx[:, None])).astype(np.float32)
    tril = np.tril(np.ones((CHUNK, CHUNK), np.float32))
    itile, cmask = _segment_tables(CHUNK, tril)
    tile4 = lambda a: jnp.asarray(np.tile(a, (1, SSM_GROUPS)))
    return (jnp.asarray(np.tile(btril, (1, PIECES)), dtype=BF16), _expand_table(SSM_HEADDIM),
            tile4(itile), tile4(cmask), _head_block_mask(PAIR, CHUNK))


def _sample_consts():
    rs = SAMPLE_ROWS
    seq = np.arange(rs) // SAMPLE_LEN
    pos = np.arange(rs) % SAMPLE_LEN
    same = (seq[:, None] == seq[None, :]).astype(np.float32)
    btril = same * (pos[None, :] <= pos[:, None])
    bs3 = np.tile(np.concatenate([btril, same], axis=0), (1, PIECES))
    shmat = np.zeros((CONV_W - 1, rs, 2, rs), np.float32)
    for back in range(1, CONV_W):
        for r in range(rs):
            if pos[r] >= back:
                shmat[back - 1, r, 0, r - back] = 1.0
            else:
                shmat[back - 1, r, 1, seq[r] * SAMPLE_LEN + SAMPLE_LEN + pos[r] - back] = 1.0
    shmat = np.repeat(shmat[:, :, :, None, :], PIECES, axis=3).reshape((CONV_W - 1) * rs, 2 * PIECES * rs)
    itile, cmask = _segment_tables(rs, btril)
    tile4 = lambda a: jnp.asarray(np.tile(a, (1, SSM_GROUPS)))
    assert GLA_DK == SSM_DSTATE == LANES
    qmask = (seq[:, None] == (np.arange(SEQ_PER_STEP * LANES) // LANES)[None, :]).astype(np.float32)
    smask = (seq[:, None] == (np.arange(SEQ_PER_STEP * GROUP_LANES) // GROUP_LANES)[None, :]).astype(np.float32)
    return (jnp.asarray(bs3, dtype=BF16), jnp.asarray(shmat, dtype=BF16),
            _expand_table(SSM_HEADDIM), _expand_table(rs), tile4(itile), tile4(cmask),
            _head_block_mask(SSM_HPG, rs), jnp.asarray(qmask, dtype=BF16), jnp.asarray(smask, dtype=BF16),
            jnp.asarray(pos.reshape(rs, 1).astype(np.int32)))


def _prep_weights(p):
    offs = np.cumsum((0,) + IN_SPLITS)
    w_in_t = jnp.swapaxes(p["w_in"], 1, 2)
    piece = lambda i, j: w_in_t[:, offs[i]:offs[j], :].astype(BF16)
    lane_pad = lambda w: jnp.pad(w, ((0, 0), (0, 0), (0, LANES - w.shape[-1])))
    row_pad = lambda w: jnp.pad(w, ((0, 0), (0, LANES - w.shape[1]), (0, 0)))
    rep3 = lambda a: lane_pad(jnp.concatenate([a] * PIECES, axis=-1))
    return dict(
        wqkvg=piece(0, 4), wa1=row_pad(piece(4, 5)), wzx=piece(5, 7), wgate=piece(8, 10),
        wdt3=row_pad(jnp.concatenate([piece(7, 8)] * PIECES, axis=1)),
        wa2=jnp.pad(p["w_gla_a2"], ((0, 0), (0, LANES - GLA_GATE_RANK), (0, 0))).astype(BF16),
        dtb3=rep3(p["dt_bias"][:, None, :]), alog3=rep3(p["a_log"][:, None, :]),
        dskip_x=jnp.repeat(p["d_skip"], SSM_HEADDIM, axis=-1)[:, None, :],
        wpg=p["w_proj_gla"].astype(BF16), wps=p["w_proj_ssm"].astype(BF16), wo=p["w_out"].astype(BF16),
        gu1=p["w_ffn1_gu"].astype(BF16), d1=p["w_ffn1_down"].astype(BF16),
        gu2=p["w_ffn2_gu"].astype(BF16), d2=p["w_ffn2_down"].astype(BF16),
    )


PROMPT_TM = 512
SSM_IN_TM = 512
CORE_ROWS = 256


def _trunk(x, w, p, fw, *, tm, mixer):
    depth = p["w_in"].shape[0]
    row = lambda name, l: p[name][l].reshape(1, -1)
    for l in range(depth):
        x = _ffn(x, row("norm_ffn1", l), w["gu1"], w["d1"], fw, l=l, tm=tm, final=False)
        o, sg, y, sz, ga, gb = mixer(l, x)
        x = _merge(x, o, sg, y, sz, ga, gb, row("gla_norm", l), row("ssm_norm", l),
                   w["wpg"], w["wps"], w["wo"], l=l, tm=tm)
        x = _ffn(x, row("norm_ffn2", l), w["gu2"], w["d2"], fw, l=l, tm=tm, final=l == depth - 1)
    return x


def _trunk_prompt(x, w, p, fw, nb):
    consts = _prompt_consts(CORE_ROWS)
    row = lambda name, l: p[name][l].reshape(1, -1)
    glas, ssms, convs = [], [], []

    def mixer(l, x):
        q, k, v, sg, ga, gb, la = _gla_in(x, row("norm_mix", l), w["wqkvg"], w["wgate"], w["wa1"], w["wa2"],
                                          row("b_gla_a2", l), l=l, tm=PROMPT_TM)
        sz, xa, dt, tail = _ssm_in_prompt(x, row("norm_mix", l), w["wzx"], w["wdt3"], w["dtb3"][l],
                                          p["conv_w"][l], row("conv_b", l), l=l, nb=nb, tm=SSM_IN_TM)
        o, y, s_fin, h_fin = _core_prompt(q, k, v, la, xa, dt, w["alog3"][l], w["dskip_x"][l], consts,
                                          nb=nb, rows=CORE_ROWS)
        glas.append(s_fin)
        ssms.append(h_fin.reshape(nb, SSM_HEADS, SSM_HEADDIM, SSM_DSTATE))
        convs.append(tail)
        return o, sg, y, sz, ga, gb

    y = _trunk(x, w, p, fw, tm=PROMPT_TM, mixer=mixer)
    return y, jnp.stack(glas), jnp.stack(ssms), jnp.stack(convs)


def _trunk_sample(x, state_gla, state_ssm, state_conv, w, p, fw):
    consts = _sample_consts()
    t = x.shape[0]
    nseq = t // SAMPLE_LEN
    depth = state_gla.shape[0]
    row = lambda name, l: p[name][l].reshape(1, -1)
    h_all = state_ssm.reshape(depth, nseq, SSM_INNER, SSM_DSTATE)
    new_states = [None]
    convs = []

    def mixer(l, x):
        q, k, v, sg, ga, gb, la = _gla_in(x, row("norm_mix", l), w["wqkvg"], w["wgate"], w["wa1"], w["wa2"],
                                          row("b_gla_a2", l), l=l, tm=t)
        sz, xr, dt = _ssm_in_sample(x, row("norm_mix", l), w["wzx"], w["wdt3"], w["dtb3"][l], l=l, tm=t)
        c4 = jnp.pad(state_conv[l], ((0, 0), (SAMPLE_LEN - (CONV_W - 1), 0), (0, 0))).reshape(t, CONV_DIM)
        o, y, s1, h1 = _core_sample(q, k, v, la, xr, c4, dt, state_gla, h_all, p["conv_w"][l],
                                    row("conv_b", l), w["alog3"][l], w["dskip_x"][l], consts,
                                    l=l, new_states=new_states[0])
        new_states[0] = (s1, h1)
        convs.append(xr.reshape(nseq, SAMPLE_LEN, CONV_DIM)[:, SAMPLE_LEN - (CONV_W - 1):])
        return o, sg, y, sz, ga, gb

    y = _trunk(x, w, p, fw, tm=t, mixer=mixer)
    s1, h1 = new_states[0]
    return y, s1, h1.reshape(state_ssm.shape), jnp.stack(convs)


def kernel(x_prompt, x_sample, state_gla, state_ssm, state_conv, norm_ffn1, w_ffn1_gu, w_ffn1_down, norm_mix, w_in, w_gla_a2, b_gla_a2, gla_norm, conv_w, conv_b, dt_bias, a_log, d_skip, ssm_norm, w_proj_gla, w_proj_ssm, w_out, norm_ffn2, w_ffn2_gu, w_ffn2_down, norm_final):
    p = dict(norm_ffn1=norm_ffn1, w_ffn1_gu=w_ffn1_gu, w_ffn1_down=w_ffn1_down, norm_mix=norm_mix,
             w_in=w_in, w_gla_a2=w_gla_a2, b_gla_a2=b_gla_a2, gla_norm=gla_norm, conv_w=conv_w,
             conv_b=conv_b, dt_bias=dt_bias, a_log=a_log, d_skip=d_skip, ssm_norm=ssm_norm,
             w_proj_gla=w_proj_gla, w_proj_ssm=w_proj_ssm, w_out=w_out, norm_ffn2=norm_ffn2,
             w_ffn2_gu=w_ffn2_gu, w_ffn2_down=w_ffn2_down)
    w = _prep_weights(p)
    fw = norm_final.reshape(1, D_MODEL)
    bp, lp, _ = x_prompt.shape
    bs, ls, _ = x_sample.shape
    assert ls == SAMPLE_LEN and lp % CORE_ROWS == 0 and lp % PROMPT_TM == 0
    assert bs % SEQ_PER_STEP == 0 and bp % SEQS_PER_STEP == 0
    yp, gla_p, ssm_p, conv_p = _trunk_prompt(x_prompt.reshape(bp * lp, D_MODEL), w, p, fw, bp)
    ys, gla_s, ssm_s, conv_s = _trunk_sample(x_sample.reshape(bs * ls, D_MODEL), state_gla, state_ssm,
                                             state_conv, w, p, fw)
    return (yp.reshape(bp, lp, D_MODEL), ys.reshape(bs, ls, D_MODEL),
            gla_p, ssm_p, conv_p, gla_s, ssm_s, conv_s)
```

```python
import functools

import numpy as np
import jax
import jax.numpy as jnp
from jax import lax
from jax.experimental import pallas as pl
from jax.experimental.pallas import tpu as pltpu

F32 = jnp.float32
BF16 = jnp.bfloat16

D_MODEL = 1024
EPS = 1e-6
FFN_DIM = 2816
GLA_HEADS = 4
GLA_DK = 128
GLA_DV = 256
GLA_DK_TOTAL = GLA_HEADS * GLA_DK
GLA_DV_TOTAL = GLA_HEADS * GLA_DV
GLA_GATE_RANK = 16
GLA_TAU = 16.0
SSM_INNER = 2048
SSM_HEADDIM = 64
SSM_HEADS = 32
SSM_GROUPS = 4
SSM_HPG = SSM_HEADS // SSM_GROUPS
SSM_DSTATE = 128
CONV_W = 4
CONV_DIM = SSM_INNER + 2 * SSM_GROUPS * SSM_DSTATE
CHUNK = 64
IN_SPLITS = (GLA_DK_TOTAL, GLA_DK_TOTAL, GLA_DV_TOTAL, GLA_DV_TOTAL, GLA_GATE_RANK,
             SSM_INNER, CONV_DIM, SSM_HEADS, D_MODEL, D_MODEL)

LANES = 128
BF16_ROWS = 16
FFN_TILE = 256
GROUP_LANES = SSM_HPG * SSM_HEADDIM
PIECES = 3
LOG2E = 1.4426950408889634
VMEM_LIMIT = 56 * 1024 * 1024

NT_DIMS = (((1,), (1,)), ((), ()))
TN_DIMS = (((0,), (0,)), ((), ()))


def _dot(a, b):
    return jnp.dot(a, b, preferred_element_type=F32)


def _dot_nt(a, b):
    return lax.dot_general(a, b, NT_DIMS, preferred_element_type=F32)


def _dot_tn(a, b):
    return lax.dot_general(a, b, TN_DIMS, preferred_element_type=F32)


def _rms(x):
    return x * lax.rsqrt(jnp.mean(x * x, axis=-1, keepdims=True) + EPS)


def _sigmoid(x):
    return 0.5 * jnp.tanh(0.5 * x) + 0.5


def _silu(x):
    h = 0.5 * x
    return h * jnp.tanh(h) + h


def _softplus(x):
    return jnp.maximum(x, 0.0) + jnp.log1p(jnp.exp(-jnp.abs(x)))


def _split3(x):
    hi = x.astype(BF16)
    r1 = x - hi.astype(F32)
    mid = r1.astype(BF16)
    lo = (r1 - mid.astype(F32)).astype(BF16)
    return hi, mid, lo


def _stack3(x):
    return jnp.concatenate(_split3(x), axis=0)


def _pieces_by_lane(x):
    hi, mid, lo = (piece.astype(F32) for piece in _split3(x))
    lane = lax.broadcasted_iota(jnp.int32, x.shape, 1)
    picked = jnp.where(lane < SSM_HEADS, hi, jnp.where(lane < 2 * SSM_HEADS, mid, lo))
    return picked.astype(BF16)


def _pieces_by_row(x, piece_of_row):
    hi, mid, lo = (piece.astype(F32) for piece in _split3(x))
    idx = jnp.broadcast_to(piece_of_row, x.shape)
    picked = jnp.where(idx == 0, hi, jnp.where(idx == 1, mid, jnp.where(idx == 2, lo, 0.0)))
    return picked.astype(BF16)


def _params(n_arbitrary=1):
    return pltpu.CompilerParams(dimension_semantics=("arbitrary",) * n_arbitrary,
                                vmem_limit_bytes=VMEM_LIMIT)


def _const_spec(shape):
    nd = len(shape)
    return pl.BlockSpec(shape, lambda *_: (0,) * nd)


def _layer_spec(l, shape):
    return pl.BlockSpec((None,) + shape, lambda *_: (l, 0, 0), pipeline_mode=pl.Buffered(1))


IN_OFFS = tuple(int(o) for o in np.cumsum((0,) + IN_SPLITS))


def _in_rows_spec(l, first, last):
    start, rows = l * IN_OFFS[-1] + IN_OFFS[first], IN_OFFS[last] - IN_OFFS[first]
    return pl.BlockSpec((pl.Element(rows), pl.Element(D_MODEL)), lambda *_: (start, 0),
                        pipeline_mode=pl.Buffered(1))


def _ffn_kernel(x_ref, nw_ref, wgu_ref, wd_ref, fw_ref, o_ref, acc_ref, *, final):
    x = x_ref[...]
    xn = (_rms(x) * nw_ref[...]).astype(BF16)
    for c in range(FFN_DIM // FFN_TILE):
        lo = c * FFN_TILE
        g = _dot(xn, wgu_ref[:, lo:lo + FFN_TILE])
        u = _dot(xn, wgu_ref[:, FFN_DIM + lo:FFN_DIM + lo + FFN_TILE])
        a = (_silu(g) * u).astype(BF16)
        part = _dot(a, wd_ref[lo:lo + FFN_TILE, :])
        if c == 0:
            acc_ref[...] = part
        else:
            acc_ref[...] += part
    y = x + 0.5 * acc_ref[...]
    if final:
        y = _rms(y) * fw_ref[...]
    o_ref[...] = y


def _ffn(x, nw, wgu, wd, fw, *, l, tm, final):
    t = x.shape[0]
    return pl.pallas_call(
        functools.partial(_ffn_kernel, final=final),
        grid=(t // tm,),
        in_specs=[pl.BlockSpec((tm, D_MODEL), lambda i: (i, 0)),
                  _const_spec((1, D_MODEL)),
                  _layer_spec(l, (D_MODEL, 2 * FFN_DIM)),
                  _layer_spec(l, (FFN_DIM, D_MODEL)),
                  _const_spec((1, D_MODEL))],
        out_specs=pl.BlockSpec((tm, D_MODEL), lambda i: (i, 0)),
        out_shape=jax.ShapeDtypeStruct((t, D_MODEL), F32),
        scratch_shapes=[pltpu.VMEM((tm, D_MODEL), F32)],
        compiler_params=_params(),
        name="ffn_final" if final else "ffn",
    )(x, nw, wgu, wd, fw)


def _gla_in_kernel(x_ref, nw_ref, wqkvg_ref, wgate_ref, wa1_ref, wa2_ref, ba2_ref,
                   q_ref, k_ref, v_ref, sg_ref, ga_ref, gb_ref, la_ref):
    u = (_rms(x_ref[...]) * nw_ref[...]).astype(BF16)
    q_ref[...] = _dot_nt(u, wqkvg_ref[0:512, :]).astype(BF16)
    k_ref[...] = _dot_nt(u, wqkvg_ref[512:1024, :]).astype(BF16)
    v_ref[...] = _dot_nt(u, wqkvg_ref[1024:2048, :]).astype(BF16)
    sg_ref[...] = _silu(_dot_nt(u, wqkvg_ref[2048:3072, :])).astype(BF16)
    ga_ref[...] = _sigmoid(_dot_nt(u, wgate_ref[0:1024, :])).astype(BF16)
    gb_ref[...] = _sigmoid(_dot_nt(u, wgate_ref[1024:2048, :])).astype(BF16)
    a_lr = _dot_nt(u, wa1_ref[...]).astype(BF16)
    z = _dot(a_lr, wa2_ref[...]) + ba2_ref[...]
    la_ref[...] = -_softplus(-z) / GLA_TAU


def _gla_in(x, nw, wqkvg, wgate, wa1, wa2, ba2, *, l, tm):
    t = x.shape[0]
    row = lambda width: pl.BlockSpec((tm, width), lambda i: (i, 0))
    out = lambda width, dt: jax.ShapeDtypeStruct((t, width), dt)
    return pl.pallas_call(
        _gla_in_kernel,
        grid=(t // tm,),
        in_specs=[row(D_MODEL), _const_spec((1, D_MODEL)),
                  _in_rows_spec(l, 0, 4), _in_rows_spec(l, 8, 10),
                  _layer_spec(l, (LANES, D_MODEL)), _layer_spec(l, (LANES, GLA_DK_TOTAL)),
                  _const_spec((1, GLA_DK_TOTAL))],
        out_specs=[row(512), row(512), row(1024), row(1024), row(1024), row(1024), row(512)],
        out_shape=[out(512, BF16), out(512, BF16), out(1024, BF16), out(1024, BF16),
                   out(1024, BF16), out(1024, BF16), out(512, F32)],
        compiler_params=_params(),
        name="gla_in",
    )(x, nw, wqkvg, wgate, wa1, wa2, ba2)


ZX_COLS = SSM_INNER + CONV_DIM


CONV_COLS = 512
SUBLANES = 8


def _ssm_in_prompt_kernel(x_ref, nw_ref, wzx_ref, wdt_ref, dtb_ref, cw_ref, cb_ref,
                          sz_ref, xa_ref, dt_ref, tail_ref, carry_ref, *, tm):
    j = pl.program_id(1)

    @pl.when(j == 0)
    def _():
        carry_ref[...] = jnp.zeros(carry_ref.shape, F32)

    u = (_rms(x_ref[...]) * nw_ref[...]).astype(BF16)
    sz_ref[...] = _silu(_dot_nt(u, wzx_ref[0:SSM_INNER, :])).astype(BF16)
    dt_ref[...] = _softplus(_dot_nt(u, wdt_ref[...]) + dtb_ref[...])
    tiles = (tm // SUBLANES, SUBLANES, CONV_COLS)
    sub = lax.broadcasted_iota(jnp.int32, tiles, 1)
    for cc in range(CONV_DIM // CONV_COLS):
        cols = slice(cc * CONV_COLS, (cc + 1) * CONV_COLS)
        xr = _dot_nt(u, wzx_ref[SSM_INNER + cc * CONV_COLS:SSM_INNER + (cc + 1) * CONV_COLS, :])
        x3 = xr.reshape(tiles)
        p3 = jnp.concatenate([carry_ref[:, cols], xr[0:tm - SUBLANES, :]], axis=0).reshape(tiles)
        acc = cb_ref[:, cols] + xr * cw_ref[CONV_W - 1:CONV_W, cols]
        for i in range(CONV_W - 1):
            back = CONV_W - 1 - i
            sh = pltpu.roll(jnp.where(sub >= SUBLANES - back, p3, x3), back, axis=1)
            acc = acc + sh.reshape(tm, CONV_COLS) * cw_ref[i:i + 1, cols]
        xa_ref[:, cols] = _silu(acc).astype(BF16)
        carry_ref[:, cols] = xr[tm - SUBLANES:tm, :]
    tail_ref[0] = carry_ref[SUBLANES - (CONV_W - 1):SUBLANES, :]


def _ssm_in_prompt(x, nw, wzx, wdt, dtb, cw, cb, *, l, nb, tm):
    t = x.shape[0]
    nj = t // nb // tm
    row = lambda width: pl.BlockSpec((tm, width), lambda b, j: (b * nj + j, 0))
    return pl.pallas_call(
        functools.partial(_ssm_in_prompt_kernel, tm=tm),
        grid=(nb, nj),
        in_specs=[row(D_MODEL), _const_spec((1, D_MODEL)),
                  _in_rows_spec(l, 5, 7), _layer_spec(l, (LANES, D_MODEL)),
                  _const_spec((1, LANES)), _const_spec((CONV_W, CONV_DIM)), _const_spec((1, CONV_DIM))],
        out_specs=[row(SSM_INNER), row(CONV_DIM), row(LANES),
                   pl.BlockSpec((1, CONV_W - 1, CONV_DIM), lambda b, j: (b, 0, 0))],
        out_shape=[jax.ShapeDtypeStruct((t, SSM_INNER), BF16),
                   jax.ShapeDtypeStruct((t, CONV_DIM), BF16),
                   jax.ShapeDtypeStruct((t, LANES), F32),
                   jax.ShapeDtypeStruct((nb, CONV_W - 1, CONV_DIM), F32)],
        scratch_shapes=[pltpu.VMEM((SUBLANES, CONV_DIM), F32)],
        compiler_params=_params(n_arbitrary=2),
        name="ssm_in_prompt",
    )(x, nw, wzx, wdt, dtb, cw, cb)


def _ssm_in_sample_kernel(x_ref, nw_ref, wzx_ref, wdt_ref, dtb_ref, sz_ref, xr_ref, dt_ref):
    u = (_rms(x_ref[...]) * nw_ref[...]).astype(BF16)
    sz_ref[...] = _silu(_dot_nt(u, wzx_ref[0:SSM_INNER, :])).astype(BF16)
    xr_ref[...] = _dot_nt(u, wzx_ref[SSM_INNER:ZX_COLS, :])
    dt_ref[...] = _softplus(_dot_nt(u, wdt_ref[...]) + dtb_ref[...])


def _ssm_in_sample(x, nw, wzx, wdt, dtb, *, l, tm):
    t = x.shape[0]
    row = lambda width: pl.BlockSpec((tm, width), lambda i: (i, 0))
    return pl.pallas_call(
        _ssm_in_sample_kernel,
        grid=(t // tm,),
        in_specs=[row(D_MODEL), _const_spec((1, D_MODEL)),
                  _in_rows_spec(l, 5, 7), _layer_spec(l, (LANES, D_MODEL)),
                  _const_spec((1, LANES))],
        out_specs=[row(SSM_INNER), row(CONV_DIM), row(LANES)],
        out_shape=[jax.ShapeDtypeStruct((t, SSM_INNER), BF16),
                   jax.ShapeDtypeStruct((t, CONV_DIM), F32),
                   jax.ShapeDtypeStruct((t, LANES), F32)],
        compiler_params=_params(),
        name="ssm_in_sample",
    )(x, nw, wzx, wdt, dtb)


def _gla_rows(q, k, bcum, blast):
    qf = q.astype(F32) * (GLA_DK ** -0.5)
    kf = k.astype(F32)
    q_in = (qf * jnp.exp2(bcum)).astype(BF16)
    k_in = (kf * jnp.exp2(-bcum)).astype(BF16)
    k_out = (kf * jnp.exp2(blast - bcum)).astype(BF16)
    return q_in, k_in, k_out


def _segment_weights(ac_s, itile, cmask):
    arow = jnp.sum(ac_s * itile, axis=0, keepdims=True)
    return jnp.where(cmask > 0.5, jnp.exp2(ac_s - arow), 0.0)


PAIR = 2


SEQS_PER_STEP = 1


def _core_prompt_kernel(q_ref, k_ref, v_ref, la_ref, xa_ref, dt_ref, alog3_ref, dskip_ref,
                        btril3_ref, e3_ref, itile_ref, cmask_ref, bmask_ref,
                        o_ref, y_ref, gla_ref, ssm_ref, s_ref, ht_ref, *, rows, nj):
    j = pl.program_id(1)

    @pl.when(j == 0)
    def _():
        s_ref[...] = jnp.zeros(s_ref.shape, F32)
        ht_ref[...] = jnp.zeros(ht_ref.shape, F32)

    btril3 = btril3_ref[...]
    causal = btril3[0:CHUNK, 0:CHUNK] > 0.5
    ones16 = jnp.ones((BF16_ROWS, LANES), BF16)
    row16 = lax.broadcasted_iota(jnp.int32, (BF16_ROWS, 1), 0)
    itile = itile_ref[...]
    cmask = cmask_ref[...]
    bmask = bmask_ref[...]
    e3 = e3_ref[...]
    a3 = -jnp.exp(alog3_ref[...]) * LOG2E

    bcum_all, dtx_all, acx_all = [], [], []
    for sq in range(SEQS_PER_STEP):
        bcum_all.append(_dot(btril3, _stack3(la_ref[sq] * LOG2E)))
        dt3 = dt_ref[sq]
        acum3 = _dot(btril3, _stack3(dt3 * a3))
        dtx_all.append(_dot(_pieces_by_lane(dt3), e3))
        acx_all.append(_dot(_pieces_by_lane(acum3), e3))

    chunks = [slice(c * CHUNK, (c + 1) * CHUNK) for c in range(rows // CHUNK)]
    kls = [slice(h * GLA_DK, (h + 1) * GLA_DK) for h in range(GLA_HEADS)]
    vls = [slice(h * GLA_DV, (h + 1) * GLA_DV) for h in range(GLA_HEADS)]
    gls = [slice(g * GROUP_LANES, (g + 1) * GROUP_LANES) for g in range(SSM_GROUPS)]

    for sq in range(SEQS_PER_STEP):
        prep = []
        for r in chunks:
            bcum = bcum_all[sq][r, :]
            blast = bcum[CHUNK - 1:CHUNK, :]
            q_in, k_in, k_out = _gla_rows(q_ref[sq, r, :], k_ref[sq, r, :], bcum, blast)
            bl_pieces = _pieces_by_row(jnp.broadcast_to(blast, (BF16_ROWS, GLA_DK_TOTAL)), row16)
            prep.append((q_in, k_in, k_out, bl_pieces, v_ref[sq, r, :]))
        pairs = [(c, h) for h in range(GLA_HEADS) for c in range(len(chunks))]
        qk = {(c, h): _dot_nt(prep[c][0][:, kls[h]], prep[c][1][:, kls[h]]) for c, h in pairs}
        sc = {ch: jnp.where(causal, qk[ch], 0.0).astype(BF16) for ch in pairs}
        intra = {}
        for c, h in pairs:
            _, _, k_out, bl_pieces, v = prep[c]
            dec = jnp.exp2(_dot_tn(bl_pieces[:, kls[h]], ones16))
            intra[c, h] = (_dot(sc[c, h], v[:, vls[h]]), _dot_tn(k_out[:, kls[h]], v[:, vls[h]]), dec)
        for c, r in enumerate(chunks):
            for h, (kl, vl) in enumerate(zip(kls, vls)):
                o_intra, ds, dec = intra[c, h]
                s_old = s_ref[sq, h]
                o_ref[sq, r, vl] = (o_intra + _dot(prep[c][0][:, kl], s_old.astype(BF16))).astype(BF16)
                s_ref[sq, h] = s_old * jnp.concatenate([dec, dec], axis=1) + ds
        prep = []
        for r in chunks:
            acx = acx_all[sq][r, :]
            alast = acx[CHUNK - 1:CHUNK, :]
            xa = xa_ref[sq, r, :]
            xs = xa[:, 0:SSM_INNER].astype(F32)
            xg = xs * dtx_all[sq][r, :]
            xw = (xg * jnp.exp2(alast - acx)).astype(BF16)
            seg = _segment_weights(acx, itile, cmask)
            prep.append((xa, xs, xg.astype(BF16), xw, jnp.exp2(acx), seg, jnp.exp2(alast)))
        pairs = [(c, g) for g in range(SSM_GROUPS) for c in range(len(chunks))]
        b_of = lambda c, g: prep[c][0][:, SSM_INNER + g * SSM_DSTATE:SSM_INNER + (g + 1) * SSM_DSTATE]
        c_of = lambda c, g: prep[c][0][:, SSM_INNER + (SSM_GROUPS + g) * SSM_DSTATE:
                                       SSM_INNER + (SSM_GROUPS + g + 1) * SSM_DSTATE]
        cbx = {(c, g): _dot_nt(c_of(c, g), jnp.concatenate([b_of(c, g)] * SSM_HPG, axis=0)) for c, g in pairs}
        w = {(c, g): (cbx[c, g] * prep[c][5][:, gls[g]]).astype(BF16) for c, g in pairs}
        intra = {}
        for c, g in pairs:
            xs, xgb, xw = prep[c][1], prep[c][2], prep[c][3]
            yd = []
            for pr in range(SSM_HPG // PAIR):
                lo = g * GROUP_LANES + pr * LANES
                xblk = jnp.concatenate([xgb[:, lo:lo + LANES]] * PAIR, axis=0) * bmask
                yd.append(_dot(w[c, g][:, pr * LANES:(pr + 1) * LANES], xblk))
            y_part = jnp.concatenate(yd, axis=1) + dskip_ref[:, gls[g]] * xs[:, gls[g]]
            intra[c, g] = (c_of(c, g), y_part, _dot_tn(b_of(c, g), xw[:, gls[g]]))
        for c, r in enumerate(chunks):
            eac, dec_row = prep[c][4], prep[c][6]
            for g, gl in enumerate(gls):
                cg, y_part, ds_t = intra[c, g]
                h_old = ht_ref[sq, :, gl]
                y_ref[sq, r, gl] = (y_part + _dot(cg, h_old.astype(BF16)) * eac[:, gl]).astype(BF16)
                ht_ref[sq, :, gl] = h_old * dec_row[:, gl] + ds_t

    @pl.when(j == nj - 1)
    def _():
        for sq in range(SEQS_PER_STEP):
            gla_ref[sq] = s_ref[sq]
            ssm_ref[sq] = ht_ref[sq].T


def _core_prompt(q, k, v, la, xa, dt, alog3, dskip_x, consts, *, nb, rows):
    t = q.shape[0]
    nq = SEQS_PER_STEP
    half, length = nb // nq, t // nb
    nj = length // rows
    split = lambda a: a.reshape(nq, half, length, a.shape[-1])
    row = lambda width: pl.BlockSpec((nq, None, rows, width), lambda b, j: (0, b, j, 0))
    o, y, s_fin, h_fin = pl.pallas_call(
        functools.partial(_core_prompt_kernel, rows=rows, nj=nj),
        grid=(half, nj),
        in_specs=[row(512), row(512), row(1024), row(512), row(CONV_DIM), row(LANES),
                  _const_spec((1, LANES)), _const_spec((1, SSM_INNER))]
                 + [_const_spec(c.shape) for c in consts],
        out_specs=[row(GLA_DV_TOTAL), row(SSM_INNER),
                   pl.BlockSpec((nq, None, GLA_HEADS, GLA_DK, GLA_DV), lambda b, j: (0, b, 0, 0, 0)),
                   pl.BlockSpec((nq, None, SSM_INNER, SSM_DSTATE), lambda b, j: (0, b, 0, 0))],
        out_shape=[jax.ShapeDtypeStruct((nq, half, length, GLA_DV_TOTAL), BF16),
                   jax.ShapeDtypeStruct((nq, half, length, SSM_INNER), BF16),
                   jax.ShapeDtypeStruct((nq, half, GLA_HEADS, GLA_DK, GLA_DV), F32),
                   jax.ShapeDtypeStruct((nq, half, SSM_INNER, SSM_DSTATE), F32)],
        scratch_shapes=[pltpu.VMEM((nq, GLA_HEADS, GLA_DK, GLA_DV), F32),
                        pltpu.VMEM((nq, SSM_DSTATE, SSM_INNER), F32)],
        compiler_params=_params(n_arbitrary=2),
        name="core_prompt",
    )(split(q), split(k), split(v), split(la), split(xa), split(dt), alog3, dskip_x, *consts)
    return (o.reshape(t, GLA_DV_TOTAL), y.reshape(t, SSM_INNER),
            s_fin.reshape(nb, GLA_HEADS, GLA_DK, GLA_DV), h_fin.reshape(nb, SSM_INNER, SSM_DSTATE))


SEQ_PER_STEP = 4
SAMPLE_LEN = 4
SAMPLE_ROWS = SEQ_PER_STEP * SAMPLE_LEN


def _core_sample_kernel(q_ref, k_ref, v_ref, la_ref, xr_ref, c4_ref, dt_ref, s0_ref, h0_ref,
                        cw_ref, cb_ref, alog3_ref, dskip_ref,
                        bs3_ref, shmat_ref, e3_ref, es3_ref, itile_ref, cmask_ref, bmask_ref,
                        qmask_ref, smask_ref, pos_ref, *rest):
    o_ref, y_ref, s1_ref, h1_ref = rest[-4:]
    rs = SAMPLE_ROWS
    bs3 = bs3_ref[...]
    same_causal = bs3[0:rs, 0:rs] > 0.5
    ones16 = jnp.ones((BF16_ROWS, LANES), BF16)
    pos = pos_ref[...]
    xr = xr_ref[...]
    shifted = _dot(shmat_ref[...], jnp.concatenate(_split3(xr) + _split3(c4_ref[...]), axis=0))
    acc = cb_ref[...] + xr * cw_ref[CONV_W - 1:CONV_W, :]
    for i in range(CONV_W - 1):
        back = CONV_W - 1 - i
        acc = acc + shifted[(back - 1) * rs:back * rs, :] * cw_ref[i:i + 1, :]
    xa = _silu(acc)
    sums = _dot(bs3, _stack3(la_ref[...] * LOG2E))
    bcum, blast = sums[0:rs, :], sums[rs:2 * rs, :]
    q_in, k_in, k_out = _gla_rows(q_ref[...], k_ref[...], bcum, blast)
    bl_pieces = _pieces_by_row(blast, pos)
    v = v_ref[...]
    qmask = qmask_ref[...]
    seq_blk = lambda a: jnp.concatenate([a] * SEQ_PER_STEP, axis=1)
    for h in range(GLA_HEADS):
        kl = slice(h * GLA_DK, (h + 1) * GLA_DK)
        vl = slice(h * GLA_DV, (h + 1) * GLA_DV)
        sc = jnp.where(same_causal, _dot_nt(q_in[:, kl], k_in[:, kl]), 0.0).astype(BF16)
        s_old = s0_ref[:, h].reshape(SEQ_PER_STEP * GLA_DK, GLA_DV)
        o_h = _dot(sc, v[:, vl]) + _dot(seq_blk(q_in[:, kl]) * qmask, s_old.astype(BF16))
        o_ref[:, vl] = o_h.astype(BF16)
        ds = _dot_tn(seq_blk(k_out[:, kl]) * qmask, v[:, vl])
        dec = jnp.exp2(_dot_tn(seq_blk(bl_pieces[:, kl]) * qmask, ones16))
        s_new = s_old * jnp.concatenate([dec, dec], axis=1) + ds
        s1_ref[:, h] = s_new.reshape(SEQ_PER_STEP, GLA_DK, GLA_DV)
    dt3 = dt_ref[...]
    sums = _dot(bs3, _stack3(dt3 * (-jnp.exp(alog3_ref[...]) * LOG2E)))
    acum3, alast3 = sums[0:rs, :], sums[rs:2 * rs, :]
    p_ac = _pieces_by_lane(acum3)
    wide = _dot(jnp.concatenate([_pieces_by_lane(dt3), p_ac, _pieces_by_lane(alast3)], axis=0), e3_ref[...])
    dtx, acx, alx = wide[0:rs, :], wide[rs:2 * rs, :], wide[2 * rs:3 * rs, :]
    xs = xa[:, 0:SSM_INNER]
    xg = xs * dtx
    xgb = xg.astype(BF16)
    xw = xg * jnp.exp2(alx - acx)
    eac = jnp.exp2(acx)
    seg = _segment_weights(_dot(p_ac, es3_ref[...]), itile_ref[...], cmask_ref[...])
    al_pieces = _pieces_by_row(alx, pos)
    smask = smask_ref[...]
    sl = SSM_HPG * rs
    for g in range(SSM_GROUPS):
        gl = slice(g * GROUP_LANES, (g + 1) * GROUP_LANES)
        bg = xa[:, SSM_INNER + g * SSM_DSTATE:SSM_INNER + (g + 1) * SSM_DSTATE].astype(BF16)
        cg = xa[:, SSM_INNER + (SSM_GROUPS + g) * SSM_DSTATE:
                SSM_INNER + (SSM_GROUPS + g + 1) * SSM_DSTATE].astype(BF16)
        cbx = _dot_nt(cg, jnp.concatenate([bg] * SSM_HPG, axis=0))
        w = (cbx * seg[:, g * sl:(g + 1) * sl]).astype(BF16)
        xblk = jnp.concatenate([xgb[:, gl]] * SSM_HPG, axis=0) * bmask_ref[...]
        yd = _dot(w, xblk)
        h_old = [h0_ref[s, gl, :] for s in range(SEQ_PER_STEP)]
        yo = _dot_nt(seq_blk(cg) * qmask, jnp.concatenate([hs.astype(BF16) for hs in h_old], axis=1))
        y = yd + yo * eac[:, gl] + dskip_ref[:, gl] * xs[:, gl]
        y_ref[:, gl] = y.astype(BF16)
        ds = _dot_tn(seq_blk(xw[:, gl]).astype(BF16) * smask, bg)
        dec = jnp.exp2(_dot_tn(seq_blk(al_pieces[:, gl]) * smask, ones16))
        for s in range(SEQ_PER_STEP):
            sr = slice(s * GROUP_LANES, (s + 1) * GROUP_LANES)
            h1_ref[s, gl, :] = h_old[s] * dec[sr, :] + ds[sr, :]


def _core_sample(q, k, v, la, xr, c4, dt, s_all, h_all, cw, cb, alog3, dskip_x, consts, *, l, new_states):
    t = q.shape[0]
    rs = SAMPLE_ROWS
    nseq = t // SAMPLE_LEN
    row = lambda width: pl.BlockSpec((rs, width), lambda i: (i, 0))
    s_spec = pl.BlockSpec((None, SEQ_PER_STEP, GLA_HEADS, GLA_DK, GLA_DV), lambda i: (l, i, 0, 0, 0))
    h_spec = pl.BlockSpec((None, SEQ_PER_STEP, SSM_INNER, SSM_DSTATE), lambda i: (l, i, 0, 0))
    inputs = [q, k, v, la, xr, c4, dt, s_all, h_all, cw, cb, alog3, dskip_x, *consts]
    in_specs = ([row(512), row(512), row(1024), row(512), row(CONV_DIM), row(CONV_DIM), row(LANES),
                 s_spec, h_spec,
                 _const_spec((CONV_W, CONV_DIM)), _const_spec((1, CONV_DIM)),
                 _const_spec((1, LANES)), _const_spec((1, SSM_INNER))]
                + [_const_spec(c.shape) for c in consts])
    aliases = {}
    if new_states is not None:
        aliases = {len(inputs): 2, len(inputs) + 1: 3}
        inputs += list(new_states)
        in_specs += [pl.BlockSpec(memory_space=pl.ANY)] * 2
    return pl.pallas_call(
        _core_sample_kernel,
        grid=(nseq // SEQ_PER_STEP,),
        in_specs=in_specs,
        out_specs=[row(GLA_DV_TOTAL), row(SSM_INNER), s_spec, h_spec],
        out_shape=[jax.ShapeDtypeStruct((t, GLA_DV_TOTAL), BF16),
                   jax.ShapeDtypeStruct((t, SSM_INNER), BF16),
                   jax.ShapeDtypeStruct(s_all.shape, F32),
                   jax.ShapeDtypeStruct(h_all.shape, F32)],
        input_output_aliases=aliases,
        compiler_params=_params(),
        name="core_sample",
    )(*inputs)


def _merge_kernel(x_ref, o_ref, sg_ref, y_ref, sz_ref, ga_ref, gb_ref,
                  gn_ref, sn_ref, wpg_ref, wps_ref, wo_ref, out_ref):
    o = o_ref[...].astype(F32)
    parts = []
    for h in range(GLA_HEADS):
        vl = slice(h * GLA_DV, (h + 1) * GLA_DV)
        parts.append(_rms(o[:, vl]) * gn_ref[:, vl])
    og = (jnp.concatenate(parts, axis=1) * sg_ref[...].astype(F32)).astype(BF16)
    yz = y_ref[...].astype(F32) * sz_ref[...].astype(F32)
    yn = (_rms(yz) * sn_ref[...]).astype(BF16)
    m = (ga_ref[...].astype(F32) * _dot(og, wpg_ref[...])
         + gb_ref[...].astype(F32) * _dot(yn, wps_ref[...]))
    out_ref[...] = x_ref[...] + _dot(m.astype(BF16), wo_ref[...])


def _merge(x, o, sg, y, sz, ga, gb, gn, sn, wpg, wps, wo, *, l, tm):
    t = x.shape[0]
    row = lambda width: pl.BlockSpec((tm, width), lambda i: (i, 0))
    return pl.pallas_call(
        _merge_kernel,
        grid=(t // tm,),
        in_specs=[row(D_MODEL), row(1024), row(1024), row(2048), row(2048), row(1024), row(1024),
                  _const_spec((1, GLA_DV_TOTAL)), _const_spec((1, SSM_INNER)),
                  _layer_spec(l, (GLA_DV_TOTAL, D_MODEL)), _layer_spec(l, (SSM_INNER, D_MODEL)),
                  _layer_spec(l, (D_MODEL, D_MODEL))],
        out_specs=row(D_MODEL),
        out_shape=jax.ShapeDtypeStruct((t, D_MODEL), F32),
        compiler_params=_params(),
        name="merge",
    )(x, o, sg, y, sz, ga, gb, gn, sn, wpg, wps, wo)


def _expand_table(width):
    t = np.zeros((LANES, SSM_HEADS * width), np.float32)
    for k in range(PIECES):
        for h in range(SSM_HEADS):
            t[k * SSM_HEADS + h, h * width:(h + 1) * width] = 1.0
    return jnp.asarray(t, dtype=BF16)


def _segment_tables(rows, allowed):
    s_of_lane = np.arange(SSM_HPG * rows) % rows
    itile = (np.arange(rows)[:, None] == s_of_lane[None, :]).astype(np.float32)
    cmask = allowed[:, s_of_lane].astype(np.float32)
    return itile, cmask


def _head_block_mask(heads, rows):
    hr = np.arange(heads * rows) // rows
    hc = np.arange(heads * SSM_HEADDIM) // SSM_HEADDIM
    return jnp.asarray((hr[:, None] == hc[None, :]).astype(np.float32), dtype=BF16)


def _prompt_consts(rows):
    idx = np.arange(rows)
    btril = ((idx[:, None] // CHUNK == idx[None, :] // CHUNK) & (idx[None, :] <= idx[:, None])).astype(np.float32)
    tril = np.tril(np.ones((CHUNK, CHUNK), np.float32))
    itile, cmask = _segment_tables(CHUNK, tril)
    tile4 = lambda a: jnp.asarray(np.tile(a, (1, SSM_GROUPS)))
    return (jnp.asarray(np.tile(btril, (1, PIECES)), dtype=BF16), _expand_table(SSM_HEADDIM),
            tile4(itile), tile4(cmask), _head_block_mask(PAIR, CHUNK))


def _sample_consts():
    rs = SAMPLE_ROWS
    seq = np.arange(rs) // SAMPLE_LEN
    pos = np.arange(rs) % SAMPLE_LEN
    same = (seq[:, None] == seq[None, :]).astype(np.float32)
    btril = same * (pos[None, :] <= pos[:, None])
    bs3 = np.tile(np.concatenate([btril, same], axis=0), (1, PIECES))
    shmat = np.zeros((CONV_W - 1, rs, 2, rs), np.float32)
    for back in range(1, CONV_W):
        for r in range(rs):
            if pos[r] >= back:
                shmat[back - 1, r, 0, r - back] = 1.0
            else:
                shmat[back - 1, r, 1, seq[r] * SAMPLE_LEN + SAMPLE_LEN + pos[r] - back] = 1.0
    shmat = np.repeat(shmat[:, :, :, None, :], PIECES, axis=3).reshape((CONV_W - 1) * rs, 2 * PIECES * rs)
    itile, cmask = _segment_tables(rs, btril)
    tile4 = lambda a: jnp.asarray(np.tile(a, (1, SSM_GROUPS)))
    assert GLA_DK == SSM_DSTATE == LANES
    qmask = (seq[:, None] == (np.arange(SEQ_PER_STEP * LANES) // LANES)[None, :]).astype(np.float32)
    smask = (seq[:, None] == (np.arange(SEQ_PER_STEP * GROUP_LANES) // GROUP_LANES)[None, :]).astype(np.float32)
    return (jnp.asarray(bs3, dtype=BF16), jnp.asarray(shmat, dtype=BF16),
            _expand_table(SSM_HEADDIM), _expand_table(rs), tile4(itile), tile4(cmask),
            _head_block_mask(SSM_HPG, rs), jnp.asarray(qmask, dtype=BF16), jnp.asarray(smask, dtype=BF16),
            jnp.asarray(pos.reshape(rs, 1).astype(np.int32)))


def _prep_weights(p):
    win = jnp.swapaxes(p["w_in"], 1, 2).astype(BF16)
    piece = lambda i, j: win[:, IN_OFFS[i]:IN_OFFS[j], :]
    lane_pad = lambda w: jnp.pad(w, ((0, 0), (0, 0), (0, LANES - w.shape[-1])))
    row_pad = lambda w: jnp.pad(w, ((0, 0), (0, LANES - w.shape[1]), (0, 0)))
    rep3 = lambda a: lane_pad(jnp.concatenate([a] * PIECES, axis=-1))
    return dict(
        win=win.reshape(-1, D_MODEL), wa1=row_pad(piece(4, 5)),
        wdt3=row_pad(jnp.concatenate([piece(7, 8)] * PIECES, axis=1)),
        wa2=jnp.pad(p["w_gla_a2"], ((0, 0), (0, LANES - GLA_GATE_RANK), (0, 0))).astype(BF16),
        dtb3=rep3(p["dt_bias"][:, None, :]), alog3=rep3(p["a_log"][:, None, :]),
        dskip_x=jnp.repeat(p["d_skip"], SSM_HEADDIM, axis=-1)[:, None, :],
        wpg=p["w_proj_gla"].astype(BF16), wps=p["w_proj_ssm"].astype(BF16), wo=p["w_out"].astype(BF16),
        gu1=p["w_ffn1_gu"].astype(BF16), d1=p["w_ffn1_down"].astype(BF16),
        gu2=p["w_ffn2_gu"].astype(BF16), d2=p["w_ffn2_down"].astype(BF16),
    )


PROMPT_TM = 512
IN_PROJ_TM = 1024
CORE_ROWS = 256


def _trunk(x, w, p, fw, *, tm, mixer):
    depth = p["w_in"].shape[0]
    row = lambda name, l: p[name][l].reshape(1, -1)
    for l in range(depth):
        x = _ffn(x, row("norm_ffn1", l), w["gu1"], w["d1"], fw, l=l, tm=tm, final=False)
        o, sg, y, sz, ga, gb = mixer(l, x)
        x = _merge(x, o, sg, y, sz, ga, gb, row("gla_norm", l), row("ssm_norm", l),
                   w["wpg"], w["wps"], w["wo"], l=l, tm=tm)
        x = _ffn(x, row("norm_ffn2", l), w["gu2"], w["d2"], fw, l=l, tm=tm, final=l == depth - 1)
    return x


def _trunk_prompt(x, w, p, fw, nb):
    consts = _prompt_consts(CORE_ROWS)
    row = lambda name, l: p[name][l].reshape(1, -1)
    glas, ssms, convs = [], [], []

    def mixer(l, x):
        q, k, v, sg, ga, gb, la = _gla_in(x, row("norm_mix", l), w["win"], w["win"], w["wa1"], w["wa2"],
                                          row("b_gla_a2", l), l=l, tm=IN_PROJ_TM)
        sz, xa, dt, tail = _ssm_in_prompt(x, row("norm_mix", l), w["win"], w["wdt3"], w["dtb3"][l],
                                          p["conv_w"][l], row("conv_b", l), l=l, nb=nb, tm=IN_PROJ_TM)
        o, y, s_fin, h_fin = _core_prompt(q, k, v, la, xa, dt, w["alog3"][l], w["dskip_x"][l], consts,
                                          nb=nb, rows=CORE_ROWS)
        glas.append(s_fin)
        ssms.append(h_fin.reshape(nb, SSM_HEADS, SSM_HEADDIM, SSM_DSTATE))
        convs.append(tail)
        return o, sg, y, sz, ga, gb

    y = _trunk(x, w, p, fw, tm=PROMPT_TM, mixer=mixer)
    return y, jnp.stack(glas), jnp.stack(ssms), jnp.stack(convs)


def _trunk_sample(x, state_gla, state_ssm, state_conv, w, p, fw):
    consts = _sample_consts()
    t = x.shape[0]
    nseq = t // SAMPLE_LEN
    depth = state_gla.shape[0]
    row = lambda name, l: p[name][l].reshape(1, -1)
    h_all = state_ssm.reshape(depth, nseq, SSM_INNER, SSM_DSTATE)
    new_states = [None]
    convs = []

    def mixer(l, x):
        q, k, v, sg, ga, gb, la = _gla_in(x, row("norm_mix", l), w["win"], w["win"], w["wa1"], w["wa2"],
                                          row("b_gla_a2", l), l=l, tm=t)
        sz, xr, dt = _ssm_in_sample(x, row("norm_mix", l), w["win"], w["wdt3"], w["dtb3"][l], l=l, tm=t)
        c4 = jnp.pad(state_conv[l], ((0, 0), (SAMPLE_LEN - (CONV_W - 1), 0), (0, 0))).reshape(t, CONV_DIM)
        o, y, s1, h1 = _core_sample(q, k, v, la, xr, c4, dt, state_gla, h_all, p["conv_w"][l],
                                    row("conv_b", l), w["alog3"][l], w["dskip_x"][l], consts,
                                    l=l, new_states=new_states[0])
        new_states[0] = (s1, h1)
        convs.append(xr.reshape(nseq, SAMPLE_LEN, CONV_DIM)[:, SAMPLE_LEN - (CONV_W - 1):])
        return o, sg, y, sz, ga, gb

    y = _trunk(x, w, p, fw, tm=t, mixer=mixer)
    s1, h1 = new_states[0]
    return y, s1, h1.reshape(state_ssm.shape), jnp.stack(convs)


def kernel(x_prompt, x_sample, state_gla, state_ssm, state_conv, norm_ffn1, w_ffn1_gu, w_ffn1_down, norm_mix, w_in, w_gla_a2, b_gla_a2, gla_norm, conv_w, conv_b, dt_bias, a_log, d_skip, ssm_norm, w_proj_gla, w_proj_ssm, w_out, norm_ffn2, w_ffn2_gu, w_ffn2_down, norm_final):
    p = dict(norm_ffn1=norm_ffn1, w_ffn1_gu=w_ffn1_gu, w_ffn1_down=w_ffn1_down, norm_mix=norm_mix,
             w_in=w_in, w_gla_a2=w_gla_a2, b_gla_a2=b_gla_a2, gla_norm=gla_norm, conv_w=conv_w,
             conv_b=conv_b, dt_bias=dt_bias, a_log=a_log, d_skip=d_skip, ssm_norm=ssm_norm,
             w_proj_gla=w_proj_gla, w_proj_ssm=w_proj_ssm, w_out=w_out, norm_ffn2=norm_ffn2,
             w_ffn2_gu=w_ffn2_gu, w_ffn2_down=w_ffn2_down)
    w = _prep_weights(p)
    fw = norm_final.reshape(1, D_MODEL)
    bp, lp, _ = x_prompt.shape
    bs, ls, _ = x_sample.shape
    assert ls == SAMPLE_LEN and lp % CORE_ROWS == 0 and lp % PROMPT_TM == 0 and lp % IN_PROJ_TM == 0
    assert bs % SEQ_PER_STEP == 0 and bp % SEQS_PER_STEP == 0
    yp, gla_p, ssm_p, conv_p = _trunk_prompt(x_prompt.reshape(bp * lp, D_MODEL), w, p, fw, bp)
    ys, gla_s, ssm_s, conv_s = _trunk_sample(x_sample.reshape(bs * ls, D_MODEL), state_gla, state_ssm,
                                             state_conv, w, p, fw)
    return (yp.reshape(bp, lp, D_MODEL), ys.reshape(bs, ls, D_MODEL),
            gla_p, ssm_p, conv_p, gla_s, ssm_s, conv_s)
```

```python
import functools

import numpy as np
import jax
import jax.numpy as jnp
from jax import lax
from jax.experimental import pallas as pl
from jax.experimental.pallas import tpu as pltpu

F32 = jnp.float32
BF16 = jnp.bfloat16

D_MODEL = 1024
EPS = 1e-6
FFN_DIM = 2816
GLA_HEADS = 4
GLA_DK = 128
GLA_DV = 256
GLA_DK_TOTAL = GLA_HEADS * GLA_DK
GLA_DV_TOTAL = GLA_HEADS * GLA_DV
GLA_GATE_RANK = 16
GLA_TAU = 16.0
SSM_INNER = 2048
SSM_HEADDIM = 64
SSM_HEADS = 32
SSM_GROUPS = 4
SSM_HPG = SSM_HEADS // SSM_GROUPS
SSM_DSTATE = 128
CONV_W = 4
CONV_DIM = SSM_INNER + 2 * SSM_GROUPS * SSM_DSTATE
CHUNK = 64
IN_SPLITS = (GLA_DK_TOTAL, GLA_DK_TOTAL, GLA_DV_TOTAL, GLA_DV_TOTAL, GLA_GATE_RANK,
             SSM_INNER, CONV_DIM, SSM_HEADS, D_MODEL, D_MODEL)

LANES = 128
BF16_ROWS = 16
FFN_TILE = 256
GROUP_LANES = SSM_HPG * SSM_HEADDIM
PIECES = 3
LOG2E = 1.4426950408889634
VMEM_LIMIT = 56 * 1024 * 1024

NT_DIMS = (((1,), (1,)), ((), ()))
TN_DIMS = (((0,), (0,)), ((), ()))


def _dot(a, b):
    return jnp.dot(a, b, preferred_element_type=F32)


def _dot_nt(a, b):
    return lax.dot_general(a, b, NT_DIMS, preferred_element_type=F32)


def _dot_tn(a, b):
    return lax.dot_general(a, b, TN_DIMS, preferred_element_type=F32)


def _rms(x):
    return x * lax.rsqrt(jnp.mean(x * x, axis=-1, keepdims=True) + EPS)


def _sigmoid(x):
    return 0.5 * jnp.tanh(0.5 * x) + 0.5


def _silu(x):
    h = 0.5 * x
    return h * jnp.tanh(h) + h


def _softplus(x):
    return jnp.maximum(x, 0.0) + jnp.log1p(jnp.exp(-jnp.abs(x)))


def _split3(x):
    hi = x.astype(BF16)
    r1 = x - hi.astype(F32)
    mid = r1.astype(BF16)
    lo = (r1 - mid.astype(F32)).astype(BF16)
    return hi, mid, lo


def _stack3(x):
    return jnp.concatenate(_split3(x), axis=0)


def _pieces_by_lane(x):
    hi, mid, lo = (piece.astype(F32) for piece in _split3(x))
    lane = lax.broadcasted_iota(jnp.int32, x.shape, 1)
    picked = jnp.where(lane < SSM_HEADS, hi, jnp.where(lane < 2 * SSM_HEADS, mid, lo))
    return picked.astype(BF16)


def _pieces_by_row(x, piece_of_row):
    hi, mid, lo = (piece.astype(F32) for piece in _split3(x))
    idx = jnp.broadcast_to(piece_of_row, x.shape)
    picked = jnp.where(idx == 0, hi, jnp.where(idx == 1, mid, jnp.where(idx == 2, lo, 0.0)))
    return picked.astype(BF16)


def _params(n_arbitrary=1):
    return pltpu.CompilerParams(dimension_semantics=("arbitrary",) * n_arbitrary,
                                vmem_limit_bytes=VMEM_LIMIT)


def _const_spec(shape):
    nd = len(shape)
    return pl.BlockSpec(shape, lambda *_: (0,) * nd)


def _layer_spec(l, shape):
    return pl.BlockSpec((None,) + shape, lambda *_: (l, 0, 0), pipeline_mode=pl.Buffered(1))


IN_OFFS = tuple(int(o) for o in np.cumsum((0,) + IN_SPLITS))


def _in_rows_spec(l, first, last):
    start, rows = l * IN_OFFS[-1] + IN_OFFS[first], IN_OFFS[last] - IN_OFFS[first]
    return pl.BlockSpec((pl.Element(rows), pl.Element(D_MODEL)), lambda *_: (start, 0),
                        pipeline_mode=pl.Buffered(1))


def _two_group_call(body, name, rows_p, rows_s, shared, shared_specs, outs, *, tm, scratch_shapes=()):
    tp, ts = rows_p[0].shape[0], rows_s[0].shape[0]
    n_p = tp // tm
    assert tp % tm == 0 and len(rows_p) == len(rows_s)
    p_spec = lambda width: pl.BlockSpec((tm, width), lambda i: (jnp.minimum(i, n_p - 1), 0))
    s_spec = lambda width: pl.BlockSpec((ts, width), lambda i: (0, 0))
    n_in, n_sh, n_out = len(rows_p), len(shared), len(outs)

    def kernel(*refs):
        in_p, in_s = refs[:n_in], refs[n_in:2 * n_in]
        sh = refs[2 * n_in:2 * n_in + n_sh]
        out_p = refs[2 * n_in + n_sh:2 * n_in + n_sh + n_out]
        out_s = refs[2 * n_in + n_sh + n_out:2 * n_in + n_sh + 2 * n_out]
        scratch = refs[2 * n_in + n_sh + 2 * n_out:]
        i = pl.program_id(0)

        @pl.when(i < n_p)
        def _():
            body(in_p, sh, out_p, scratch)

        @pl.when(i == n_p)
        def _():
            body(in_s, sh, out_s, scratch)

    res = pl.pallas_call(
        kernel,
        grid=(n_p + 1,),
        in_specs=[p_spec(a.shape[1]) for a in rows_p] + [s_spec(a.shape[1]) for a in rows_s] + list(shared_specs),
        out_specs=[p_spec(width) for width, _ in outs] + [s_spec(width) for width, _ in outs],
        out_shape=[jax.ShapeDtypeStruct((tp, width), dt) for width, dt in outs]
                  + [jax.ShapeDtypeStruct((ts, width), dt) for width, dt in outs],
        scratch_shapes=list(scratch_shapes),
        compiler_params=_params(),
        name=name,
    )(*rows_p, *rows_s, *shared)
    return res[:n_out], res[n_out:]


def _ffn_kernel(x_ref, nw_ref, wgu_ref, wd_ref, fw_ref, o_ref, acc_ref, *, final):
    x = x_ref[...]
    rows = x.shape[0]
    xn = (_rms(x) * nw_ref[...]).astype(BF16)
    for c in range(FFN_DIM // FFN_TILE):
        lo = c * FFN_TILE
        g = _dot(xn, wgu_ref[:, lo:lo + FFN_TILE])
        u = _dot(xn, wgu_ref[:, FFN_DIM + lo:FFN_DIM + lo + FFN_TILE])
        a = (_silu(g) * u).astype(BF16)
        part = _dot(a, wd_ref[lo:lo + FFN_TILE, :])
        if c == 0:
            acc_ref[0:rows, :] = part
        else:
            acc_ref[0:rows, :] += part
    y = x + 0.5 * acc_ref[0:rows, :]
    if final:
        y = _rms(y) * fw_ref[...]
    o_ref[...] = y


def _ffn(xp, xs, nw, wgu, wd, fw, *, l, tm, final):
    body = lambda rows, sh, outs, scratch: _ffn_kernel(rows[0], *sh, outs[0], scratch[0], final=final)
    (yp,), (ys,) = _two_group_call(
        body, "ffn_final" if final else "ffn", [xp], [xs], [nw, wgu, wd, fw],
        [_const_spec((1, D_MODEL)), _layer_spec(l, (D_MODEL, 2 * FFN_DIM)), _layer_spec(l, (FFN_DIM, D_MODEL)),
         _const_spec((1, D_MODEL))],
        [(D_MODEL, F32)], tm=tm, scratch_shapes=[pltpu.VMEM((max(tm, xs.shape[0]), D_MODEL), F32)])
    return yp, ys


def _gla_in_kernel(x_ref, nw_ref, wqkvg_ref, wgate_ref, wa1_ref, wa2_ref, ba2_ref,
                   q_ref, k_ref, v_ref, sg_ref, ga_ref, gb_ref, la_ref):
    u = (_rms(x_ref[...]) * nw_ref[...]).astype(BF16)
    q_ref[...] = _dot_nt(u, wqkvg_ref[0:512, :]).astype(BF16)
    k_ref[...] = _dot_nt(u, wqkvg_ref[512:1024, :]).astype(BF16)
    v_ref[...] = _dot_nt(u, wqkvg_ref[1024:2048, :]).astype(BF16)
    sg_ref[...] = _silu(_dot_nt(u, wqkvg_ref[2048:3072, :])).astype(BF16)
    ga_ref[...] = _sigmoid(_dot_nt(u, wgate_ref[0:1024, :])).astype(BF16)
    gb_ref[...] = _sigmoid(_dot_nt(u, wgate_ref[1024:2048, :])).astype(BF16)
    a_lr = _dot_nt(u, wa1_ref[...]).astype(BF16)
    z = _dot(a_lr, wa2_ref[...]) + ba2_ref[...]
    la_ref[...] = -_softplus(-z) / GLA_TAU


def _gla_in(xp, xs, nw, win, wa1, wa2, ba2, *, l, tm):
    body = lambda rows, sh, outs, scratch: _gla_in_kernel(rows[0], *sh, *outs)
    return _two_group_call(
        body, "gla_in", [xp], [xs], [nw, win, win, wa1, wa2, ba2],
        [_const_spec((1, D_MODEL)), _in_rows_spec(l, 0, 4), _in_rows_spec(l, 8, 10),
         _layer_spec(l, (LANES, D_MODEL)), _layer_spec(l, (LANES, GLA_DK_TOTAL)), _const_spec((1, GLA_DK_TOTAL))],
        [(512, BF16), (512, BF16), (1024, BF16), (1024, BF16), (1024, BF16), (1024, BF16), (512, F32)], tm=tm)


ZX_COLS = SSM_INNER + CONV_DIM


CONV_COLS = 512
SUBLANES = 8


def _ssm_in_prompt_kernel(x_ref, nw_ref, wzx_ref, wdt_ref, dtb_ref, cw_ref, cb_ref,
                          sz_ref, xa_ref, dt_ref, tail_ref, carry_ref, *, tm):
    j = pl.program_id(1)

    @pl.when(j == 0)
    def _():
        carry_ref[...] = jnp.zeros(carry_ref.shape, F32)

    u = (_rms(x_ref[...]) * nw_ref[...]).astype(BF16)
    sz_ref[...] = _silu(_dot_nt(u, wzx_ref[0:SSM_INNER, :])).astype(BF16)
    dt_ref[...] = _softplus(_dot_nt(u, wdt_ref[...]) + dtb_ref[...])
    tiles = (tm // SUBLANES, SUBLANES, CONV_COLS)
    sub = lax.broadcasted_iota(jnp.int32, tiles, 1)
    for cc in range(CONV_DIM // CONV_COLS):
        cols = slice(cc * CONV_COLS, (cc + 1) * CONV_COLS)
        xr = _dot_nt(u, wzx_ref[SSM_INNER + cc * CONV_COLS:SSM_INNER + (cc + 1) * CONV_COLS, :])
        x3 = xr.reshape(tiles)
        p3 = jnp.concatenate([carry_ref[:, cols], xr[0:tm - SUBLANES, :]], axis=0).reshape(tiles)
        acc = cb_ref[:, cols] + xr * cw_ref[CONV_W - 1:CONV_W, cols]
        for i in range(CONV_W - 1):
            back = CONV_W - 1 - i
            sh = pltpu.roll(jnp.where(sub >= SUBLANES - back, p3, x3), back, axis=1)
            acc = acc + sh.reshape(tm, CONV_COLS) * cw_ref[i:i + 1, cols]
        xa_ref[:, cols] = _silu(acc).astype(BF16)
        carry_ref[:, cols] = xr[tm - SUBLANES:tm, :]
    tail_ref[0] = carry_ref[SUBLANES - (CONV_W - 1):SUBLANES, :]


def _ssm_in_prompt(x, nw, wzx, wdt, dtb, cw, cb, *, l, nb, tm):
    t = x.shape[0]
    nj = t // nb // tm
    row = lambda width: pl.BlockSpec((tm, width), lambda b, j: (b * nj + j, 0))
    return pl.pallas_call(
        functools.partial(_ssm_in_prompt_kernel, tm=tm),
        grid=(nb, nj),
        in_specs=[row(D_MODEL), _const_spec((1, D_MODEL)),
                  _in_rows_spec(l, 5, 7), _layer_spec(l, (LANES, D_MODEL)),
                  _const_spec((1, LANES)), _const_spec((CONV_W, CONV_DIM)), _const_spec((1, CONV_DIM))],
        out_specs=[row(SSM_INNER), row(CONV_DIM), row(LANES),
                   pl.BlockSpec((1, CONV_W - 1, CONV_DIM), lambda b, j: (b, 0, 0))],
        out_shape=[jax.ShapeDtypeStruct((t, SSM_INNER), BF16),
                   jax.ShapeDtypeStruct((t, CONV_DIM), BF16),
                   jax.ShapeDtypeStruct((t, LANES), F32),
                   jax.ShapeDtypeStruct((nb, CONV_W - 1, CONV_DIM), F32)],
        scratch_shapes=[pltpu.VMEM((SUBLANES, CONV_DIM), F32)],
        compiler_params=_params(n_arbitrary=2),
        name="ssm_in_prompt",
    )(x, nw, wzx, wdt, dtb, cw, cb)


def _ssm_in_sample_kernel(x_ref, nw_ref, wzx_ref, wdt_ref, dtb_ref, sz_ref, xr_ref, dt_ref):
    u = (_rms(x_ref[...]) * nw_ref[...]).astype(BF16)
    sz_ref[...] = _silu(_dot_nt(u, wzx_ref[0:SSM_INNER, :])).astype(BF16)
    xr_ref[...] = _dot_nt(u, wzx_ref[SSM_INNER:ZX_COLS, :])
    dt_ref[...] = _softplus(_dot_nt(u, wdt_ref[...]) + dtb_ref[...])


def _ssm_in_sample(x, nw, wzx, wdt, dtb, *, l, tm):
    t = x.shape[0]
    row = lambda width: pl.BlockSpec((tm, width), lambda i: (i, 0))
    return pl.pallas_call(
        _ssm_in_sample_kernel,
        grid=(t // tm,),
        in_specs=[row(D_MODEL), _const_spec((1, D_MODEL)),
                  _in_rows_spec(l, 5, 7), _layer_spec(l, (LANES, D_MODEL)),
                  _const_spec((1, LANES))],
        out_specs=[row(SSM_INNER), row(CONV_DIM), row(LANES)],
        out_shape=[jax.ShapeDtypeStruct((t, SSM_INNER), BF16),
                   jax.ShapeDtypeStruct((t, CONV_DIM), F32),
                   jax.ShapeDtypeStruct((t, LANES), F32)],
        compiler_params=_params(),
        name="ssm_in_sample",
    )(x, nw, wzx, wdt, dtb)


def _gla_rows(q, k, bcum, blast):
    qf = q.astype(F32) * (GLA_DK ** -0.5)
    kf = k.astype(F32)
    q_in = (qf * jnp.exp2(bcum)).astype(BF16)
    k_in = (kf * jnp.exp2(-bcum)).astype(BF16)
    k_out = (kf * jnp.exp2(blast - bcum)).astype(BF16)
    return q_in, k_in, k_out


def _segment_weights(ac_s, itile, cmask):
    arow = jnp.sum(ac_s * itile, axis=0, keepdims=True)
    return jnp.where(cmask > 0.5, jnp.exp2(ac_s - arow), 0.0)


PAIR = 2


SEQS_PER_STEP = 1


def _core_prompt_kernel(q_ref, k_ref, v_ref, la_ref, xa_ref, dt_ref, alog3_ref, dskip_ref,
                        btril3_ref, e3_ref, itile_ref, cmask_ref, bmask_ref, *rest, rows, nj):
    o_ref, y_ref, gla_ref, ssm_ref, s_ref, ht_ref = rest[-6:]
    j = pl.program_id(1)

    @pl.when(j == 0)
    def _():
        s_ref[...] = jnp.zeros(s_ref.shape, F32)
        ht_ref[...] = jnp.zeros(ht_ref.shape, F32)

    btril3 = btril3_ref[...]
    causal = btril3[0:CHUNK, 0:CHUNK] > 0.5
    ones16 = jnp.ones((BF16_ROWS, LANES), BF16)
    row16 = lax.broadcasted_iota(jnp.int32, (BF16_ROWS, 1), 0)
    itile = itile_ref[...]
    cmask = cmask_ref[...]
    bmask = bmask_ref[...]
    e3 = e3_ref[...]
    a3 = -jnp.exp(alog3_ref[...]) * LOG2E

    bcum_all, dtx_all, acx_all = [], [], []
    for sq in range(SEQS_PER_STEP):
        bcum_all.append(_dot(btril3, _stack3(la_ref[sq] * LOG2E)))
        dt3 = dt_ref[sq]
        acum3 = _dot(btril3, _stack3(dt3 * a3))
        dtx_all.append(_dot(_pieces_by_lane(dt3), e3))
        acx_all.append(_dot(_pieces_by_lane(acum3), e3))

    chunks = [slice(c * CHUNK, (c + 1) * CHUNK) for c in range(rows // CHUNK)]
    kls = [slice(h * GLA_DK, (h + 1) * GLA_DK) for h in range(GLA_HEADS)]
    vls = [slice(h * GLA_DV, (h + 1) * GLA_DV) for h in range(GLA_HEADS)]
    gls = [slice(g * GROUP_LANES, (g + 1) * GROUP_LANES) for g in range(SSM_GROUPS)]

    for sq in range(SEQS_PER_STEP):
        prep = []
        for r in chunks:
            bcum = bcum_all[sq][r, :]
            blast = bcum[CHUNK - 1:CHUNK, :]
            q_in, k_in, k_out = _gla_rows(q_ref[sq, r, :], k_ref[sq, r, :], bcum, blast)
            bl_pieces = _pieces_by_row(jnp.broadcast_to(blast, (BF16_ROWS, GLA_DK_TOTAL)), row16)
            prep.append((q_in, k_in, k_out, bl_pieces, v_ref[sq, r, :]))
        pairs = [(c, h) for h in range(GLA_HEADS) for c in range(len(chunks))]
        qk = {(c, h): _dot_nt(prep[c][0][:, kls[h]], prep[c][1][:, kls[h]]) for c, h in pairs}
        sc = {ch: jnp.where(causal, qk[ch], 0.0).astype(BF16) for ch in pairs}
        intra = {}
        for c, h in pairs:
            _, _, k_out, bl_pieces, v = prep[c]
            dec = jnp.exp2(_dot_tn(bl_pieces[:, kls[h]], ones16))
            intra[c, h] = (_dot(sc[c, h], v[:, vls[h]]), _dot_tn(k_out[:, kls[h]], v[:, vls[h]]), dec)
        for c, r in enumerate(chunks):
            for h, (kl, vl) in enumerate(zip(kls, vls)):
                o_intra, ds, dec = intra[c, h]
                s_old = s_ref[sq, h]
                o_ref[sq, r, vl] = (o_intra + _dot(prep[c][0][:, kl], s_old.astype(BF16))).astype(BF16)
                s_ref[sq, h] = s_old * jnp.concatenate([dec, dec], axis=1) + ds
        prep = []
        for r in chunks:
            acx = acx_all[sq][r, :]
            alast = acx[CHUNK - 1:CHUNK, :]
            xa = xa_ref[sq, r, :]
            xs = xa[:, 0:SSM_INNER].astype(F32)
            xg = xs * dtx_all[sq][r, :]
            xw = (xg * jnp.exp2(alast - acx)).astype(BF16)
            seg = _segment_weights(acx, itile, cmask)
            prep.append((xa, xs, xg.astype(BF16), xw, jnp.exp2(acx), seg, jnp.exp2(alast)))
        pairs = [(c, g) for g in range(SSM_GROUPS) for c in range(len(chunks))]
        b_of = lambda c, g: prep[c][0][:, SSM_INNER + g * SSM_DSTATE:SSM_INNER + (g + 1) * SSM_DSTATE]
        c_of = lambda c, g: prep[c][0][:, SSM_INNER + (SSM_GROUPS + g) * SSM_DSTATE:
                                       SSM_INNER + (SSM_GROUPS + g + 1) * SSM_DSTATE]
        cbx = {(c, g): _dot_nt(c_of(c, g), jnp.concatenate([b_of(c, g)] * SSM_HPG, axis=0)) for c, g in pairs}
        w = {(c, g): (cbx[c, g] * prep[c][5][:, gls[g]]).astype(BF16) for c, g in pairs}
        intra = {}
        for c, g in pairs:
            xs, xgb, xw = prep[c][1], prep[c][2], prep[c][3]
            yd = []
            for pr in range(SSM_HPG // PAIR):
                lo = g * GROUP_LANES + pr * LANES
                xblk = jnp.concatenate([xgb[:, lo:lo + LANES]] * PAIR, axis=0) * bmask
                yd.append(_dot(w[c, g][:, pr * LANES:(pr + 1) * LANES], xblk))
            y_part = jnp.concatenate(yd, axis=1) + dskip_ref[:, gls[g]] * xs[:, gls[g]]
            intra[c, g] = (c_of(c, g), y_part, _dot_tn(b_of(c, g), xw[:, gls[g]]))
        for c, r in enumerate(chunks):
            eac, dec_row = prep[c][4], prep[c][6]
            for g, gl in enumerate(gls):
                cg, y_part, ds_t = intra[c, g]
                h_old = ht_ref[sq, :, gl]
                y_ref[sq, r, gl] = (y_part + _dot(cg, h_old.astype(BF16)) * eac[:, gl]).astype(BF16)
                ht_ref[sq, :, gl] = h_old * dec_row[:, gl] + ds_t

    @pl.when(j == nj - 1)
    def _():
        for sq in range(SEQS_PER_STEP):
            gla_ref[sq] = s_ref[sq]
            ssm_ref[sq] = ht_ref[sq].T


def _core_prompt(q, k, v, la, xa, dt, alog3, dskip_x, consts, *, nb, rows, l, depth, final_states):
    t = q.shape[0]
    nq = SEQS_PER_STEP
    half, length = nb // nq, t // nb
    nj = length // rows
    split = lambda a: a.reshape(nq, half, length, a.shape[-1])
    row = lambda width: pl.BlockSpec((nq, None, rows, width), lambda b, j: (0, b, j, 0))
    inputs = [split(q), split(k), split(v), split(la), split(xa), split(dt), alog3, dskip_x, *consts]
    in_specs = ([row(512), row(512), row(1024), row(512), row(CONV_DIM), row(LANES),
                 _const_spec((1, LANES)), _const_spec((1, SSM_INNER))]
                + [_const_spec(c.shape) for c in consts])
    aliases = {}
    if final_states is not None:
        aliases = {len(inputs): 2, len(inputs) + 1: 3}
        inputs += list(final_states)
        in_specs += [pl.BlockSpec(memory_space=pl.ANY)] * 2
    o, y, s_fin, h_fin = pl.pallas_call(
        functools.partial(_core_prompt_kernel, rows=rows, nj=nj),
        grid=(half, nj),
        in_specs=in_specs,
        out_specs=[row(GLA_DV_TOTAL), row(SSM_INNER),
                   pl.BlockSpec((None, nq, None, GLA_HEADS, GLA_DK, GLA_DV), lambda b, j: (l, 0, b, 0, 0, 0)),
                   pl.BlockSpec((None, nq, None, SSM_INNER, SSM_DSTATE), lambda b, j: (l, 0, b, 0, 0))],
        out_shape=[jax.ShapeDtypeStruct((nq, half, length, GLA_DV_TOTAL), BF16),
                   jax.ShapeDtypeStruct((nq, half, length, SSM_INNER), BF16),
                   jax.ShapeDtypeStruct((depth, nq, half, GLA_HEADS, GLA_DK, GLA_DV), F32),
                   jax.ShapeDtypeStruct((depth, nq, half, SSM_INNER, SSM_DSTATE), F32)],
        input_output_aliases=aliases,
        scratch_shapes=[pltpu.VMEM((nq, GLA_HEADS, GLA_DK, GLA_DV), F32),
                        pltpu.VMEM((nq, SSM_DSTATE, SSM_INNER), F32)],
        compiler_params=_params(n_arbitrary=2),
        name="core_prompt",
    )(*inputs)
    return o.reshape(t, GLA_DV_TOTAL), y.reshape(t, SSM_INNER), (s_fin, h_fin)


SEQ_PER_STEP = 4
SAMPLE_LEN = 4
SAMPLE_ROWS = SEQ_PER_STEP * SAMPLE_LEN


def _core_sample_kernel(q_ref, k_ref, v_ref, la_ref, xr_ref, c4_ref, dt_ref, s0_ref, h0_ref,
                        cw_ref, cb_ref, alog3_ref, dskip_ref,
                        bs3_ref, shmat_ref, e3_ref, es3_ref, itile_ref, cmask_ref, bmask_ref,
                        qmask_ref, smask_ref, pos_ref, *rest):
    o_ref, y_ref, s1_ref, h1_ref = rest[-4:]
    rs = SAMPLE_ROWS
    bs3 = bs3_ref[...]
    same_causal = bs3[0:rs, 0:rs] > 0.5
    ones16 = jnp.ones((BF16_ROWS, LANES), BF16)
    pos = pos_ref[...]
    xr = xr_ref[...]
    shifted = _dot(shmat_ref[...], jnp.concatenate(_split3(xr) + _split3(c4_ref[...]), axis=0))
    acc = cb_ref[...] + xr * cw_ref[CONV_W - 1:CONV_W, :]
    for i in range(CONV_W - 1):
        back = CONV_W - 1 - i
        acc = acc + shifted[(back - 1) * rs:back * rs, :] * cw_ref[i:i + 1, :]
    xa = _silu(acc)
    sums = _dot(bs3, _stack3(la_ref[...] * LOG2E))
    bcum, blast = sums[0:rs, :], sums[rs:2 * rs, :]
    q_in, k_in, k_out = _gla_rows(q_ref[...], k_ref[...], bcum, blast)
    bl_pieces = _pieces_by_row(blast, pos)
    v = v_ref[...]
    qmask = qmask_ref[...]
    seq_blk = lambda a: jnp.concatenate([a] * SEQ_PER_STEP, axis=1)
    for h in range(GLA_HEADS):
        kl = slice(h * GLA_DK, (h + 1) * GLA_DK)
        vl = slice(h * GLA_DV, (h + 1) * GLA_DV)
        sc = jnp.where(same_causal, _dot_nt(q_in[:, kl], k_in[:, kl]), 0.0).astype(BF16)
        s_old = s0_ref[:, h].reshape(SEQ_PER_STEP * GLA_DK, GLA_DV)
        o_h = _dot(sc, v[:, vl]) + _dot(seq_blk(q_in[:, kl]) * qmask, s_old.astype(BF16))
        o_ref[:, vl] = o_h.astype(BF16)
        ds = _dot_tn(seq_blk(k_out[:, kl]) * qmask, v[:, vl])
        dec = jnp.exp2(_dot_tn(seq_blk(bl_pieces[:, kl]) * qmask, ones16))
        s_new = s_old * jnp.concatenate([dec, dec], axis=1) + ds
        s1_ref[:, h] = s_new.reshape(SEQ_PER_STEP, GLA_DK, GLA_DV)
    dt3 = dt_ref[...]
    sums = _dot(bs3, _stack3(dt3 * (-jnp.exp(alog3_ref[...]) * LOG2E)))
    acum3, alast3 = sums[0:rs, :], sums[rs:2 * rs, :]
    p_ac = _pieces_by_lane(acum3)
    wide = _dot(jnp.concatenate([_pieces_by_lane(dt3), p_ac, _pieces_by_lane(alast3)], axis=0), e3_ref[...])
    dtx, acx, alx = wide[0:rs, :], wide[rs:2 * rs, :], wide[2 * rs:3 * rs, :]
    xs = xa[:, 0:SSM_INNER]
    xg = xs * dtx
    xgb = xg.astype(BF16)
    xw = xg * jnp.exp2(alx - acx)
    eac = jnp.exp2(acx)
    seg = _segment_weights(_dot(p_ac, es3_ref[...]), itile_ref[...], cmask_ref[...])
    al_pieces = _pieces_by_row(alx, pos)
    smask = smask_ref[...]
    sl = SSM_HPG * rs
    for g in range(SSM_GROUPS):
        gl = slice(g * GROUP_LANES, (g + 1) * GROUP_LANES)
        bg = xa[:, SSM_INNER + g * SSM_DSTATE:SSM_INNER + (g + 1) * SSM_DSTATE].astype(BF16)
        cg = xa[:, SSM_INNER + (SSM_GROUPS + g) * SSM_DSTATE:
                SSM_INNER + (SSM_GROUPS + g + 1) * SSM_DSTATE].astype(BF16)
        cbx = _dot_nt(cg, jnp.concatenate([bg] * SSM_HPG, axis=0))
        w = (cbx * seg[:, g * sl:(g + 1) * sl]).astype(BF16)
        xblk = jnp.concatenate([xgb[:, gl]] * SSM_HPG, axis=0) * bmask_ref[...]
        yd = _dot(w, xblk)
        h_old = [h0_ref[s, gl, :] for s in range(SEQ_PER_STEP)]
        yo = _dot_nt(seq_blk(cg) * qmask, jnp.concatenate([hs.astype(BF16) for hs in h_old], axis=1))
        y = yd + yo * eac[:, gl] + dskip_ref[:, gl] * xs[:, gl]
        y_ref[:, gl] = y.astype(BF16)
        ds = _dot_tn(seq_blk(xw[:, gl]).astype(BF16) * smask, bg)
        dec = jnp.exp2(_dot_tn(seq_blk(al_pieces[:, gl]) * smask, ones16))
        for s in range(SEQ_PER_STEP):
            sr = slice(s * GROUP_LANES, (s + 1) * GROUP_LANES)
            h1_ref[s, gl, :] = h_old[s] * dec[sr, :] + ds[sr, :]


def _core_sample(q, k, v, la, xr, c4, dt, s_all, h_all, cw, cb, alog3, dskip_x, consts, *, l, new_states):
    t = q.shape[0]
    rs = SAMPLE_ROWS
    nseq = t // SAMPLE_LEN
    row = lambda width: pl.BlockSpec((rs, width), lambda i: (i, 0))
    s_spec = pl.BlockSpec((None, SEQ_PER_STEP, GLA_HEADS, GLA_DK, GLA_DV), lambda i: (l, i, 0, 0, 0))
    h_spec = pl.BlockSpec((None, SEQ_PER_STEP, SSM_INNER, SSM_DSTATE), lambda i: (l, i, 0, 0))
    inputs = [q, k, v, la, xr, c4, dt, s_all, h_all, cw, cb, alog3, dskip_x, *consts]
    in_specs = ([row(512), row(512), row(1024), row(512), row(CONV_DIM), row(CONV_DIM), row(LANES),
                 s_spec, h_spec,
                 _const_spec((CONV_W, CONV_DIM)), _const_spec((1, CONV_DIM)),
                 _const_spec((1, LANES)), _const_spec((1, SSM_INNER))]
                + [_const_spec(c.shape) for c in consts])
    aliases = {}
    if new_states is not None:
        aliases = {len(inputs): 2, len(inputs) + 1: 3}
        inputs += list(new_states)
        in_specs += [pl.BlockSpec(memory_space=pl.ANY)] * 2
    return pl.pallas_call(
        _core_sample_kernel,
        grid=(nseq // SEQ_PER_STEP,),
        in_specs=in_specs,
        out_specs=[row(GLA_DV_TOTAL), row(SSM_INNER), s_spec, h_spec],
        out_shape=[jax.ShapeDtypeStruct((t, GLA_DV_TOTAL), BF16),
                   jax.ShapeDtypeStruct((t, SSM_INNER), BF16),
                   jax.ShapeDtypeStruct(s_all.shape, F32),
                   jax.ShapeDtypeStruct(h_all.shape, F32)],
        input_output_aliases=aliases,
        compiler_params=_params(),
        name="core_sample",
    )(*inputs)


def _merge_kernel(x_ref, o_ref, sg_ref, y_ref, sz_ref, ga_ref, gb_ref,
                  gn_ref, sn_ref, wpg_ref, wps_ref, wo_ref, out_ref):
    o = o_ref[...].astype(F32)
    parts = []
    for h in range(GLA_HEADS):
        vl = slice(h * GLA_DV, (h + 1) * GLA_DV)
        parts.append(_rms(o[:, vl]) * gn_ref[:, vl])
    og = (jnp.concatenate(parts, axis=1) * sg_ref[...].astype(F32)).astype(BF16)
    yz = y_ref[...].astype(F32) * sz_ref[...].astype(F32)
    yn = (_rms(yz) * sn_ref[...]).astype(BF16)
    m = (ga_ref[...].astype(F32) * _dot(og, wpg_ref[...])
         + gb_ref[...].astype(F32) * _dot(yn, wps_ref[...]))
    out_ref[...] = x_ref[...] + _dot(m.astype(BF16), wo_ref[...])


def _merge(rows_p, rows_s, gn, sn, wpg, wps, wo, *, l, tm):
    body = lambda rows, sh, outs, scratch: _merge_kernel(*rows, *sh, outs[0])
    (xp,), (xs,) = _two_group_call(
        body, "merge", list(rows_p), list(rows_s), [gn, sn, wpg, wps, wo],
        [_const_spec((1, GLA_DV_TOTAL)), _const_spec((1, SSM_INNER)), _layer_spec(l, (GLA_DV_TOTAL, D_MODEL)),
         _layer_spec(l, (SSM_INNER, D_MODEL)), _layer_spec(l, (D_MODEL, D_MODEL))],
        [(D_MODEL, F32)], tm=tm)
    return xp, xs


def _expand_table(width):
    t = np.zeros((LANES, SSM_HEADS * width), np.float32)
    for k in range(PIECES):
        for h in range(SSM_HEADS):
            t[k * SSM_HEADS + h, h * width:(h + 1) * width] = 1.0
    return jnp.asarray(t, dtype=BF16)


def _segment_tables(rows, allowed):
    s_of_lane = np.arange(SSM_HPG * rows) % rows
    itile = (np.arange(rows)[:, None] == s_of_lane[None, :]).astype(np.float32)
    cmask = allowed[:, s_of_lane].astype(np.float32)
    return itile, cmask


def _head_block_mask(heads, rows):
    hr = np.arange(heads * rows) // rows
    hc = np.arange(heads * SSM_HEADDIM) // SSM_HEADDIM
    return jnp.asarray((hr[:, None] == hc[None, :]).astype(np.float32), dtype=BF16)


def _prompt_consts(rows):
    idx = np.arange(rows)
    btril = ((idx[:, None] // CHUNK == idx[None, :] // CHUNK) & (idx[None, :] <= idx[:, None])).astype(np.float32)
    tril = np.tril(np.ones((CHUNK, CHUNK), np.float32))
    itile, cmask = _segment_tables(CHUNK, tril)
    tile4 = lambda a: jnp.asarray(np.tile(a, (1, SSM_GROUPS)))
    return (jnp.asarray(np.tile(btril, (1, PIECES)), dtype=BF16), _expand_table(SSM_HEADDIM),
            tile4(itile), tile4(cmask), _head_block_mask(PAIR, CHUNK))


def _sample_consts():
    rs = SAMPLE_ROWS
    seq = np.arange(rs) // SAMPLE_LEN
    pos = np.arange(rs) % SAMPLE_LEN
    same = (seq[:, None] == seq[None, :]).astype(np.float32)
    btril = same * (pos[None, :] <= pos[:, None])
    bs3 = np.tile(np.concatenate([btril, same], axis=0), (1, PIECES))
    shmat = np.zeros((CONV_W - 1, rs, 2, rs), np.float32)
    for back in range(1, CONV_W):
        for r in range(rs):
            if pos[r] >= back:
                shmat[back - 1, r, 0, r - back] = 1.0
            else:
                shmat[back - 1, r, 1, seq[r] * SAMPLE_LEN + SAMPLE_LEN + pos[r] - back] = 1.0
    shmat = np.repeat(shmat[:, :, :, None, :], PIECES, axis=3).reshape((CONV_W - 1) * rs, 2 * PIECES * rs)
    itile, cmask = _segment_tables(rs, btril)
    tile4 = lambda a: jnp.asarray(np.tile(a, (1, SSM_GROUPS)))
    assert GLA_DK == SSM_DSTATE == LANES
    qmask = (seq[:, None] == (np.arange(SEQ_PER_STEP * LANES) // LANES)[None, :]).astype(np.float32)
    smask = (seq[:, None] == (np.arange(SEQ_PER_STEP * GROUP_LANES) // GROUP_LANES)[None, :]).astype(np.float32)
    return (jnp.asarray(bs3, dtype=BF16), jnp.asarray(shmat, dtype=BF16),
            _expand_table(SSM_HEADDIM), _expand_table(rs), tile4(itile), tile4(cmask),
            _head_block_mask(SSM_HPG, rs), jnp.asarray(qmask, dtype=BF16), jnp.asarray(smask, dtype=BF16),
            jnp.asarray(pos.reshape(rs, 1).astype(np.int32)))


def _prep_weights(p):
    win = jnp.swapaxes(p["w_in"], 1, 2).astype(BF16)
    piece = lambda i, j: jnp.swapaxes(p["w_in"][:, :, IN_OFFS[i]:IN_OFFS[j]], 1, 2).astype(BF16)
    lane_pad = lambda w: jnp.pad(w, ((0, 0), (0, 0), (0, LANES - w.shape[-1])))
    row_pad = lambda w: jnp.pad(w, ((0, 0), (0, LANES - w.shape[1]), (0, 0)))
    rep3 = lambda a: lane_pad(jnp.concatenate([a] * PIECES, axis=-1))
    return dict(
        win=win.reshape(-1, D_MODEL), wa1=row_pad(piece(4, 5)),
        wdt3=row_pad(jnp.concatenate([piece(7, 8)] * PIECES, axis=1)),
        wa2=jnp.pad(p["w_gla_a2"], ((0, 0), (0, LANES - GLA_GATE_RANK), (0, 0))).astype(BF16),
        dtb3=rep3(p["dt_bias"][:, None, :]), alog3=rep3(p["a_log"][:, None, :]),
        dskip_x=jnp.repeat(p["d_skip"], SSM_HEADDIM, axis=-1)[:, None, :],
        wpg=p["w_proj_gla"].astype(BF16), wps=p["w_proj_ssm"].astype(BF16), wo=p["w_out"].astype(BF16),
        gu1=p["w_ffn1_gu"].astype(BF16), d1=p["w_ffn1_down"].astype(BF16),
        gu2=p["w_ffn2_gu"].astype(BF16), d2=p["w_ffn2_down"].astype(BF16),
    )


PROMPT_TM = 512
IN_PROJ_TM = 1024
CORE_ROWS = 256


def _trunk(xp, xs, state_gla, state_ssm, state_conv, w, p, fw, nb):
    depth = p["w_in"].shape[0]
    ts = xs.shape[0]
    nseq = ts // SAMPLE_LEN
    row = lambda name, l: p[name][l].reshape(1, -1)
    pconsts, sconsts = _prompt_consts(CORE_ROWS), _sample_consts()
    h_all = state_ssm.reshape(depth, nseq, SSM_INNER, SSM_DSTATE)
    finals_p, finals_s = None, None
    convs_p, convs_s = [], []
    for l in range(depth):
        xp, xs = _ffn(xp, xs, row("norm_ffn1", l), w["gu1"], w["d1"], fw, l=l, tm=PROMPT_TM, final=False)
        (q, k, v, sg, ga, gb, la), (q_s, k_s, v_s, sg_s, ga_s, gb_s, la_s) = _gla_in(
            xp, xs, row("norm_mix", l), w["win"], w["wa1"], w["wa2"], row("b_gla_a2", l), l=l, tm=IN_PROJ_TM)
        sz, xa, dt, tail = _ssm_in_prompt(xp, row("norm_mix", l), w["win"], w["wdt3"], w["dtb3"][l],
                                          p["conv_w"][l], row("conv_b", l), l=l, nb=nb, tm=IN_PROJ_TM)
        o, y, finals_p = _core_prompt(q, k, v, la, xa, dt, w["alog3"][l], w["dskip_x"][l], pconsts,
                                      nb=nb, rows=CORE_ROWS, l=l, depth=depth, final_states=finals_p)
        convs_p.append(tail)
        sz_s, xr, dt_s = _ssm_in_sample(xs, row("norm_mix", l), w["win"], w["wdt3"], w["dtb3"][l], l=l, tm=ts)
        c4 = jnp.pad(state_conv[l], ((0, 0), (SAMPLE_LEN - (CONV_W - 1), 0), (0, 0))).reshape(ts, CONV_DIM)
        o_s, y_s, s1, h1 = _core_sample(q_s, k_s, v_s, la_s, xr, c4, dt_s, state_gla, h_all, p["conv_w"][l],
                                        row("conv_b", l), w["alog3"][l], w["dskip_x"][l], sconsts,
                                        l=l, new_states=finals_s)
        finals_s = (s1, h1)
        convs_s.append(xr.reshape(nseq, SAMPLE_LEN, CONV_DIM)[:, SAMPLE_LEN - (CONV_W - 1):])
        xp, xs = _merge((xp, o, sg, y, sz, ga, gb), (xs, o_s, sg_s, y_s, sz_s, ga_s, gb_s),
                        row("gla_norm", l), row("ssm_norm", l), w["wpg"], w["wps"], w["wo"], l=l, tm=PROMPT_TM)
        xp, xs = _ffn(xp, xs, row("norm_ffn2", l), w["gu2"], w["d2"], fw, l=l, tm=PROMPT_TM,
                      final=l == depth - 1)
    s_fin, h_fin = finals_p
    s1, h1 = finals_s
    return (xp, xs,
            s_fin.reshape(depth, nb, GLA_HEADS, GLA_DK, GLA_DV),
            h_fin.reshape(depth, nb, SSM_HEADS, SSM_HEADDIM, SSM_DSTATE), jnp.stack(convs_p),
            s1, h1.reshape(state_ssm.shape), jnp.stack(convs_s))


def kernel(x_prompt, x_sample, state_gla, state_ssm, state_conv, norm_ffn1, w_ffn1_gu, w_ffn1_down, norm_mix, w_in, w_gla_a2, b_gla_a2, gla_norm, conv_w, conv_b, dt_bias, a_log, d_skip, ssm_norm, w_proj_gla, w_proj_ssm, w_out, norm_ffn2, w_ffn2_gu, w_ffn2_down, norm_final):
    p = dict(norm_ffn1=norm_ffn1, w_ffn1_gu=w_ffn1_gu, w_ffn1_down=w_ffn1_down, norm_mix=norm_mix,
             w_in=w_in, w_gla_a2=w_gla_a2, b_gla_a2=b_gla_a2, gla_norm=gla_norm, conv_w=conv_w,
             conv_b=conv_b, dt_bias=dt_bias, a_log=a_log, d_skip=d_skip, ssm_norm=ssm_norm,
             w_proj_gla=w_proj_gla, w_proj_ssm=w_proj_ssm, w_out=w_out, norm_ffn2=norm_ffn2,
             w_ffn2_gu=w_ffn2_gu, w_ffn2_down=w_ffn2_down)
    w = _prep_weights(p)
    fw = norm_final.reshape(1, D_MODEL)
    bp, lp, _ = x_prompt.shape
    bs, ls, _ = x_sample.shape
    assert ls == SAMPLE_LEN and lp % CORE_ROWS == 0 and lp % PROMPT_TM == 0 and lp % IN_PROJ_TM == 0
    assert bs % SEQ_PER_STEP == 0 and bp % SEQS_PER_STEP == 0
    yp, ys, gla_p, ssm_p, conv_p, gla_s, ssm_s, conv_s = _trunk(
        x_prompt.reshape(bp * lp, D_MODEL), x_sample.reshape(bs * ls, D_MODEL),
        state_gla, state_ssm, state_conv, w, p, fw, bp)
    return (yp.reshape(bp, lp, D_MODEL), ys.reshape(bs, ls, D_MODEL),
            gla_p, ssm_p, conv_p, gla_s, ssm_s, conv_s)
```

```python
import functools

import numpy as np
import jax
import jax.numpy as jnp
from jax import lax
from jax.experimental import pallas as pl
from jax.experimental.pallas import tpu as pltpu

F32 = jnp.float32
BF16 = jnp.bfloat16

D_MODEL = 1024
EPS = 1e-6
FFN_DIM = 2816
GLA_HEADS = 4
GLA_DK = 128
GLA_DV = 256
GLA_DK_TOTAL = GLA_HEADS * GLA_DK
GLA_DV_TOTAL = GLA_HEADS * GLA_DV
GLA_GATE_RANK = 16
GLA_TAU = 16.0
SSM_INNER = 2048
SSM_HEADDIM = 64
SSM_HEADS = 32
SSM_GROUPS = 4
SSM_HPG = SSM_HEADS // SSM_GROUPS
SSM_DSTATE = 128
CONV_W = 4
CONV_DIM = SSM_INNER + 2 * SSM_GROUPS * SSM_DSTATE
CHUNK = 64
IN_SPLITS = (GLA_DK_TOTAL, GLA_DK_TOTAL, GLA_DV_TOTAL, GLA_DV_TOTAL, GLA_GATE_RANK,
             SSM_INNER, CONV_DIM, SSM_HEADS, D_MODEL, D_MODEL)

LANES = 128
BF16_ROWS = 16
FFN_TILE = 256
GROUP_LANES = SSM_HPG * SSM_HEADDIM
PIECES = 3
LOG2E = 1.4426950408889634
VMEM_LIMIT = 56 * 1024 * 1024

NT_DIMS = (((1,), (1,)), ((), ()))
TN_DIMS = (((0,), (0,)), ((), ()))


def _dot(a, b):
    return jnp.dot(a, b, preferred_element_type=F32)


def _dot_nt(a, b):
    return lax.dot_general(a, b, NT_DIMS, preferred_element_type=F32)


def _dot_tn(a, b):
    return lax.dot_general(a, b, TN_DIMS, preferred_element_type=F32)


def _rms(x):
    return x * lax.rsqrt(jnp.mean(x * x, axis=-1, keepdims=True) + EPS)


def _sigmoid(x):
    return 0.5 * jnp.tanh(0.5 * x) + 0.5


def _silu(x):
    h = 0.5 * x
    return h * jnp.tanh(h) + h


def _softplus(x):
    return jnp.maximum(x, 0.0) + jnp.log1p(jnp.exp(-jnp.abs(x)))


def _split3(x):
    hi = x.astype(BF16)
    r1 = x - hi.astype(F32)
    mid = r1.astype(BF16)
    lo = (r1 - mid.astype(F32)).astype(BF16)
    return hi, mid, lo


def _stack3(x):
    return jnp.concatenate(_split3(x), axis=0)


def _pieces_by_lane(x):
    hi, mid, lo = (piece.astype(F32) for piece in _split3(x))
    lane = lax.broadcasted_iota(jnp.int32, x.shape, 1)
    picked = jnp.where(lane < SSM_HEADS, hi, jnp.where(lane < 2 * SSM_HEADS, mid, lo))
    return picked.astype(BF16)


def _pieces_by_row(x, piece_of_row):
    hi, mid, lo = (piece.astype(F32) for piece in _split3(x))
    idx = jnp.broadcast_to(piece_of_row, x.shape)
    picked = jnp.where(idx == 0, hi, jnp.where(idx == 1, mid, jnp.where(idx == 2, lo, 0.0)))
    return picked.astype(BF16)


def _params(n_arbitrary=1):
    return pltpu.CompilerParams(dimension_semantics=("arbitrary",) * n_arbitrary,
                                vmem_limit_bytes=VMEM_LIMIT)


def _const_spec(shape):
    nd = len(shape)
    return pl.BlockSpec(shape, lambda *_: (0,) * nd)


def _layer_spec(l, shape):
    return pl.BlockSpec((None,) + shape, lambda *_: (l, 0, 0), pipeline_mode=pl.Buffered(1))


IN_OFFS = tuple(int(o) for o in np.cumsum((0,) + IN_SPLITS))


def _in_rows_spec(l, first, last):
    start, rows = l * IN_OFFS[-1] + IN_OFFS[first], IN_OFFS[last] - IN_OFFS[first]
    return pl.BlockSpec((pl.Element(rows), pl.Element(D_MODEL)), lambda *_: (start, 0),
                        pipeline_mode=pl.Buffered(1))


def _two_group_call(body, name, rows_p, rows_s, shared, shared_specs, outs, *, tm, scratch_shapes=()):
    tp, ts = rows_p[0].shape[0], rows_s[0].shape[0]
    n_p = tp // tm
    assert tp % tm == 0 and len(rows_p) == len(rows_s)
    p_spec = lambda width: pl.BlockSpec((tm, width), lambda i: (jnp.minimum(i, n_p - 1), 0))
    s_spec = lambda width: pl.BlockSpec((ts, width), lambda i: (0, 0))
    n_in, n_sh, n_out = len(rows_p), len(shared), len(outs)

    def kernel(*refs):
        in_p, in_s = refs[:n_in], refs[n_in:2 * n_in]
        sh = refs[2 * n_in:2 * n_in + n_sh]
        out_p = refs[2 * n_in + n_sh:2 * n_in + n_sh + n_out]
        out_s = refs[2 * n_in + n_sh + n_out:2 * n_in + n_sh + 2 * n_out]
        scratch = refs[2 * n_in + n_sh + 2 * n_out:]
        i = pl.program_id(0)

        @pl.when(i < n_p)
        def _():
            body(in_p, sh, out_p, scratch)

        @pl.when(i == n_p)
        def _():
            body(in_s, sh, out_s, scratch)

    res = pl.pallas_call(
        kernel,
        grid=(n_p + 1,),
        in_specs=[p_spec(a.shape[1]) for a in rows_p] + [s_spec(a.shape[1]) for a in rows_s] + list(shared_specs),
        out_specs=[p_spec(width) for width, _ in outs] + [s_spec(width) for width, _ in outs],
        out_shape=[jax.ShapeDtypeStruct((tp, width), dt) for width, dt in outs]
                  + [jax.ShapeDtypeStruct((ts, width), dt) for width, dt in outs],
        scratch_shapes=list(scratch_shapes),
        compiler_params=_params(),
        name=name,
    )(*rows_p, *rows_s, *shared)
    return res[:n_out], res[n_out:]


def _ffn_kernel(x_ref, nw_ref, wgu_ref, wd_ref, fw_ref, o_ref, acc_ref, *, final):
    x = x_ref[...]
    rows = x.shape[0]
    xn = (_rms(x) * nw_ref[...]).astype(BF16)
    for c in range(FFN_DIM // FFN_TILE):
        lo = c * FFN_TILE
        g = _dot(xn, wgu_ref[:, lo:lo + FFN_TILE])
        u = _dot(xn, wgu_ref[:, FFN_DIM + lo:FFN_DIM + lo + FFN_TILE])
        a = (_silu(g) * u).astype(BF16)
        part = _dot(a, wd_ref[lo:lo + FFN_TILE, :])
        if c == 0:
            acc_ref[0:rows, :] = part
        else:
            acc_ref[0:rows, :] += part
    y = x + 0.5 * acc_ref[0:rows, :]
    if final:
        y = _rms(y) * fw_ref[...]
    o_ref[...] = y


def _ffn(xp, xs, nw, wgu, wd, fw, *, l, tm, final):
    body = lambda rows, sh, outs, scratch: _ffn_kernel(rows[0], *sh, outs[0], scratch[0], final=final)
    (yp,), (ys,) = _two_group_call(
        body, "ffn_final" if final else "ffn", [xp], [xs], [nw, wgu, wd, fw],
        [_const_spec((1, D_MODEL)), _layer_spec(l, (D_MODEL, 2 * FFN_DIM)), _layer_spec(l, (FFN_DIM, D_MODEL)),
         _const_spec((1, D_MODEL))],
        [(D_MODEL, F32)], tm=tm, scratch_shapes=[pltpu.VMEM((max(tm, xs.shape[0]), D_MODEL), F32)])
    return yp, ys


def _gla_in_kernel(x_ref, nw_ref, wqkvg_ref, wgate_ref, wa1_ref, wa2_ref, ba2_ref,
                   q_ref, k_ref, v_ref, sg_ref, ga_ref, gb_ref, la_ref):
    u = (_rms(x_ref[...]) * nw_ref[...]).astype(BF16)
    q_ref[...] = _dot_nt(u, wqkvg_ref[0:512, :]).astype(BF16)
    k_ref[...] = _dot_nt(u, wqkvg_ref[512:1024, :]).astype(BF16)
    v_ref[...] = _dot_nt(u, wqkvg_ref[1024:2048, :]).astype(BF16)
    sg_ref[...] = _silu(_dot_nt(u, wqkvg_ref[2048:3072, :])).astype(BF16)
    ga_ref[...] = _sigmoid(_dot_nt(u, wgate_ref[0:1024, :])).astype(BF16)
    gb_ref[...] = _sigmoid(_dot_nt(u, wgate_ref[1024:2048, :])).astype(BF16)
    a_lr = _dot_nt(u, wa1_ref[...]).astype(BF16)
    z = _dot(a_lr, wa2_ref[...]) + ba2_ref[...]
    la_ref[...] = -_softplus(-z) / GLA_TAU


def _gla_in(xp, xs, nw, win, wa1, wa2, ba2, *, l, tm):
    body = lambda rows, sh, outs, scratch: _gla_in_kernel(rows[0], *sh, *outs)
    return _two_group_call(
        body, "gla_in", [xp], [xs], [nw, win, win, wa1, wa2, ba2],
        [_const_spec((1, D_MODEL)), _in_rows_spec(l, 0, 4), _in_rows_spec(l, 8, 10),
         _layer_spec(l, (LANES, D_MODEL)), _layer_spec(l, (LANES, GLA_DK_TOTAL)), _const_spec((1, GLA_DK_TOTAL))],
        [(512, BF16), (512, BF16), (1024, BF16), (1024, BF16), (1024, BF16), (1024, BF16), (512, F32)], tm=tm)


ZX_COLS = SSM_INNER + CONV_DIM


CONV_COLS = 512
SUBLANES = 8


def _ssm_in_prompt_kernel(x_ref, nw_ref, wzx_ref, wdt_ref, dtb_ref, cw_ref, cb_ref,
                          sz_ref, xa_ref, dt_ref, tail_ref, carry_ref, *, tm):
    j = pl.program_id(1)

    @pl.when(j == 0)
    def _():
        carry_ref[...] = jnp.zeros(carry_ref.shape, F32)

    u = (_rms(x_ref[...]) * nw_ref[...]).astype(BF16)
    sz_ref[...] = _silu(_dot_nt(u, wzx_ref[0:SSM_INNER, :])).astype(BF16)
    dt_ref[...] = _softplus(_dot_nt(u, wdt_ref[...]) + dtb_ref[...])
    tiles = (tm // SUBLANES, SUBLANES, CONV_COLS)
    sub = lax.broadcasted_iota(jnp.int32, tiles, 1)
    for cc in range(CONV_DIM // CONV_COLS):
        cols = slice(cc * CONV_COLS, (cc + 1) * CONV_COLS)
        xr = _dot_nt(u, wzx_ref[SSM_INNER + cc * CONV_COLS:SSM_INNER + (cc + 1) * CONV_COLS, :])
        x3 = xr.reshape(tiles)
        p3 = jnp.concatenate([carry_ref[:, cols], xr[0:tm - SUBLANES, :]], axis=0).reshape(tiles)
        acc = cb_ref[:, cols] + xr * cw_ref[CONV_W - 1:CONV_W, cols]
        for i in range(CONV_W - 1):
            back = CONV_W - 1 - i
            sh = pltpu.roll(jnp.where(sub >= SUBLANES - back, p3, x3), back, axis=1)
            acc = acc + sh.reshape(tm, CONV_COLS) * cw_ref[i:i + 1, cols]
        xa_ref[:, cols] = _silu(acc).astype(BF16)
        carry_ref[:, cols] = xr[tm - SUBLANES:tm, :]
    tail_ref[0] = carry_ref[SUBLANES - (CONV_W - 1):SUBLANES, :]


def _ssm_in_prompt(x, nw, wzx, wdt, dtb, cw, cb, *, l, nb, tm):
    t = x.shape[0]
    nj = t // nb // tm
    row = lambda width: pl.BlockSpec((tm, width), lambda b, j: (b * nj + j, 0))
    return pl.pallas_call(
        functools.partial(_ssm_in_prompt_kernel, tm=tm),
        grid=(nb, nj),
        in_specs=[row(D_MODEL), _const_spec((1, D_MODEL)),
                  _in_rows_spec(l, 5, 7), _layer_spec(l, (LANES, D_MODEL)),
                  _const_spec((1, LANES)), _const_spec((CONV_W, CONV_DIM)), _const_spec((1, CONV_DIM))],
        out_specs=[row(SSM_INNER), row(CONV_DIM), row(LANES),
                   pl.BlockSpec((1, CONV_W - 1, CONV_DIM), lambda b, j: (b, 0, 0))],
        out_shape=[jax.ShapeDtypeStruct((t, SSM_INNER), BF16),
                   jax.ShapeDtypeStruct((t, CONV_DIM), BF16),
                   jax.ShapeDtypeStruct((t, LANES), F32),
                   jax.ShapeDtypeStruct((nb, CONV_W - 1, CONV_DIM), F32)],
        scratch_shapes=[pltpu.VMEM((SUBLANES, CONV_DIM), F32)],
        compiler_params=_params(n_arbitrary=2),
        name="ssm_in_prompt",
    )(x, nw, wzx, wdt, dtb, cw, cb)


def _ssm_in_sample_kernel(x_ref, nw_ref, wzx_ref, wdt_ref, dtb_ref, sz_ref, xr_ref, dt_ref):
    u = (_rms(x_ref[...]) * nw_ref[...]).astype(BF16)
    sz_ref[...] = _silu(_dot_nt(u, wzx_ref[0:SSM_INNER, :])).astype(BF16)
    xr_ref[...] = _dot_nt(u, wzx_ref[SSM_INNER:ZX_COLS, :])
    dt_ref[...] = _softplus(_dot_nt(u, wdt_ref[...]) + dtb_ref[...])


def _ssm_in_sample(x, nw, wzx, wdt, dtb, *, l, tm):
    t = x.shape[0]
    row = lambda width: pl.BlockSpec((tm, width), lambda i: (i, 0))
    return pl.pallas_call(
        _ssm_in_sample_kernel,
        grid=(t // tm,),
        in_specs=[row(D_MODEL), _const_spec((1, D_MODEL)),
                  _in_rows_spec(l, 5, 7), _layer_spec(l, (LANES, D_MODEL)),
                  _const_spec((1, LANES))],
        out_specs=[row(SSM_INNER), row(CONV_DIM), row(LANES)],
        out_shape=[jax.ShapeDtypeStruct((t, SSM_INNER), BF16),
                   jax.ShapeDtypeStruct((t, CONV_DIM), F32),
                   jax.ShapeDtypeStruct((t, LANES), F32)],
        compiler_params=_params(),
        name="ssm_in_sample",
    )(x, nw, wzx, wdt, dtb)


def _gla_rows(q, k, bcum, blast):
    qf = q.astype(F32) * (GLA_DK ** -0.5)
    kf = k.astype(F32)
    q_in = (qf * jnp.exp2(bcum)).astype(BF16)
    k_in = (kf * jnp.exp2(-bcum)).astype(BF16)
    k_out = (kf * jnp.exp2(blast - bcum)).astype(BF16)
    return q_in, k_in, k_out


def _segment_weights(ac_s, itile, cmask):
    arow = jnp.sum(ac_s * itile, axis=0, keepdims=True)
    return jnp.where(cmask > 0.5, jnp.exp2(ac_s - arow), 0.0)


PAIR = 2


SEQS_PER_STEP = 1
CHUNK_BATCH = 1


def _core_prompt_kernel(q_ref, k_ref, v_ref, la_ref, xa_ref, dt_ref, alog3_ref, dskip_ref,
                        btril3_ref, e3_ref, itile_ref, cmask_ref, bmask_ref, *rest, rows, nj):
    o_ref, y_ref, gla_ref, ssm_ref, s_ref, ht_ref = rest[-6:]
    j = pl.program_id(1)

    @pl.when(j == 0)
    def _():
        s_ref[...] = jnp.zeros(s_ref.shape, F32)
        ht_ref[...] = jnp.zeros(ht_ref.shape, F32)

    btril3 = btril3_ref[...]
    causal = btril3[0:CHUNK, 0:CHUNK] > 0.5
    ones16 = jnp.ones((BF16_ROWS, LANES), BF16)
    row16 = lax.broadcasted_iota(jnp.int32, (BF16_ROWS, 1), 0)
    itile = itile_ref[...]
    cmask = cmask_ref[...]
    bmask = bmask_ref[...]
    e3 = e3_ref[...]
    a3 = -jnp.exp(alog3_ref[...]) * LOG2E

    bcum_all, dtx_all, acx_all = [], [], []
    for sq in range(SEQS_PER_STEP):
        bcum_all.append(_dot(btril3, _stack3(la_ref[sq] * LOG2E)))
        dt3 = dt_ref[sq]
        acum3 = _dot(btril3, _stack3(dt3 * a3))
        dtx_all.append(_dot(_pieces_by_lane(dt3), e3))
        acx_all.append(_dot(_pieces_by_lane(acum3), e3))

    all_chunks = [slice(c * CHUNK, (c + 1) * CHUNK) for c in range(rows // CHUNK)]
    batches = [all_chunks[b:b + CHUNK_BATCH] for b in range(0, len(all_chunks), CHUNK_BATCH)]
    kls = [slice(h * GLA_DK, (h + 1) * GLA_DK) for h in range(GLA_HEADS)]
    vls = [slice(h * GLA_DV, (h + 1) * GLA_DV) for h in range(GLA_HEADS)]
    gls = [slice(g * GROUP_LANES, (g + 1) * GROUP_LANES) for g in range(SSM_GROUPS)]

    for chunks, sq in [(batch, sq) for batch in batches for sq in range(SEQS_PER_STEP)]:
        prep = []
        for r in chunks:
            bcum = bcum_all[sq][r, :]
            blast = bcum[CHUNK - 1:CHUNK, :]
            q_in, k_in, k_out = _gla_rows(q_ref[sq, r, :], k_ref[sq, r, :], bcum, blast)
            bl_pieces = _pieces_by_row(jnp.broadcast_to(blast, (BF16_ROWS, GLA_DK_TOTAL)), row16)
            prep.append((q_in, k_in, k_out, bl_pieces, v_ref[sq, r, :]))
        pairs = [(c, h) for h in range(GLA_HEADS) for c in range(len(chunks))]
        qk = {(c, h): _dot_nt(prep[c][0][:, kls[h]], prep[c][1][:, kls[h]]) for c, h in pairs}
        sc = {ch: jnp.where(causal, qk[ch], 0.0).astype(BF16) for ch in pairs}
        intra = {}
        for c, h in pairs:
            _, _, k_out, bl_pieces, v = prep[c]
            dec = jnp.exp2(_dot_tn(bl_pieces[:, kls[h]], ones16))
            intra[c, h] = (_dot(sc[c, h], v[:, vls[h]]), _dot_tn(k_out[:, kls[h]], v[:, vls[h]]), dec)
        for c, r in enumerate(chunks):
            for h, (kl, vl) in enumerate(zip(kls, vls)):
                o_intra, ds, dec = intra[c, h]
                s_old = s_ref[sq, h]
                o_ref[sq, r, vl] = (o_intra + _dot(prep[c][0][:, kl], s_old.astype(BF16))).astype(BF16)
                s_ref[sq, h] = s_old * jnp.concatenate([dec, dec], axis=1) + ds
        prep = []
        for r in chunks:
            acx = acx_all[sq][r, :]
            alast = acx[CHUNK - 1:CHUNK, :]
            xa = xa_ref[sq, r, :]
            xs = xa[:, 0:SSM_INNER].astype(F32)
            xg = xs * dtx_all[sq][r, :]
            xw = (xg * jnp.exp2(alast - acx)).astype(BF16)
            seg = _segment_weights(acx, itile, cmask)
            prep.append((xa, xs, xg.astype(BF16), xw, jnp.exp2(acx), seg, jnp.exp2(alast)))
        pairs = [(c, g) for g in range(SSM_GROUPS) for c in range(len(chunks))]
        b_of = lambda c, g: prep[c][0][:, SSM_INNER + g * SSM_DSTATE:SSM_INNER + (g + 1) * SSM_DSTATE]
        c_of = lambda c, g: prep[c][0][:, SSM_INNER + (SSM_GROUPS + g) * SSM_DSTATE:
                                       SSM_INNER + (SSM_GROUPS + g + 1) * SSM_DSTATE]
        cbx = {(c, g): _dot_nt(c_of(c, g), jnp.concatenate([b_of(c, g)] * SSM_HPG, axis=0)) for c, g in pairs}
        w = {(c, g): (cbx[c, g] * prep[c][5][:, gls[g]]).astype(BF16) for c, g in pairs}
        intra = {}
        for c, g in pairs:
            xs, xgb, xw = prep[c][1], prep[c][2], prep[c][3]
            yd = []
            for pr in range(SSM_HPG // PAIR):
                lo = g * GROUP_LANES + pr * LANES
                xblk = jnp.concatenate([xgb[:, lo:lo + LANES]] * PAIR, axis=0) * bmask
                yd.append(_dot(w[c, g][:, pr * LANES:(pr + 1) * LANES], xblk))
            y_part = jnp.concatenate(yd, axis=1) + dskip_ref[:, gls[g]] * xs[:, gls[g]]
            intra[c, g] = (c_of(c, g), y_part, _dot_tn(b_of(c, g), xw[:, gls[g]]))
        for c, r in enumerate(chunks):
            eac, dec_row = prep[c][4], prep[c][6]
            for g, gl in enumerate(gls):
                cg, y_part, ds_t = intra[c, g]
                h_old = ht_ref[sq, :, gl]
                y_ref[sq, r, gl] = (y_part + _dot(cg, h_old.astype(BF16)) * eac[:, gl]).astype(BF16)
                ht_ref[sq, :, gl] = h_old * dec_row[:, gl] + ds_t

    @pl.when(j == nj - 1)
    def _():
        for sq in range(SEQS_PER_STEP):
            gla_ref[sq] = s_ref[sq]
            ssm_ref[sq] = ht_ref[sq].T


def _core_prompt(q, k, v, la, xa, dt, alog3, dskip_x, consts, *, nb, rows, l, depth, final_states):
    t = q.shape[0]
    nq = SEQS_PER_STEP
    half, length = nb // nq, t // nb
    nj = length // rows
    split = lambda a: a.reshape(nq, half, length, a.shape[-1])
    row = lambda width: pl.BlockSpec((nq, None, rows, width), lambda b, j: (0, b, j, 0))
    inputs = [split(q), split(k), split(v), split(la), split(xa), split(dt), alog3, dskip_x, *consts]
    in_specs = ([row(512), row(512), row(1024), row(512), row(CONV_DIM), row(LANES),
                 _const_spec((1, LANES)), _const_spec((1, SSM_INNER))]
                + [_const_spec(c.shape) for c in consts])
    aliases = {}
    if final_states is not None:
        aliases = {len(inputs): 2, len(inputs) + 1: 3}
        inputs += list(final_states)
        in_specs += [pl.BlockSpec(memory_space=pl.ANY)] * 2
    o, y, s_fin, h_fin = pl.pallas_call(
        functools.partial(_core_prompt_kernel, rows=rows, nj=nj),
        grid=(half, nj),
        in_specs=in_specs,
        out_specs=[row(GLA_DV_TOTAL), row(SSM_INNER),
                   pl.BlockSpec((None, nq, None, GLA_HEADS, GLA_DK, GLA_DV), lambda b, j: (l, 0, b, 0, 0, 0)),
                   pl.BlockSpec((None, nq, None, SSM_INNER, SSM_DSTATE), lambda b, j: (l, 0, b, 0, 0))],
        out_shape=[jax.ShapeDtypeStruct((nq, half, length, GLA_DV_TOTAL), BF16),
                   jax.ShapeDtypeStruct((nq, half, length, SSM_INNER), BF16),
                   jax.ShapeDtypeStruct((depth, nq, half, GLA_HEADS, GLA_DK, GLA_DV), F32),
                   jax.ShapeDtypeStruct((depth, nq, half, SSM_INNER, SSM_DSTATE), F32)],
        input_output_aliases=aliases,
        scratch_shapes=[pltpu.VMEM((nq, GLA_HEADS, GLA_DK, GLA_DV), F32),
                        pltpu.VMEM((nq, SSM_DSTATE, SSM_INNER), F32)],
        compiler_params=_params(n_arbitrary=2),
        name="core_prompt",
    )(*inputs)
    return o.reshape(t, GLA_DV_TOTAL), y.reshape(t, SSM_INNER), (s_fin, h_fin)


SEQ_PER_STEP = 4
SAMPLE_LEN = 4
SAMPLE_ROWS = SEQ_PER_STEP * SAMPLE_LEN


def _core_sample_kernel(q_ref, k_ref, v_ref, la_ref, xr_ref, c4_ref, dt_ref, s0_ref, h0_ref,
                        cw_ref, cb_ref, alog3_ref, dskip_ref,
                        bs3_ref, shmat_ref, e3_ref, es3_ref, itile_ref, cmask_ref, bmask_ref,
                        qmask_ref, smask_ref, pos_ref, *rest):
    o_ref, y_ref, s1_ref, h1_ref = rest[-4:]
    rs = SAMPLE_ROWS
    bs3 = bs3_ref[...]
    same_causal = bs3[0:rs, 0:rs] > 0.5
    ones16 = jnp.ones((BF16_ROWS, LANES), BF16)
    pos = pos_ref[...]
    xr = xr_ref[...]
    shifted = _dot(shmat_ref[...], jnp.concatenate(_split3(xr) + _split3(c4_ref[...]), axis=0))
    acc = cb_ref[...] + xr * cw_ref[CONV_W - 1:CONV_W, :]
    for i in range(CONV_W - 1):
        back = CONV_W - 1 - i
        acc = acc + shifted[(back - 1) * rs:back * rs, :] * cw_ref[i:i + 1, :]
    xa = _silu(acc)
    sums = _dot(bs3, _stack3(la_ref[...] * LOG2E))
    bcum, blast = sums[0:rs, :], sums[rs:2 * rs, :]
    q_in, k_in, k_out = _gla_rows(q_ref[...], k_ref[...], bcum, blast)
    bl_pieces = _pieces_by_row(blast, pos)
    v = v_ref[...]
    qmask = qmask_ref[...]
    seq_blk = lambda a: jnp.concatenate([a] * SEQ_PER_STEP, axis=1)
    for h in range(GLA_HEADS):
        kl = slice(h * GLA_DK, (h + 1) * GLA_DK)
        vl = slice(h * GLA_DV, (h + 1) * GLA_DV)
        sc = jnp.where(same_causal, _dot_nt(q_in[:, kl], k_in[:, kl]), 0.0).astype(BF16)
        s_old = s0_ref[:, h].reshape(SEQ_PER_STEP * GLA_DK, GLA_DV)
        o_h = _dot(sc, v[:, vl]) + _dot(seq_blk(q_in[:, kl]) * qmask, s_old.astype(BF16))
        o_ref[:, vl] = o_h.astype(BF16)
        ds = _dot_tn(seq_blk(k_out[:, kl]) * qmask, v[:, vl])
        dec = jnp.exp2(_dot_tn(seq_blk(bl_pieces[:, kl]) * qmask, ones16))
        s_new = s_old * jnp.concatenate([dec, dec], axis=1) + ds
        s1_ref[:, h] = s_new.reshape(SEQ_PER_STEP, GLA_DK, GLA_DV)
    dt3 = dt_ref[...]
    sums = _dot(bs3, _stack3(dt3 * (-jnp.exp(alog3_ref[...]) * LOG2E)))
    acum3, alast3 = sums[0:rs, :], sums[rs:2 * rs, :]
    p_ac = _pieces_by_lane(acum3)
    wide = _dot(jnp.concatenate([_pieces_by_lane(dt3), p_ac, _pieces_by_lane(alast3)], axis=0), e3_ref[...])
    dtx, acx, alx = wide[0:rs, :], wide[rs:2 * rs, :], wide[2 * rs:3 * rs, :]
    xs = xa[:, 0:SSM_INNER]
    xg = xs * dtx
    xgb = xg.astype(BF16)
    xw = xg * jnp.exp2(alx - acx)
    eac = jnp.exp2(acx)
    seg = _segment_weights(_dot(p_ac, es3_ref[...]), itile_ref[...], cmask_ref[...])
    al_pieces = _pieces_by_row(alx, pos)
    smask = smask_ref[...]
    sl = SSM_HPG * rs
    for g in range(SSM_GROUPS):
        gl = slice(g * GROUP_LANES, (g + 1) * GROUP_LANES)
        bg = xa[:, SSM_INNER + g * SSM_DSTATE:SSM_INNER + (g + 1) * SSM_DSTATE].astype(BF16)
        cg = xa[:, SSM_INNER + (SSM_GROUPS + g) * SSM_DSTATE:
                SSM_INNER + (SSM_GROUPS + g + 1) * SSM_DSTATE].astype(BF16)
        cbx = _dot_nt(cg, jnp.concatenate([bg] * SSM_HPG, axis=0))
        w = (cbx * seg[:, g * sl:(g + 1) * sl]).astype(BF16)
        xblk = jnp.concatenate([xgb[:, gl]] * SSM_HPG, axis=0) * bmask_ref[...]
        yd = _dot(w, xblk)
        h_old = [h0_ref[s, gl, :] for s in range(SEQ_PER_STEP)]
        yo = _dot_nt(seq_blk(cg) * qmask, jnp.concatenate([hs.astype(BF16) for hs in h_old], axis=1))
        y = yd + yo * eac[:, gl] + dskip_ref[:, gl] * xs[:, gl]
        y_ref[:, gl] = y.astype(BF16)
        ds = _dot_tn(seq_blk(xw[:, gl]).astype(BF16) * smask, bg)
        dec = jnp.exp2(_dot_tn(seq_blk(al_pieces[:, gl]) * smask, ones16))
        for s in range(SEQ_PER_STEP):
            sr = slice(s * GROUP_LANES, (s + 1) * GROUP_LANES)
            h1_ref[s, gl, :] = h_old[s] * dec[sr, :] + ds[sr, :]


def _core_sample(q, k, v, la, xr, c4, dt, s_all, h_all, cw, cb, alog3, dskip_x, consts, *, l, new_states):
    t = q.shape[0]
    rs = SAMPLE_ROWS
    nseq = t // SAMPLE_LEN
    row = lambda width: pl.BlockSpec((rs, width), lambda i: (i, 0))
    s_spec = pl.BlockSpec((None, SEQ_PER_STEP, GLA_HEADS, GLA_DK, GLA_DV), lambda i: (l, i, 0, 0, 0))
    h_spec = pl.BlockSpec((None, SEQ_PER_STEP, SSM_INNER, SSM_DSTATE), lambda i: (l, i, 0, 0))
    inputs = [q, k, v, la, xr, c4, dt, s_all, h_all, cw, cb, alog3, dskip_x, *consts]
    in_specs = ([row(512), row(512), row(1024), row(512), row(CONV_DIM), row(CONV_DIM), row(LANES),
                 s_spec, h_spec,
                 _const_spec((CONV_W, CONV_DIM)), _const_spec((1, CONV_DIM)),
                 _const_spec((1, LANES)), _const_spec((1, SSM_INNER))]
                + [_const_spec(c.shape) for c in consts])
    aliases = {}
    if new_states is not None:
        aliases = {len(inputs): 2, len(inputs) + 1: 3}
        inputs += list(new_states)
        in_specs += [pl.BlockSpec(memory_space=pl.ANY)] * 2
    return pl.pallas_call(
        _core_sample_kernel,
        grid=(nseq // SEQ_PER_STEP,),
        in_specs=in_specs,
        out_specs=[row(GLA_DV_TOTAL), row(SSM_INNER), s_spec, h_spec],
        out_shape=[jax.ShapeDtypeStruct((t, GLA_DV_TOTAL), BF16),
                   jax.ShapeDtypeStruct((t, SSM_INNER), BF16),
                   jax.ShapeDtypeStruct(s_all.shape, F32),
                   jax.ShapeDtypeStruct(h_all.shape, F32)],
        input_output_aliases=aliases,
        compiler_params=_params(),
        name="core_sample",
    )(*inputs)


def _merge_kernel(x_ref, o_ref, sg_ref, y_ref, sz_ref, ga_ref, gb_ref,
                  gn_ref, sn_ref, wpg_ref, wps_ref, wo_ref, out_ref):
    o = o_ref[...].astype(F32)
    parts = []
    for h in range(GLA_HEADS):
        vl = slice(h * GLA_DV, (h + 1) * GLA_DV)
        parts.append(_rms(o[:, vl]) * gn_ref[:, vl])
    og = (jnp.concatenate(parts, axis=1) * sg_ref[...].astype(F32)).astype(BF16)
    m_gla = ga_ref[...].astype(F32) * _dot(og, wpg_ref[...])
    yz = y_ref[...].astype(F32) * sz_ref[...].astype(F32)
    yn = (_rms(yz) * sn_ref[...]).astype(BF16)
    m = m_gla + gb_ref[...].astype(F32) * _dot(yn, wps_ref[...])
    out_ref[...] = x_ref[...] + _dot(m.astype(BF16), wo_ref[...])


def _merge(rows_p, rows_s, gn, sn, wpg, wps, wo, *, l, tm):
    body = lambda rows, sh, outs, scratch: _merge_kernel(*rows, *sh, outs[0])
    (xp,), (xs,) = _two_group_call(
        body, "merge", list(rows_p), list(rows_s), [gn, sn, wpg, wps, wo],
        [_const_spec((1, GLA_DV_TOTAL)), _const_spec((1, SSM_INNER)), _layer_spec(l, (GLA_DV_TOTAL, D_MODEL)),
         _layer_spec(l, (SSM_INNER, D_MODEL)), _layer_spec(l, (D_MODEL, D_MODEL))],
        [(D_MODEL, F32)], tm=tm)
    return xp, xs


def _expand_table(width):
    t = np.zeros((LANES, SSM_HEADS * width), np.float32)
    for k in range(PIECES):
        for h in range(SSM_HEADS):
            t[k * SSM_HEADS + h, h * width:(h + 1) * width] = 1.0
    return jnp.asarray(t, dtype=BF16)


def _segment_tables(rows, allowed):
    s_of_lane = np.arange(SSM_HPG * rows) % rows
    itile = (np.arange(rows)[:, None] == s_of_lane[None, :]).astype(np.float32)
    cmask = allowed[:, s_of_lane].astype(np.float32)
    return itile, cmask


def _head_block_mask(heads, rows):
    hr = np.arange(heads * rows) // rows
    hc = np.arange(heads * SSM_HEADDIM) // SSM_HEADDIM
    return jnp.asarray((hr[:, None] == hc[None, :]).astype(np.float32), dtype=BF16)


def _prompt_consts(rows):
    idx = np.arange(rows)
    btril = ((idx[:, None] // CHUNK == idx[None, :] // CHUNK) & (idx[None, :] <= idx[:, None])).astype(np.float32)
    tril = np.tril(np.ones((CHUNK, CHUNK), np.float32))
    itile, cmask = _segment_tables(CHUNK, tril)
    tile4 = lambda a: jnp.asarray(np.tile(a, (1, SSM_GROUPS)))
    return (jnp.asarray(np.tile(btril, (1, PIECES)), dtype=BF16), _expand_table(SSM_HEADDIM),
            tile4(itile), tile4(cmask), _head_block_mask(PAIR, CHUNK))


def _sample_consts():
    rs = SAMPLE_ROWS
    seq = np.arange(rs) // SAMPLE_LEN
    pos = np.arange(rs) % SAMPLE_LEN
    same = (seq[:, None] == seq[None, :]).astype(np.float32)
    btril = same * (pos[None, :] <= pos[:, None])
    bs3 = np.tile(np.concatenate([btril, same], axis=0), (1, PIECES))
    shmat = np.zeros((CONV_W - 1, rs, 2, rs), np.float32)
    for back in range(1, CONV_W):
        for r in range(rs):
            if pos[r] >= back:
                shmat[back - 1, r, 0, r - back] = 1.0
            else:
                shmat[back - 1, r, 1, seq[r] * SAMPLE_LEN + SAMPLE_LEN + pos[r] - back] = 1.0
    shmat = np.repeat(shmat[:, :, :, None, :], PIECES, axis=3).reshape((CONV_W - 1) * rs, 2 * PIECES * rs)
    itile, cmask = _segment_tables(rs, btril)
    tile4 = lambda a: jnp.asarray(np.tile(a, (1, SSM_GROUPS)))
    assert GLA_DK == SSM_DSTATE == LANES
    qmask = (seq[:, None] == (np.arange(SEQ_PER_STEP * LANES) // LANES)[None, :]).astype(np.float32)
    smask = (seq[:, None] == (np.arange(SEQ_PER_STEP * GROUP_LANES) // GROUP_LANES)[None, :]).astype(np.float32)
    return (jnp.asarray(bs3, dtype=BF16), jnp.asarray(shmat, dtype=BF16),
            _expand_table(SSM_HEADDIM), _expand_table(rs), tile4(itile), tile4(cmask),
            _head_block_mask(SSM_HPG, rs), jnp.asarray(qmask, dtype=BF16), jnp.asarray(smask, dtype=BF16),
            jnp.asarray(pos.reshape(rs, 1).astype(np.int32)))


def _prep_weights(p):
    win = jnp.swapaxes(p["w_in"], 1, 2).astype(BF16)
    piece = lambda i, j: jnp.swapaxes(p["w_in"][:, :, IN_OFFS[i]:IN_OFFS[j]], 1, 2).astype(BF16)
    lane_pad = lambda w: jnp.pad(w, ((0, 0), (0, 0), (0, LANES - w.shape[-1])))
    row_pad = lambda w: jnp.pad(w, ((0, 0), (0, LANES - w.shape[1]), (0, 0)))
    rep3 = lambda a: lane_pad(jnp.concatenate([a] * PIECES, axis=-1))
    return dict(
        win=win.reshape(-1, D_MODEL), wa1=row_pad(piece(4, 5)),
        wdt3=row_pad(jnp.concatenate([piece(7, 8)] * PIECES, axis=1)),
        wa2=jnp.pad(p["w_gla_a2"], ((0, 0), (0, LANES - GLA_GATE_RANK), (0, 0))).astype(BF16),
        dtb3=rep3(p["dt_bias"][:, None, :]), alog3=rep3(p["a_log"][:, None, :]),
        dskip_x=jnp.repeat(p["d_skip"], SSM_HEADDIM, axis=-1)[:, None, :],
        wpg=p["w_proj_gla"].astype(BF16), wps=p["w_proj_ssm"].astype(BF16), wo=p["w_out"].astype(BF16),
        gu1=p["w_ffn1_gu"].astype(BF16), d1=p["w_ffn1_down"].astype(BF16),
        gu2=p["w_ffn2_gu"].astype(BF16), d2=p["w_ffn2_down"].astype(BF16),
    )


PROMPT_TM = 512
FFN_TM = 512
IN_PROJ_TM = 1024
CORE_ROWS = 256


def _trunk(xp, xs, state_gla, state_ssm, state_conv, w, p, fw, nb):
    depth = p["w_in"].shape[0]
    ts = xs.shape[0]
    nseq = ts // SAMPLE_LEN
    row = lambda name, l: p[name][l].reshape(1, -1)
    pconsts, sconsts = _prompt_consts(CORE_ROWS), _sample_consts()
    h_all = state_ssm.reshape(depth, nseq, SSM_INNER, SSM_DSTATE)
    finals_p, finals_s = None, None
    convs_p, convs_s = [], []
    for l in range(depth):
        xp, xs = _ffn(xp, xs, row("norm_ffn1", l), w["gu1"], w["d1"], fw, l=l, tm=FFN_TM, final=False)
        (q, k, v, sg, ga, gb, la), (q_s, k_s, v_s, sg_s, ga_s, gb_s, la_s) = _gla_in(
            xp, xs, row("norm_mix", l), w["win"], w["wa1"], w["wa2"], row("b_gla_a2", l), l=l, tm=IN_PROJ_TM)
        sz, xa, dt, tail = _ssm_in_prompt(xp, row("norm_mix", l), w["win"], w["wdt3"], w["dtb3"][l],
                                          p["conv_w"][l], row("conv_b", l), l=l, nb=nb, tm=IN_PROJ_TM)
        o, y, finals_p = _core_prompt(q, k, v, la, xa, dt, w["alog3"][l], w["dskip_x"][l], pconsts,
                                      nb=nb, rows=CORE_ROWS, l=l, depth=depth, final_states=finals_p)
        convs_p.append(tail)
        sz_s, xr, dt_s = _ssm_in_sample(xs, row("norm_mix", l), w["win"], w["wdt3"], w["dtb3"][l], l=l, tm=ts)
        c4 = jnp.pad(state_conv[l], ((0, 0), (SAMPLE_LEN - (CONV_W - 1), 0), (0, 0))).reshape(ts, CONV_DIM)
        o_s, y_s, s1, h1 = _core_sample(q_s, k_s, v_s, la_s, xr, c4, dt_s, state_gla, h_all, p["conv_w"][l],
                                        row("conv_b", l), w["alog3"][l], w["dskip_x"][l], sconsts,
                                        l=l, new_states=finals_s)
        finals_s = (s1, h1)
        convs_s.append(xr.reshape(nseq, SAMPLE_LEN, CONV_DIM)[:, SAMPLE_LEN - (CONV_W - 1):])
        xp, xs = _merge((xp, o, sg, y, sz, ga, gb), (xs, o_s, sg_s, y_s, sz_s, ga_s, gb_s),
                        row("gla_norm", l), row("ssm_norm", l), w["wpg"], w["wps"], w["wo"], l=l, tm=PROMPT_TM)
        xp, xs = _ffn(xp, xs, row("norm_ffn2", l), w["gu2"], w["d2"], fw, l=l, tm=FFN_TM,
                      final=l == depth - 1)
    s_fin, h_fin = finals_p
    s1, h1 = finals_s
    return (xp, xs,
            s_fin.reshape(depth, nb, GLA_HEADS, GLA_DK, GLA_DV),
            h_fin.reshape(depth, nb, SSM_HEADS, SSM_HEADDIM, SSM_DSTATE), jnp.stack(convs_p),
            s1, h1.reshape(state_ssm.shape), jnp.stack(convs_s))


def kernel(x_prompt, x_sample, state_gla, state_ssm, state_conv, norm_ffn1, w_ffn1_gu, w_ffn1_down, norm_mix, w_in, w_gla_a2, b_gla_a2, gla_norm, conv_w, conv_b, dt_bias, a_log, d_skip, ssm_norm, w_proj_gla, w_proj_ssm, w_out, norm_ffn2, w_ffn2_gu, w_ffn2_down, norm_final):
    p = dict(norm_ffn1=norm_ffn1, w_ffn1_gu=w_ffn1_gu, w_ffn1_down=w_ffn1_down, norm_mix=norm_mix,
             w_in=w_in, w_gla_a2=w_gla_a2, b_gla_a2=b_gla_a2, gla_norm=gla_norm, conv_w=conv_w,
             conv_b=conv_b, dt_bias=dt_bias, a_log=a_log, d_skip=d_skip, ssm_norm=ssm_norm,
             w_proj_gla=w_proj_gla, w_proj_ssm=w_proj_ssm, w_out=w_out, norm_ffn2=norm_ffn2,
             w_ffn2_gu=w_ffn2_gu, w_ffn2_down=w_ffn2_down)
    w = _prep_weights(p)
    fw = norm_final.reshape(1, D_MODEL)
    bp, lp, _ = x_prompt.shape
    bs, ls, _ = x_sample.shape
    assert ls == SAMPLE_LEN and lp % CORE_ROWS == 0 and all(lp % tm == 0 for tm in (PROMPT_TM, FFN_TM, IN_PROJ_TM))
    assert bs % SEQ_PER_STEP == 0 and bp % SEQS_PER_STEP == 0
    yp, ys, gla_p, ssm_p, conv_p, gla_s, ssm_s, conv_s = _trunk(
        x_prompt.reshape(bp * lp, D_MODEL), x_sample.reshape(bs * ls, D_MODEL),
        state_gla, state_ssm, state_conv, w, p, fw, bp)
    return (yp.reshape(bp, lp, D_MODEL), ys.reshape(bs, ls, D_MODEL),
            gla_p, ssm_p, conv_p, gla_s, ssm_s, conv_s)
```

```python
import functools

import numpy as np
import jax
import jax.numpy as jnp
from jax import lax
from jax.experimental import pallas as pl
from jax.experimental.pallas import tpu as pltpu

F32 = jnp.float32
BF16 = jnp.bfloat16

D_MODEL = 1024
EPS = 1e-6
FFN_DIM = 2816
GLA_HEADS = 4
GLA_DK = 128
GLA_DV = 256
GLA_DK_TOTAL = GLA_HEADS * GLA_DK
GLA_DV_TOTAL = GLA_HEADS * GLA_DV
GLA_GATE_RANK = 16
GLA_TAU = 16.0
SSM_INNER = 2048
SSM_HEADDIM = 64
SSM_HEADS = 32
SSM_GROUPS = 4
SSM_HPG = SSM_HEADS // SSM_GROUPS
SSM_DSTATE = 128
CONV_W = 4
CONV_DIM = SSM_INNER + 2 * SSM_GROUPS * SSM_DSTATE
CHUNK = 64
IN_SPLITS = (GLA_DK_TOTAL, GLA_DK_TOTAL, GLA_DV_TOTAL, GLA_DV_TOTAL, GLA_GATE_RANK,
             SSM_INNER, CONV_DIM, SSM_HEADS, D_MODEL, D_MODEL)

LANES = 128
BF16_ROWS = 16
FFN_TILE = 256
GROUP_LANES = SSM_HPG * SSM_HEADDIM
PIECES = 3
LOG2E = 1.4426950408889634
VMEM_LIMIT = 56 * 1024 * 1024

NT_DIMS = (((1,), (1,)), ((), ()))
TN_DIMS = (((0,), (0,)), ((), ()))


def _dot(a, b):
    return jnp.dot(a, b, preferred_element_type=F32)


def _dot_nt(a, b):
    return lax.dot_general(a, b, NT_DIMS, preferred_element_type=F32)


def _dot_tn(a, b):
    return lax.dot_general(a, b, TN_DIMS, preferred_element_type=F32)


def _rms(x):
    return x * lax.rsqrt(jnp.mean(x * x, axis=-1, keepdims=True) + EPS)


def _sigmoid(x):
    return 0.5 * jnp.tanh(0.5 * x) + 0.5


def _silu(x):
    h = 0.5 * x
    return h * jnp.tanh(h) + h


def _softplus(x):
    return jnp.maximum(x, 0.0) + jnp.log1p(jnp.exp(-jnp.abs(x)))


def _split3(x):
    hi = x.astype(BF16)
    r1 = x - hi.astype(F32)
    mid = r1.astype(BF16)
    lo = (r1 - mid.astype(F32)).astype(BF16)
    return hi, mid, lo


def _stack3(x):
    return jnp.concatenate(_split3(x), axis=0)


def _pieces_by_lane(x):
    hi, mid, lo = (piece.astype(F32) for piece in _split3(x))
    lane = lax.broadcasted_iota(jnp.int32, x.shape, 1)
    picked = jnp.where(lane < SSM_HEADS, hi, jnp.where(lane < 2 * SSM_HEADS, mid, lo))
    return picked.astype(BF16)


def _pieces_by_row(x, piece_of_row):
    hi, mid, lo = (piece.astype(F32) for piece in _split3(x))
    idx = jnp.broadcast_to(piece_of_row, x.shape)
    picked = jnp.where(idx == 0, hi, jnp.where(idx == 1, mid, jnp.where(idx == 2, lo, 0.0)))
    return picked.astype(BF16)


def _params(n_arbitrary=1):
    return pltpu.CompilerParams(dimension_semantics=("arbitrary",) * n_arbitrary,
                                vmem_limit_bytes=VMEM_LIMIT)


def _const_spec(shape):
    nd = len(shape)
    return pl.BlockSpec(shape, lambda *_: (0,) * nd)


def _layer_spec(l, shape):
    return pl.BlockSpec((None,) + shape, lambda *_: (l, 0, 0), pipeline_mode=pl.Buffered(1))


IN_OFFS = tuple(int(o) for o in np.cumsum((0,) + IN_SPLITS))


def _in_rows_spec(l, first, last):
    start, rows = l * IN_OFFS[-1] + IN_OFFS[first], IN_OFFS[last] - IN_OFFS[first]
    return pl.BlockSpec((pl.Element(rows), pl.Element(D_MODEL)), lambda *_: (start, 0),
                        pipeline_mode=pl.Buffered(1))


def _two_group_call(body, name, rows_p, rows_s, shared, shared_specs, outs, *, tm, scratch_shapes=()):
    tp, ts = rows_p[0].shape[0], rows_s[0].shape[0]
    n_p = tp // tm
    assert tp % tm == 0 and len(rows_p) == len(rows_s)
    p_spec = lambda width: pl.BlockSpec((tm, width), lambda i: (jnp.minimum(i, n_p - 1), 0))
    s_spec = lambda width: pl.BlockSpec((ts, width), lambda i: (0, 0))
    n_in, n_sh, n_out = len(rows_p), len(shared), len(outs)

    def kernel(*refs):
        in_p, in_s = refs[:n_in], refs[n_in:2 * n_in]
        sh = refs[2 * n_in:2 * n_in + n_sh]
        out_p = refs[2 * n_in + n_sh:2 * n_in + n_sh + n_out]
        out_s = refs[2 * n_in + n_sh + n_out:2 * n_in + n_sh + 2 * n_out]
        scratch = refs[2 * n_in + n_sh + 2 * n_out:]
        i = pl.program_id(0)

        @pl.when(i < n_p)
        def _():
            body(in_p, sh, out_p, scratch)

        @pl.when(i == n_p)
        def _():
            body(in_s, sh, out_s, scratch)

    res = pl.pallas_call(
        kernel,
        grid=(n_p + 1,),
        in_specs=[p_spec(a.shape[1]) for a in rows_p] + [s_spec(a.shape[1]) for a in rows_s] + list(shared_specs),
        out_specs=[p_spec(width) for width, _ in outs] + [s_spec(width) for width, _ in outs],
        out_shape=[jax.ShapeDtypeStruct((tp, width), dt) for width, dt in outs]
                  + [jax.ShapeDtypeStruct((ts, width), dt) for width, dt in outs],
        scratch_shapes=list(scratch_shapes),
        compiler_params=_params(),
        name=name,
    )(*rows_p, *rows_s, *shared)
    return res[:n_out], res[n_out:]


def _ffn_kernel(x_ref, nw_ref, wgu_ref, wd_ref, fw_ref, o_ref, acc_ref, *, final):
    x = x_ref[...]
    rows = x.shape[0]
    xn = (_rms(x) * nw_ref[...]).astype(BF16)
    for c in range(FFN_DIM // FFN_TILE):
        lo = c * FFN_TILE
        g = _dot(xn, wgu_ref[:, lo:lo + FFN_TILE].astype(BF16))
        u = _dot(xn, wgu_ref[:, FFN_DIM + lo:FFN_DIM + lo + FFN_TILE].astype(BF16))
        a = (_silu(g) * u).astype(BF16)
        part = _dot(a, wd_ref[lo:lo + FFN_TILE, :].astype(BF16))
        if c == 0:
            acc_ref[0:rows, :] = part
        else:
            acc_ref[0:rows, :] += part
    y = x + 0.5 * acc_ref[0:rows, :]
    if final:
        y = _rms(y) * fw_ref[...]
    o_ref[...] = y


def _ffn(xp, xs, nw, wgu, wd, fw, *, l, tm, final):
    body = lambda rows, sh, outs, scratch: _ffn_kernel(rows[0], *sh, outs[0], scratch[0], final=final)
    (yp,), (ys,) = _two_group_call(
        body, "ffn_final" if final else "ffn", [xp], [xs], [nw, wgu, wd, fw],
        [_const_spec((1, D_MODEL)), _layer_spec(l, (D_MODEL, 2 * FFN_DIM)), _layer_spec(l, (FFN_DIM, D_MODEL)),
         _const_spec((1, D_MODEL))],
        [(D_MODEL, F32)], tm=tm, scratch_shapes=[pltpu.VMEM((max(tm, xs.shape[0]), D_MODEL), F32)])
    return yp, ys


def _gla_in_kernel(x_ref, nw_ref, wqkvg_ref, wgate_ref, wa1_ref, wa2_ref, ba2_ref,
                   q_ref, k_ref, v_ref, sg_ref, ga_ref, gb_ref, la_ref):
    u = (_rms(x_ref[...]) * nw_ref[...]).astype(BF16)
    q_ref[...] = _dot_nt(u, wqkvg_ref[0:512, :]).astype(BF16)
    k_ref[...] = _dot_nt(u, wqkvg_ref[512:1024, :]).astype(BF16)
    v_ref[...] = _dot_nt(u, wqkvg_ref[1024:2048, :]).astype(BF16)
    sg_ref[...] = _silu(_dot_nt(u, wqkvg_ref[2048:3072, :])).astype(BF16)
    ga_ref[...] = _sigmoid(_dot_nt(u, wgate_ref[0:1024, :])).astype(BF16)
    gb_ref[...] = _sigmoid(_dot_nt(u, wgate_ref[1024:2048, :])).astype(BF16)
    a_lr = _dot_nt(u, wa1_ref[...]).astype(BF16)
    z = _dot(a_lr, wa2_ref[...]) + ba2_ref[...]
    la_ref[...] = -_softplus(-z) / GLA_TAU


def _gla_in(xp, xs, nw, win, wa1, wa2, ba2, *, l, tm):
    body = lambda rows, sh, outs, scratch: _gla_in_kernel(rows[0], *sh, *outs)
    return _two_group_call(
        body, "gla_in", [xp], [xs], [nw, win, win, wa1, wa2, ba2],
        [_const_spec((1, D_MODEL)), _in_rows_spec(l, 0, 4), _in_rows_spec(l, 8, 10),
         _layer_spec(l, (LANES, D_MODEL)), _layer_spec(l, (LANES, GLA_DK_TOTAL)), _const_spec((1, GLA_DK_TOTAL))],
        [(512, BF16), (512, BF16), (1024, BF16), (1024, BF16), (1024, BF16), (1024, BF16), (512, F32)], tm=tm)


ZX_COLS = SSM_INNER + CONV_DIM


CONV_COLS = 512
SUBLANES = 8


def _ssm_in_prompt_kernel(x_ref, nw_ref, wzx_ref, wdt_ref, dtb_ref, cw_ref, cb_ref,
                          sz_ref, xa_ref, dt_ref, tail_ref, carry_ref, *, tm):
    j = pl.program_id(1)

    @pl.when(j == 0)
    def _():
        carry_ref[...] = jnp.zeros(carry_ref.shape, F32)

    u = (_rms(x_ref[...]) * nw_ref[...]).astype(BF16)
    sz_ref[...] = _silu(_dot_nt(u, wzx_ref[0:SSM_INNER, :])).astype(BF16)
    dt_ref[...] = _softplus(_dot_nt(u, wdt_ref[...]) + dtb_ref[...])
    tiles = (tm // SUBLANES, SUBLANES, CONV_COLS)
    sub = lax.broadcasted_iota(jnp.int32, tiles, 1)
    for cc in range(CONV_DIM // CONV_COLS):
        cols = slice(cc * CONV_COLS, (cc + 1) * CONV_COLS)
        xr = _dot_nt(u, wzx_ref[SSM_INNER + cc * CONV_COLS:SSM_INNER + (cc + 1) * CONV_COLS, :])
        x3 = xr.reshape(tiles)
        p3 = jnp.concatenate([carry_ref[:, cols], xr[0:tm - SUBLANES, :]], axis=0).reshape(tiles)
        acc = cb_ref[:, cols] + xr * cw_ref[CONV_W - 1:CONV_W, cols]
        for i in range(CONV_W - 1):
            back = CONV_W - 1 - i
            sh = pltpu.roll(jnp.where(sub >= SUBLANES - back, p3, x3), back, axis=1)
            acc = acc + sh.reshape(tm, CONV_COLS) * cw_ref[i:i + 1, cols]
        xa_ref[:, cols] = _silu(acc).astype(BF16)
        carry_ref[:, cols] = xr[tm - SUBLANES:tm, :]
    tail_ref[0] = carry_ref[SUBLANES - (CONV_W - 1):SUBLANES, :]


def _ssm_in_prompt(x, nw, wzx, wdt, dtb, cw, cb, *, l, nb, tm):
    t = x.shape[0]
    nj = t // nb // tm
    row = lambda width: pl.BlockSpec((tm, width), lambda b, j: (b * nj + j, 0))
    return pl.pallas_call(
        functools.partial(_ssm_in_prompt_kernel, tm=tm),
        grid=(nb, nj),
        in_specs=[row(D_MODEL), _const_spec((1, D_MODEL)),
                  _in_rows_spec(l, 5, 7), _layer_spec(l, (LANES, D_MODEL)),
                  _const_spec((1, LANES)), _const_spec((CONV_W, CONV_DIM)), _const_spec((1, CONV_DIM))],
        out_specs=[row(SSM_INNER), row(CONV_DIM), row(LANES),
                   pl.BlockSpec((1, CONV_W - 1, CONV_DIM), lambda b, j: (b, 0, 0))],
        out_shape=[jax.ShapeDtypeStruct((t, SSM_INNER), BF16),
                   jax.ShapeDtypeStruct((t, CONV_DIM), BF16),
                   jax.ShapeDtypeStruct((t, LANES), F32),
                   jax.ShapeDtypeStruct((nb, CONV_W - 1, CONV_DIM), F32)],
        scratch_shapes=[pltpu.VMEM((SUBLANES, CONV_DIM), F32)],
        compiler_params=_params(n_arbitrary=2),
        name="ssm_in_prompt",
    )(x, nw, wzx, wdt, dtb, cw, cb)


def _ssm_in_sample_kernel(x_ref, nw_ref, wzx_ref, wdt_ref, dtb_ref, sz_ref, xr_ref, dt_ref):
    u = (_rms(x_ref[...]) * nw_ref[...]).astype(BF16)
    sz_ref[...] = _silu(_dot_nt(u, wzx_ref[0:SSM_INNER, :])).astype(BF16)
    xr_ref[...] = _dot_nt(u, wzx_ref[SSM_INNER:ZX_COLS, :])
    dt_ref[...] = _softplus(_dot_nt(u, wdt_ref[...]) + dtb_ref[...])


def _ssm_in_sample(x, nw, wzx, wdt, dtb, *, l, tm):
    t = x.shape[0]
    row = lambda width: pl.BlockSpec((tm, width), lambda i: (i, 0))
    return pl.pallas_call(
        _ssm_in_sample_kernel,
        grid=(t // tm,),
        in_specs=[row(D_MODEL), _const_spec((1, D_MODEL)),
                  _in_rows_spec(l, 5, 7), _layer_spec(l, (LANES, D_MODEL)),
                  _const_spec((1, LANES))],
        out_specs=[row(SSM_INNER), row(CONV_DIM), row(LANES)],
        out_shape=[jax.ShapeDtypeStruct((t, SSM_INNER), BF16),
                   jax.ShapeDtypeStruct((t, CONV_DIM), F32),
                   jax.ShapeDtypeStruct((t, LANES), F32)],
        compiler_params=_params(),
        name="ssm_in_sample",
    )(x, nw, wzx, wdt, dtb)


def _gla_rows(q, k, bcum, blast):
    qf = q.astype(F32) * (GLA_DK ** -0.5)
    kf = k.astype(F32)
    q_in = (qf * jnp.exp2(bcum)).astype(BF16)
    k_in = (kf * jnp.exp2(-bcum)).astype(BF16)
    k_out = (kf * jnp.exp2(blast - bcum)).astype(BF16)
    return q_in, k_in, k_out


def _segment_weights(ac_s, itile, cmask):
    arow = jnp.sum(ac_s * itile, axis=0, keepdims=True)
    return jnp.where(cmask > 0.5, jnp.exp2(ac_s - arow), 0.0)


PAIR = 2


SEQS_PER_STEP = 1
CHUNK_BATCH = 1


def _core_prompt_kernel(q_ref, k_ref, v_ref, la_ref, xa_ref, dt_ref, alog3_ref, dskip_ref,
                        btril3_ref, e3_ref, itile_ref, cmask_ref, bmask_ref, *rest, rows, nj):
    o_ref, y_ref, gla_ref, ssm_ref, s_ref, ht_ref = rest[-6:]
    j = pl.program_id(1)

    @pl.when(j == 0)
    def _():
        s_ref[...] = jnp.zeros(s_ref.shape, F32)
        ht_ref[...] = jnp.zeros(ht_ref.shape, F32)

    btril3 = btril3_ref[...]
    causal = btril3[0:CHUNK, 0:CHUNK] > 0.5
    ones16 = jnp.ones((BF16_ROWS, LANES), BF16)
    row16 = lax.broadcasted_iota(jnp.int32, (BF16_ROWS, 1), 0)
    itile = itile_ref[...]
    cmask = cmask_ref[...]
    bmask = bmask_ref[...]
    e3 = e3_ref[...]
    a3 = -jnp.exp(alog3_ref[...]) * LOG2E

    bcum_all, dtx_all, acx_all = [], [], []
    for sq in range(SEQS_PER_STEP):
        bcum_all.append(_dot(btril3, _stack3(la_ref[sq] * LOG2E)))
        dt3 = dt_ref[sq]
        acum3 = _dot(btril3, _stack3(dt3 * a3))
        dtx_all.append(_dot(_pieces_by_lane(dt3), e3))
        acx_all.append(_dot(_pieces_by_lane(acum3), e3))

    all_chunks = [slice(c * CHUNK, (c + 1) * CHUNK) for c in range(rows // CHUNK)]
    batches = [all_chunks[b:b + CHUNK_BATCH] for b in range(0, len(all_chunks), CHUNK_BATCH)]
    kls = [slice(h * GLA_DK, (h + 1) * GLA_DK) for h in range(GLA_HEADS)]
    vls = [slice(h * GLA_DV, (h + 1) * GLA_DV) for h in range(GLA_HEADS)]
    gls = [slice(g * GROUP_LANES, (g + 1) * GROUP_LANES) for g in range(SSM_GROUPS)]

    for chunks, sq in [(batch, sq) for batch in batches for sq in range(SEQS_PER_STEP)]:
        prep = []
        for r in chunks:
            bcum = bcum_all[sq][r, :]
            blast = bcum[CHUNK - 1:CHUNK, :]
            q_in, k_in, k_out = _gla_rows(q_ref[sq, r, :], k_ref[sq, r, :], bcum, blast)
            bl_pieces = _pieces_by_row(jnp.broadcast_to(blast, (BF16_ROWS, GLA_DK_TOTAL)), row16)
            prep.append((q_in, k_in, k_out, bl_pieces, v_ref[sq, r, :]))
        pairs = [(c, h) for h in range(GLA_HEADS) for c in range(len(chunks))]
        qk = {(c, h): _dot_nt(prep[c][0][:, kls[h]], prep[c][1][:, kls[h]]) for c, h in pairs}
        sc = {ch: jnp.where(causal, qk[ch], 0.0).astype(BF16) for ch in pairs}
        intra = {}
        for c, h in pairs:
            _, _, k_out, bl_pieces, v = prep[c]
            dec = jnp.exp2(_dot_tn(bl_pieces[:, kls[h]], ones16))
            intra[c, h] = (_dot(sc[c, h], v[:, vls[h]]), _dot_tn(k_out[:, kls[h]], v[:, vls[h]]), dec)
        for c, r in enumerate(chunks):
            for h, (kl, vl) in enumerate(zip(kls, vls)):
                o_intra, ds, dec = intra[c, h]
                s_old = s_ref[sq, h]
                o_ref[sq, r, vl] = (o_intra + _dot(prep[c][0][:, kl], s_old.astype(BF16))).astype(BF16)
                s_ref[sq, h] = s_old * jnp.concatenate([dec, dec], axis=1) + ds
        prep = []
        for r in chunks:
            acx = acx_all[sq][r, :]
            alast = acx[CHUNK - 1:CHUNK, :]
            xa = xa_ref[sq, r, :]
            xs = xa[:, 0:SSM_INNER].astype(F32)
            xg = xs * dtx_all[sq][r, :]
            xw = (xg * jnp.exp2(alast - acx)).astype(BF16)
            seg = _segment_weights(acx, itile, cmask)
            prep.append((xa, xs, xg.astype(BF16), xw, jnp.exp2(acx), seg, jnp.exp2(alast)))
        pairs = [(c, g) for g in range(SSM_GROUPS) for c in range(len(chunks))]
        b_of = lambda c, g: prep[c][0][:, SSM_INNER + g * SSM_DSTATE:SSM_INNER + (g + 1) * SSM_DSTATE]
        c_of = lambda c, g: prep[c][0][:, SSM_INNER + (SSM_GROUPS + g) * SSM_DSTATE:
                                       SSM_INNER + (SSM_GROUPS + g + 1) * SSM_DSTATE]
        cbx = {(c, g): _dot_nt(c_of(c, g), jnp.concatenate([b_of(c, g)] * SSM_HPG, axis=0)) for c, g in pairs}
        w = {(c, g): (cbx[c, g] * prep[c][5][:, gls[g]]).astype(BF16) for c, g in pairs}
        intra = {}
        for c, g in pairs:
            xs, xgb, xw = prep[c][1], prep[c][2], prep[c][3]
            yd = []
            for pr in range(SSM_HPG // PAIR):
                lo = g * GROUP_LANES + pr * LANES
                xblk = jnp.concatenate([xgb[:, lo:lo + LANES]] * PAIR, axis=0) * bmask
                yd.append(_dot(w[c, g][:, pr * LANES:(pr + 1) * LANES], xblk))
            y_part = jnp.concatenate(yd, axis=1) + dskip_ref[:, gls[g]] * xs[:, gls[g]]
            intra[c, g] = (c_of(c, g), y_part, _dot_tn(b_of(c, g), xw[:, gls[g]]))
        for c, r in enumerate(chunks):
            eac, dec_row = prep[c][4], prep[c][6]
            for g, gl in enumerate(gls):
                cg, y_part, ds_t = intra[c, g]
                h_old = ht_ref[sq, :, gl]
                y_ref[sq, r, gl] = (y_part + _dot(cg, h_old.astype(BF16)) * eac[:, gl]).astype(BF16)
                ht_ref[sq, :, gl] = h_old * dec_row[:, gl] + ds_t

    @pl.when(j == nj - 1)
    def _():
        for sq in range(SEQS_PER_STEP):
            gla_ref[sq] = s_ref[sq]
            ssm_ref[sq] = ht_ref[sq].T


def _core_prompt(q, k, v, la, xa, dt, alog3, dskip_x, consts, *, nb, rows, l, depth, final_states):
    t = q.shape[0]
    nq = SEQS_PER_STEP
    half, length = nb // nq, t // nb
    nj = length // rows
    split = lambda a: a.reshape(nq, half, length, a.shape[-1])
    row = lambda width: pl.BlockSpec((nq, None, rows, width), lambda b, j: (0, b, j, 0))
    inputs = [split(q), split(k), split(v), split(la), split(xa), split(dt), alog3, dskip_x, *consts]
    in_specs = ([row(512), row(512), row(1024), row(512), row(CONV_DIM), row(LANES),
                 _const_spec((1, LANES)), _const_spec((1, SSM_INNER))]
                + [_const_spec(c.shape) for c in consts])
    aliases = {}
    if final_states is not None:
        aliases = {len(inputs): 2, len(inputs) + 1: 3}
        inputs += list(final_states)
        in_specs += [pl.BlockSpec(memory_space=pl.ANY)] * 2
    o, y, s_fin, h_fin = pl.pallas_call(
        functools.partial(_core_prompt_kernel, rows=rows, nj=nj),
        grid=(half, nj),
        in_specs=in_specs,
        out_specs=[row(GLA_DV_TOTAL), row(SSM_INNER),
                   pl.BlockSpec((None, nq, None, GLA_HEADS, GLA_DK, GLA_DV), lambda b, j: (l, 0, b, 0, 0, 0)),
                   pl.BlockSpec((None, nq, None, SSM_INNER, SSM_DSTATE), lambda b, j: (l, 0, b, 0, 0))],
        out_shape=[jax.ShapeDtypeStruct((nq, half, length, GLA_DV_TOTAL), BF16),
                   jax.ShapeDtypeStruct((nq, half, length, SSM_INNER), BF16),
                   jax.ShapeDtypeStruct((depth, nq, half, GLA_HEADS, GLA_DK, GLA_DV), F32),
                   jax.ShapeDtypeStruct((depth, nq, half, SSM_INNER, SSM_DSTATE), F32)],
        input_output_aliases=aliases,
        scratch_shapes=[pltpu.VMEM((nq, GLA_HEADS, GLA_DK, GLA_DV), F32),
                        pltpu.VMEM((nq, SSM_DSTATE, SSM_INNER), F32)],
        compiler_params=_params(n_arbitrary=2),
        name="core_prompt",
    )(*inputs)
    return o.reshape(t, GLA_DV_TOTAL), y.reshape(t, SSM_INNER), (s_fin, h_fin)


SEQ_PER_STEP = 4
SAMPLE_LEN = 4
SAMPLE_ROWS = SEQ_PER_STEP * SAMPLE_LEN


def _core_sample_kernel(q_ref, k_ref, v_ref, la_ref, xr_ref, c4_ref, dt_ref, s0_ref, h0_ref,
                        cw_ref, cb_ref, alog3_ref, dskip_ref,
                        bs3_ref, shmat_ref, e3_ref, es3_ref, itile_ref, cmask_ref, bmask_ref,
                        qmask_ref, smask_ref, pos_ref, *rest):
    o_ref, y_ref, s1_ref, h1_ref = rest[-4:]
    rs = SAMPLE_ROWS
    bs3 = bs3_ref[...]
    same_causal = bs3[0:rs, 0:rs] > 0.5
    ones16 = jnp.ones((BF16_ROWS, LANES), BF16)
    pos = pos_ref[...]
    xr = xr_ref[...]
    shifted = _dot(shmat_ref[...], jnp.concatenate(_split3(xr) + _split3(c4_ref[...]), axis=0))
    acc = cb_ref[...] + xr * cw_ref[CONV_W - 1:CONV_W, :]
    for i in range(CONV_W - 1):
        back = CONV_W - 1 - i
        acc = acc + shifted[(back - 1) * rs:back * rs, :] * cw_ref[i:i + 1, :]
    xa = _silu(acc)
    sums = _dot(bs3, _stack3(la_ref[...] * LOG2E))
    bcum, blast = sums[0:rs, :], sums[rs:2 * rs, :]
    q_in, k_in, k_out = _gla_rows(q_ref[...], k_ref[...], bcum, blast)
    bl_pieces = _pieces_by_row(blast, pos)
    v = v_ref[...]
    qmask = qmask_ref[...]
    seq_blk = lambda a: jnp.concatenate([a] * SEQ_PER_STEP, axis=1)
    for h in range(GLA_HEADS):
        kl = slice(h * GLA_DK, (h + 1) * GLA_DK)
        vl = slice(h * GLA_DV, (h + 1) * GLA_DV)
        sc = jnp.where(same_causal, _dot_nt(q_in[:, kl], k_in[:, kl]), 0.0).astype(BF16)
        s_old = s0_ref[:, h].reshape(SEQ_PER_STEP * GLA_DK, GLA_DV)
        o_h = _dot(sc, v[:, vl]) + _dot(seq_blk(q_in[:, kl]) * qmask, s_old.astype(BF16))
        o_ref[:, vl] = o_h.astype(BF16)
        ds = _dot_tn(seq_blk(k_out[:, kl]) * qmask, v[:, vl])
        dec = jnp.exp2(_dot_tn(seq_blk(bl_pieces[:, kl]) * qmask, ones16))
        s_new = s_old * jnp.concatenate([dec, dec], axis=1) + ds
        s1_ref[:, h] = s_new.reshape(SEQ_PER_STEP, GLA_DK, GLA_DV)
    dt3 = dt_ref[...]
    sums = _dot(bs3, _stack3(dt3 * (-jnp.exp(alog3_ref[...]) * LOG2E)))
    acum3, alast3 = sums[0:rs, :], sums[rs:2 * rs, :]
    p_ac = _pieces_by_lane(acum3)
    wide = _dot(jnp.concatenate([_pieces_by_lane(dt3), p_ac, _pieces_by_lane(alast3)], axis=0), e3_ref[...])
    dtx, acx, alx = wide[0:rs, :], wide[rs:2 * rs, :], wide[2 * rs:3 * rs, :]
    xs = xa[:, 0:SSM_INNER]
    xg = xs * dtx
    xgb = xg.astype(BF16)
    xw = xg * jnp.exp2(alx - acx)
    eac = jnp.exp2(acx)
    seg = _segment_weights(_dot(p_ac, es3_ref[...]), itile_ref[...], cmask_ref[...])
    al_pieces = _pieces_by_row(alx, pos)
    smask = smask_ref[...]
    sl = SSM_HPG * rs
    for g in range(SSM_GROUPS):
        gl = slice(g * GROUP_LANES, (g + 1) * GROUP_LANES)
        bg = xa[:, SSM_INNER + g * SSM_DSTATE:SSM_INNER + (g + 1) * SSM_DSTATE].astype(BF16)
        cg = xa[:, SSM_INNER + (SSM_GROUPS + g) * SSM_DSTATE:
                SSM_INNER + (SSM_GROUPS + g + 1) * SSM_DSTATE].astype(BF16)
        cbx = _dot_nt(cg, jnp.concatenate([bg] * SSM_HPG, axis=0))
        w = (cbx * seg[:, g * sl:(g + 1) * sl]).astype(BF16)
        xblk = jnp.concatenate([xgb[:, gl]] * SSM_HPG, axis=0) * bmask_ref[...]
        yd = _dot(w, xblk)
        h_old = [h0_ref[s, gl, :] for s in range(SEQ_PER_STEP)]
        yo = _dot_nt(seq_blk(cg) * qmask, jnp.concatenate([hs.astype(BF16) for hs in h_old], axis=1))
        y = yd + yo * eac[:, gl] + dskip_ref[:, gl] * xs[:, gl]
        y_ref[:, gl] = y.astype(BF16)
        ds = _dot_tn(seq_blk(xw[:, gl]).astype(BF16) * smask, bg)
        dec = jnp.exp2(_dot_tn(seq_blk(al_pieces[:, gl]) * smask, ones16))
        for s in range(SEQ_PER_STEP):
            sr = slice(s * GROUP_LANES, (s + 1) * GROUP_LANES)
            h1_ref[s, gl, :] = h_old[s] * dec[sr, :] + ds[sr, :]


def _core_sample(q, k, v, la, xr, c4, dt, s_all, h_all, cw, cb, alog3, dskip_x, consts, *, l, new_states):
    t = q.shape[0]
    rs = SAMPLE_ROWS
    nseq = t // SAMPLE_LEN
    row = lambda width: pl.BlockSpec((rs, width), lambda i: (i, 0))
    s_spec = pl.BlockSpec((None, SEQ_PER_STEP, GLA_HEADS, GLA_DK, GLA_DV), lambda i: (l, i, 0, 0, 0))
    h_spec = pl.BlockSpec((None, SEQ_PER_STEP, SSM_INNER, SSM_DSTATE), lambda i: (l, i, 0, 0))
    inputs = [q, k, v, la, xr, c4, dt, s_all, h_all, cw, cb, alog3, dskip_x, *consts]
    in_specs = ([row(512), row(512), row(1024), row(512), row(CONV_DIM), row(CONV_DIM), row(LANES),
                 s_spec, h_spec,
                 _const_spec((CONV_W, CONV_DIM)), _const_spec((1, CONV_DIM)),
                 _const_spec((1, LANES)), _const_spec((1, SSM_INNER))]
                + [_const_spec(c.shape) for c in consts])
    aliases = {}
    if new_states is not None:
        aliases = {len(inputs): 2, len(inputs) + 1: 3}
        inputs += list(new_states)
        in_specs += [pl.BlockSpec(memory_space=pl.ANY)] * 2
    return pl.pallas_call(
        _core_sample_kernel,
        grid=(nseq // SEQ_PER_STEP,),
        in_specs=in_specs,
        out_specs=[row(GLA_DV_TOTAL), row(SSM_INNER), s_spec, h_spec],
        out_shape=[jax.ShapeDtypeStruct((t, GLA_DV_TOTAL), BF16),
                   jax.ShapeDtypeStruct((t, SSM_INNER), BF16),
                   jax.ShapeDtypeStruct(s_all.shape, F32),
                   jax.ShapeDtypeStruct(h_all.shape, F32)],
        input_output_aliases=aliases,
        compiler_params=_params(),
        name="core_sample",
    )(*inputs)


def _merge_kernel(x_ref, o_ref, sg_ref, y_ref, sz_ref, ga_ref, gb_ref,
                  gn_ref, sn_ref, wpg_ref, wps_ref, wo_ref, out_ref):
    o = o_ref[...].astype(F32)
    parts = []
    for h in range(GLA_HEADS):
        vl = slice(h * GLA_DV, (h + 1) * GLA_DV)
        parts.append(_rms(o[:, vl]) * gn_ref[:, vl])
    og = (jnp.concatenate(parts, axis=1) * sg_ref[...].astype(F32)).astype(BF16)
    m_gla = ga_ref[...].astype(F32) * _dot(og, wpg_ref[...])
    yz = y_ref[...].astype(F32) * sz_ref[...].astype(F32)
    yn = (_rms(yz) * sn_ref[...]).astype(BF16)
    m = m_gla + gb_ref[...].astype(F32) * _dot(yn, wps_ref[...])
    out_ref[...] = x_ref[...] + _dot(m.astype(BF16), wo_ref[...])


def _merge(rows_p, rows_s, gn, sn, wpg, wps, wo, *, l, tm):
    body = lambda rows, sh, outs, scratch: _merge_kernel(*rows, *sh, outs[0])
    (xp,), (xs,) = _two_group_call(
        body, "merge", list(rows_p), list(rows_s), [gn, sn, wpg, wps, wo],
        [_const_spec((1, GLA_DV_TOTAL)), _const_spec((1, SSM_INNER)), _layer_spec(l, (GLA_DV_TOTAL, D_MODEL)),
         _layer_spec(l, (SSM_INNER, D_MODEL)), _layer_spec(l, (D_MODEL, D_MODEL))],
        [(D_MODEL, F32)], tm=tm)
    return xp, xs


def _expand_table(width):
    t = np.zeros((LANES, SSM_HEADS * width), np.float32)
    for k in range(PIECES):
        for h in range(SSM_HEADS):
            t[k * SSM_HEADS + h, h * width:(h + 1) * width] = 1.0
    return jnp.asarray(t, dtype=BF16)


def _segment_tables(rows, allowed):
    s_of_lane = np.arange(SSM_HPG * rows) % rows
    itile = (np.arange(rows)[:, None] == s_of_lane[None, :]).astype(np.float32)
    cmask = allowed[:, s_of_lane].astype(np.float32)
    return itile, cmask


def _head_block_mask(heads, rows):
    hr = np.arange(heads * rows) // rows
    hc = np.arange(heads * SSM_HEADDIM) // SSM_HEADDIM
    return jnp.asarray((hr[:, None] == hc[None, :]).astype(np.float32), dtype=BF16)


def _prompt_consts(rows):
    idx = np.arange(rows)
    btril = ((idx[:, None] // CHUNK == idx[None, :] // CHUNK) & (idx[None, :] <= idx[:, None])).astype(np.float32)
    tril = np.tril(np.ones((CHUNK, CHUNK), np.float32))
    itile, cmask = _segment_tables(CHUNK, tril)
    tile4 = lambda a: jnp.asarray(np.tile(a, (1, SSM_GROUPS)))
    return (jnp.asarray(np.tile(btril, (1, PIECES)), dtype=BF16), _expand_table(SSM_HEADDIM),
            tile4(itile), tile4(cmask), _head_block_mask(PAIR, CHUNK))


def _sample_consts():
    rs = SAMPLE_ROWS
    seq = np.arange(rs) // SAMPLE_LEN
    pos = np.arange(rs) % SAMPLE_LEN
    same = (seq[:, None] == seq[None, :]).astype(np.float32)
    btril = same * (pos[None, :] <= pos[:, None])
    bs3 = np.tile(np.concatenate([btril, same], axis=0), (1, PIECES))
    shmat = np.zeros((CONV_W - 1, rs, 2, rs), np.float32)
    for back in range(1, CONV_W):
        for r in range(rs):
            if pos[r] >= back:
                shmat[back - 1, r, 0, r - back] = 1.0
            else:
                shmat[back - 1, r, 1, seq[r] * SAMPLE_LEN + SAMPLE_LEN + pos[r] - back] = 1.0
    shmat = np.repeat(shmat[:, :, :, None, :], PIECES, axis=3).reshape((CONV_W - 1) * rs, 2 * PIECES * rs)
    itile, cmask = _segment_tables(rs, btril)
    tile4 = lambda a: jnp.asarray(np.tile(a, (1, SSM_GROUPS)))
    assert GLA_DK == SSM_DSTATE == LANES
    qmask = (seq[:, None] == (np.arange(SEQ_PER_STEP * LANES) // LANES)[None, :]).astype(np.float32)
    smask = (seq[:, None] == (np.arange(SEQ_PER_STEP * GROUP_LANES) // GROUP_LANES)[None, :]).astype(np.float32)
    return (jnp.asarray(bs3, dtype=BF16), jnp.asarray(shmat, dtype=BF16),
            _expand_table(SSM_HEADDIM), _expand_table(rs), tile4(itile), tile4(cmask),
            _head_block_mask(SSM_HPG, rs), jnp.asarray(qmask, dtype=BF16), jnp.asarray(smask, dtype=BF16),
            jnp.asarray(pos.reshape(rs, 1).astype(np.int32)))


def _prep_weights(p):
    win = jnp.swapaxes(p["w_in"], 1, 2).astype(BF16)
    piece = lambda i, j: jnp.swapaxes(p["w_in"][:, :, IN_OFFS[i]:IN_OFFS[j]], 1, 2).astype(BF16)
    lane_pad = lambda w: jnp.pad(w, ((0, 0), (0, 0), (0, LANES - w.shape[-1])))
    row_pad = lambda w: jnp.pad(w, ((0, 0), (0, LANES - w.shape[1]), (0, 0)))
    rep3 = lambda a: lane_pad(jnp.concatenate([a] * PIECES, axis=-1))
    return dict(
        win=win.reshape(-1, D_MODEL), wa1=row_pad(piece(4, 5)),
        wdt3=row_pad(jnp.concatenate([piece(7, 8)] * PIECES, axis=1)),
        wa2=jnp.pad(p["w_gla_a2"], ((0, 0), (0, LANES - GLA_GATE_RANK), (0, 0))).astype(BF16),
        dtb3=rep3(p["dt_bias"][:, None, :]), alog3=rep3(p["a_log"][:, None, :]),
        dskip_x=jnp.repeat(p["d_skip"], SSM_HEADDIM, axis=-1)[:, None, :],
        wpg=p["w_proj_gla"].astype(BF16), wps=p["w_proj_ssm"].astype(BF16), wo=p["w_out"].astype(BF16),
        gu1=p["w_ffn1_gu"], d1=p["w_ffn1_down"], gu2=p["w_ffn2_gu"], d2=p["w_ffn2_down"],
    )


PROMPT_TM = 512
FFN_TM = 512
IN_PROJ_TM = 1024
CORE_ROWS = 256


def _trunk(xp, xs, state_gla, state_ssm, state_conv, w, p, fw, nb):
    depth = p["w_in"].shape[0]
    ts = xs.shape[0]
    nseq = ts // SAMPLE_LEN
    row = lambda name, l: p[name][l].reshape(1, -1)
    pconsts, sconsts = _prompt_consts(CORE_ROWS), _sample_consts()
    h_all = state_ssm.reshape(depth, nseq, SSM_INNER, SSM_DSTATE)
    finals_p, finals_s = None, None
    convs_p, convs_s = [], []
    for l in range(depth):
        xp, xs = _ffn(xp, xs, row("norm_ffn1", l), w["gu1"], w["d1"], fw, l=l, tm=FFN_TM, final=False)
        (q, k, v, sg, ga, gb, la), (q_s, k_s, v_s, sg_s, ga_s, gb_s, la_s) = _gla_in(
            xp, xs, row("norm_mix", l), w["win"], w["wa1"], w["wa2"], row("b_gla_a2", l), l=l, tm=IN_PROJ_TM)
        sz, xa, dt, tail = _ssm_in_prompt(xp, row("norm_mix", l), w["win"], w["wdt3"], w["dtb3"][l],
                                          p["conv_w"][l], row("conv_b", l), l=l, nb=nb, tm=IN_PROJ_TM)
        o, y, finals_p = _core_prompt(q, k, v, la, xa, dt, w["alog3"][l], w["dskip_x"][l], pconsts,
                                      nb=nb, rows=CORE_ROWS, l=l, depth=depth, final_states=finals_p)
        convs_p.append(tail)
        sz_s, xr, dt_s = _ssm_in_sample(xs, row("norm_mix", l), w["win"], w["wdt3"], w["dtb3"][l], l=l, tm=ts)
        c4 = jnp.pad(state_conv[l], ((0, 0), (SAMPLE_LEN - (CONV_W - 1), 0), (0, 0))).reshape(ts, CONV_DIM)
        o_s, y_s, s1, h1 = _core_sample(q_s, k_s, v_s, la_s, xr, c4, dt_s, state_gla, h_all, p["conv_w"][l],
                                        row("conv_b", l), w["alog3"][l], w["dskip_x"][l], sconsts,
                                        l=l, new_states=finals_s)
        finals_s = (s1, h1)
        convs_s.append(xr.reshape(nseq, SAMPLE_LEN, CONV_DIM)[:, SAMPLE_LEN - (CONV_W - 1):])
        xp, xs = _merge((xp, o, sg, y, sz, ga, gb), (xs, o_s, sg_s, y_s, sz_s, ga_s, gb_s),
                        row("gla_norm", l), row("ssm_norm", l), w["wpg"], w["wps"], w["wo"], l=l, tm=PROMPT_TM)
        xp, xs = _ffn(xp, xs, row("norm_ffn2", l), w["gu2"], w["d2"], fw, l=l, tm=FFN_TM,
                      final=l == depth - 1)
    s_fin, h_fin = finals_p
    s1, h1 = finals_s
    return (xp, xs,
            s_fin.reshape(depth, nb, GLA_HEADS, GLA_DK, GLA_DV),
            h_fin.reshape(depth, nb, SSM_HEADS, SSM_HEADDIM, SSM_DSTATE), jnp.stack(convs_p),
            s1, h1.reshape(state_ssm.shape), jnp.stack(convs_s))


def kernel(x_prompt, x_sample, state_gla, state_ssm, state_conv, norm_ffn1, w_ffn1_gu, w_ffn1_down, norm_mix, w_in, w_gla_a2, b_gla_a2, gla_norm, conv_w, conv_b, dt_bias, a_log, d_skip, ssm_norm, w_proj_gla, w_proj_ssm, w_out, norm_ffn2, w_ffn2_gu, w_ffn2_down, norm_final):
    p = dict(norm_ffn1=norm_ffn1, w_ffn1_gu=w_ffn1_gu, w_ffn1_down=w_ffn1_down, norm_mix=norm_mix,
             w_in=w_in, w_gla_a2=w_gla_a2, b_gla_a2=b_gla_a2, gla_norm=gla_norm, conv_w=conv_w,
             conv_b=conv_b, dt_bias=dt_bias, a_log=a_log, d_skip=d_skip, ssm_norm=ssm_norm,
             w_proj_gla=w_proj_gla, w_proj_ssm=w_proj_ssm, w_out=w_out, norm_ffn2=norm_ffn2,
             w_ffn2_gu=w_ffn2_gu, w_ffn2_down=w_ffn2_down)
    w = _prep_weights(p)
    fw = norm_final.reshape(1, D_MODEL)
    bp, lp, _ = x_prompt.shape
    bs, ls, _ = x_sample.shape
    assert ls == SAMPLE_LEN and lp % CORE_ROWS == 0 and all(lp % tm == 0 for tm in (PROMPT_TM, FFN_TM, IN_PROJ_TM))
    assert bs % SEQ_PER_STEP == 0 and bp % SEQS_PER_STEP == 0
    yp, ys, gla_p, ssm_p, conv_p, gla_s, ssm_s, conv_s = _trunk(
        x_prompt.reshape(bp * lp, D_MODEL), x_sample.reshape(bs * ls, D_MODEL),
        state_gla, state_ssm, state_conv, w, p, fw, bp)
    return (yp.reshape(bp, lp, D_MODEL), ys.reshape(bs, ls, D_MODEL),
            gla_p, ssm_p, conv_p, gla_s, ssm_s, conv_s)
```

```python
import functools

import numpy as np
import jax
import jax.numpy as jnp
from jax import lax
from jax.experimental import pallas as pl
from jax.experimental.pallas import tpu as pltpu

F32 = jnp.float32
BF16 = jnp.bfloat16

D_MODEL = 1024
EPS = 1e-6
FFN_DIM = 2816
GLA_HEADS = 4
GLA_DK = 128
GLA_DV = 256
GLA_DK_TOTAL = GLA_HEADS * GLA_DK
GLA_DV_TOTAL = GLA_HEADS * GLA_DV
GLA_GATE_RANK = 16
GLA_TAU = 16.0
SSM_INNER = 2048
SSM_HEADDIM = 64
SSM_HEADS = 32
SSM_GROUPS = 4
SSM_HPG = SSM_HEADS // SSM_GROUPS
SSM_DSTATE = 128
CONV_W = 4
CONV_DIM = SSM_INNER + 2 * SSM_GROUPS * SSM_DSTATE
CHUNK = 64
IN_SPLITS = (GLA_DK_TOTAL, GLA_DK_TOTAL, GLA_DV_TOTAL, GLA_DV_TOTAL, GLA_GATE_RANK,
             SSM_INNER, CONV_DIM, SSM_HEADS, D_MODEL, D_MODEL)

LANES = 128
BF16_ROWS = 16
FFN_TILE = 256
GROUP_LANES = SSM_HPG * SSM_HEADDIM
PIECES = 3
LOG2E = 1.4426950408889634
VMEM_LIMIT = 56 * 1024 * 1024

NT_DIMS = (((1,), (1,)), ((), ()))
TN_DIMS = (((0,), (0,)), ((), ()))


def _dot(a, b):
    return jnp.dot(a, b, preferred_element_type=F32)


def _dot_nt(a, b):
    return lax.dot_general(a, b, NT_DIMS, preferred_element_type=F32)


def _dot_tn(a, b):
    return lax.dot_general(a, b, TN_DIMS, preferred_element_type=F32)


def _rms(x):
    return x * lax.rsqrt(jnp.mean(x * x, axis=-1, keepdims=True) + EPS)


def _sigmoid(x):
    return 0.5 * jnp.tanh(0.5 * x) + 0.5


def _silu(x):
    h = 0.5 * x
    return h * jnp.tanh(h) + h


def _softplus(x):
    return jnp.maximum(x, 0.0) + jnp.log1p(jnp.exp(-jnp.abs(x)))


def _split3(x):
    hi = x.astype(BF16)
    r1 = x - hi.astype(F32)
    mid = r1.astype(BF16)
    lo = (r1 - mid.astype(F32)).astype(BF16)
    return hi, mid, lo


def _stack3(x):
    return jnp.concatenate(_split3(x), axis=0)


def _pieces_by_lane(x):
    hi, mid, lo = (piece.astype(F32) for piece in _split3(x))
    lane = lax.broadcasted_iota(jnp.int32, x.shape, 1)
    picked = jnp.where(lane < SSM_HEADS, hi, jnp.where(lane < 2 * SSM_HEADS, mid, lo))
    return picked.astype(BF16)


def _pieces_by_row(x, piece_of_row):
    hi, mid, lo = (piece.astype(F32) for piece in _split3(x))
    idx = jnp.broadcast_to(piece_of_row, x.shape)
    picked = jnp.where(idx == 0, hi, jnp.where(idx == 1, mid, jnp.where(idx == 2, lo, 0.0)))
    return picked.astype(BF16)


def _params(n_arbitrary=1):
    return pltpu.CompilerParams(dimension_semantics=("arbitrary",) * n_arbitrary,
                                vmem_limit_bytes=VMEM_LIMIT)


def _const_spec(shape):
    nd = len(shape)
    return pl.BlockSpec(shape, lambda *_: (0,) * nd)


def _layer_spec(l, shape):
    return pl.BlockSpec((None,) + shape, lambda *_: (l, 0, 0), pipeline_mode=pl.Buffered(1))


IN_OFFS = tuple(int(o) for o in np.cumsum((0,) + IN_SPLITS))


def _in_rows_spec(l, first, last):
    start, rows = l * IN_OFFS[-1] + IN_OFFS[first], IN_OFFS[last] - IN_OFFS[first]
    return pl.BlockSpec((pl.Element(rows), pl.Element(D_MODEL)), lambda *_: (start, 0),
                        pipeline_mode=pl.Buffered(1))


def _two_group_call(body, name, rows_p, rows_s, shared, shared_specs, outs, *, tm, scratch_shapes=()):
    tp, ts = rows_p[0].shape[0], rows_s[0].shape[0]
    n_p = tp // tm
    assert tp % tm == 0 and len(rows_p) == len(rows_s)
    p_spec = lambda width: pl.BlockSpec((tm, width), lambda i: (jnp.minimum(i, n_p - 1), 0))
    s_spec = lambda width: pl.BlockSpec((ts, width), lambda i: (0, 0))
    n_in, n_sh, n_out = len(rows_p), len(shared), len(outs)

    def kernel(*refs):
        in_p, in_s = refs[:n_in], refs[n_in:2 * n_in]
        sh = refs[2 * n_in:2 * n_in + n_sh]
        out_p = refs[2 * n_in + n_sh:2 * n_in + n_sh + n_out]
        out_s = refs[2 * n_in + n_sh + n_out:2 * n_in + n_sh + 2 * n_out]
        scratch = refs[2 * n_in + n_sh + 2 * n_out:]
        i = pl.program_id(0)

        @pl.when(i < n_p)
        def _():
            body(in_p, sh, out_p, scratch)

        @pl.when(i == n_p)
        def _():
            body(in_s, sh, out_s, scratch)

    res = pl.pallas_call(
        kernel,
        grid=(n_p + 1,),
        in_specs=[p_spec(a.shape[1]) for a in rows_p] + [s_spec(a.shape[1]) for a in rows_s] + list(shared_specs),
        out_specs=[p_spec(width) for width, _ in outs] + [s_spec(width) for width, _ in outs],
        out_shape=[jax.ShapeDtypeStruct((tp, width), dt) for width, dt in outs]
                  + [jax.ShapeDtypeStruct((ts, width), dt) for width, dt in outs],
        scratch_shapes=list(scratch_shapes),
        compiler_params=_params(),
        name=name,
    )(*rows_p, *rows_s, *shared)
    return res[:n_out], res[n_out:]


def _ffn_kernel(x_ref, nw_ref, wgu_ref, wd_ref, fw_ref, o_ref, acc_ref, *, final):
    x = x_ref[...]
    rows = x.shape[0]
    xn = (_rms(x) * nw_ref[...]).astype(BF16)
    for c in range(FFN_DIM // FFN_TILE):
        lo = c * FFN_TILE
        g = _dot(xn, wgu_ref[:, lo:lo + FFN_TILE].astype(BF16))
        u = _dot(xn, wgu_ref[:, FFN_DIM + lo:FFN_DIM + lo + FFN_TILE].astype(BF16))
        a = (_silu(g) * u).astype(BF16)
        part = _dot(a, wd_ref[lo:lo + FFN_TILE, :].astype(BF16))
        if c == 0:
            acc_ref[0:rows, :] = part
        else:
            acc_ref[0:rows, :] += part
    y = x + 0.5 * acc_ref[0:rows, :]
    if final:
        y = _rms(y) * fw_ref[...]
    o_ref[...] = y


def _ffn(xp, xs, nw, wgu, wd, fw, *, l, tm, final):
    body = lambda rows, sh, outs, scratch: _ffn_kernel(rows[0], *sh, outs[0], scratch[0], final=final)
    (yp,), (ys,) = _two_group_call(
        body, "ffn_final" if final else "ffn", [xp], [xs], [nw, wgu, wd, fw],
        [_const_spec((1, D_MODEL)), _layer_spec(l, (D_MODEL, 2 * FFN_DIM)), _layer_spec(l, (FFN_DIM, D_MODEL)),
         _const_spec((1, D_MODEL))],
        [(D_MODEL, F32)], tm=tm, scratch_shapes=[pltpu.VMEM((max(tm, xs.shape[0]), D_MODEL), F32)])
    return yp, ys


def _gla_in_kernel(x_ref, nw_ref, wqkvg_ref, wgate_ref, wa1_ref, wa2_ref, ba2_ref,
                   q_ref, k_ref, v_ref, sg_ref, ga_ref, gb_ref, la_ref):
    u = (_rms(x_ref[...]) * nw_ref[...]).astype(BF16)
    q_ref[...] = _dot_nt(u, wqkvg_ref[0:512, :]).astype(BF16)
    k_ref[...] = _dot_nt(u, wqkvg_ref[512:1024, :]).astype(BF16)
    v_ref[...] = _dot_nt(u, wqkvg_ref[1024:2048, :]).astype(BF16)
    sg_ref[...] = _silu(_dot_nt(u, wqkvg_ref[2048:3072, :])).astype(BF16)
    ga_ref[...] = _sigmoid(_dot_nt(u, wgate_ref[0:1024, :])).astype(BF16)
    gb_ref[...] = _sigmoid(_dot_nt(u, wgate_ref[1024:2048, :])).astype(BF16)
    a_lr = _dot_nt(u, wa1_ref[...]).astype(BF16)
    z = _dot(a_lr, wa2_ref[...]) + ba2_ref[...]
    la_ref[...] = -_softplus(-z) / GLA_TAU


def _gla_in(xp, xs, nw, win, wa1, wa2, ba2, *, l, tm):
    body = lambda rows, sh, outs, scratch: _gla_in_kernel(rows[0], *sh, *outs)
    return _two_group_call(
        body, "gla_in", [xp], [xs], [nw, win, win, wa1, wa2, ba2],
        [_const_spec((1, D_MODEL)), _in_rows_spec(l, 0, 4), _in_rows_spec(l, 8, 10),
         _layer_spec(l, (LANES, D_MODEL)), _layer_spec(l, (LANES, GLA_DK_TOTAL)), _const_spec((1, GLA_DK_TOTAL))],
        [(512, BF16), (512, BF16), (1024, BF16), (1024, BF16), (1024, BF16), (1024, BF16), (512, F32)], tm=tm)


ZX_COLS = SSM_INNER + CONV_DIM


CONV_COLS = 512
SUBLANES = 8


def _ssm_in_prompt_kernel(x_ref, nw_ref, wzx_ref, wdt_ref, dtb_ref, cw_ref, cb_ref,
                          sz_ref, xa_ref, dt_ref, tail_ref, carry_ref, *, tm):
    j = pl.program_id(1)

    @pl.when(j == 0)
    def _():
        carry_ref[...] = jnp.zeros(carry_ref.shape, F32)

    u = (_rms(x_ref[...]) * nw_ref[...]).astype(BF16)
    sz_ref[...] = _silu(_dot_nt(u, wzx_ref[0:SSM_INNER, :])).astype(BF16)
    dt_ref[...] = _softplus(_dot_nt(u, wdt_ref[...]) + dtb_ref[...])
    tiles = (tm // SUBLANES, SUBLANES, CONV_COLS)
    sub = lax.broadcasted_iota(jnp.int32, tiles, 1)
    for cc in range(CONV_DIM // CONV_COLS):
        cols = slice(cc * CONV_COLS, (cc + 1) * CONV_COLS)
        xr = _dot_nt(u, wzx_ref[SSM_INNER + cc * CONV_COLS:SSM_INNER + (cc + 1) * CONV_COLS, :])
        x3 = xr.reshape(tiles)
        p3 = jnp.concatenate([carry_ref[:, cols], xr[0:tm - SUBLANES, :]], axis=0).reshape(tiles)
        acc = cb_ref[:, cols] + xr * cw_ref[CONV_W - 1:CONV_W, cols]
        for i in range(CONV_W - 1):
            back = CONV_W - 1 - i
            sh = pltpu.roll(jnp.where(sub >= SUBLANES - back, p3, x3), back, axis=1)
            acc = acc + sh.reshape(tm, CONV_COLS) * cw_ref[i:i + 1, cols]
        xa_ref[:, cols] = _silu(acc).astype(BF16)
        carry_ref[:, cols] = xr[tm - SUBLANES:tm, :]
    tail_ref[0] = carry_ref[SUBLANES - (CONV_W - 1):SUBLANES, :]


def _ssm_in_prompt(x, nw, wzx, wdt, dtb, cw, cb, *, l, nb, tm):
    t = x.shape[0]
    nj = t // nb // tm
    row = lambda width: pl.BlockSpec((tm, width), lambda b, j: (b * nj + j, 0))
    return pl.pallas_call(
        functools.partial(_ssm_in_prompt_kernel, tm=tm),
        grid=(nb, nj),
        in_specs=[row(D_MODEL), _const_spec((1, D_MODEL)),
                  _in_rows_spec(l, 5, 7), _layer_spec(l, (LANES, D_MODEL)),
                  _const_spec((1, LANES)), _const_spec((CONV_W, CONV_DIM)), _const_spec((1, CONV_DIM))],
        out_specs=[row(SSM_INNER), row(CONV_DIM), row(LANES),
                   pl.BlockSpec((1, CONV_W - 1, CONV_DIM), lambda b, j: (b, 0, 0))],
        out_shape=[jax.ShapeDtypeStruct((t, SSM_INNER), BF16),
                   jax.ShapeDtypeStruct((t, CONV_DIM), BF16),
                   jax.ShapeDtypeStruct((t, LANES), F32),
                   jax.ShapeDtypeStruct((nb, CONV_W - 1, CONV_DIM), F32)],
        scratch_shapes=[pltpu.VMEM((SUBLANES, CONV_DIM), F32)],
        compiler_params=_params(n_arbitrary=2),
        name="ssm_in_prompt",
    )(x, nw, wzx, wdt, dtb, cw, cb)


def _ssm_in_sample_kernel(x_ref, nw_ref, wzx_ref, wdt_ref, dtb_ref, sz_ref, xr_ref, dt_ref):
    u = (_rms(x_ref[...]) * nw_ref[...]).astype(BF16)
    sz_ref[...] = _silu(_dot_nt(u, wzx_ref[0:SSM_INNER, :])).astype(BF16)
    xr_ref[...] = _dot_nt(u, wzx_ref[SSM_INNER:ZX_COLS, :])
    dt_ref[...] = _softplus(_dot_nt(u, wdt_ref[...]) + dtb_ref[...])


def _ssm_in_sample(x, nw, wzx, wdt, dtb, *, l, tm):
    t = x.shape[0]
    row = lambda width: pl.BlockSpec((tm, width), lambda i: (i, 0))
    return pl.pallas_call(
        _ssm_in_sample_kernel,
        grid=(t // tm,),
        in_specs=[row(D_MODEL), _const_spec((1, D_MODEL)),
                  _in_rows_spec(l, 5, 7), _layer_spec(l, (LANES, D_MODEL)),
                  _const_spec((1, LANES))],
        out_specs=[row(SSM_INNER), row(CONV_DIM), row(LANES)],
        out_shape=[jax.ShapeDtypeStruct((t, SSM_INNER), BF16),
                   jax.ShapeDtypeStruct((t, CONV_DIM), F32),
                   jax.ShapeDtypeStruct((t, LANES), F32)],
        compiler_params=_params(),
        name="ssm_in_sample",
    )(x, nw, wzx, wdt, dtb)


def _gla_rows(q, k, bcum, blast):
    qf = q.astype(F32) * (GLA_DK ** -0.5)
    kf = k.astype(F32)
    q_in = (qf * jnp.exp2(bcum)).astype(BF16)
    k_in = (kf * jnp.exp2(-bcum)).astype(BF16)
    k_out = (kf * jnp.exp2(blast - bcum)).astype(BF16)
    return q_in, k_in, k_out


def _segment_weights(ac_s, itile, cmask):
    arow = jnp.sum(ac_s * itile, axis=0, keepdims=True)
    return jnp.where(cmask > 0.5, jnp.exp2(ac_s - arow), 0.0)


PAIR = 2


SEQS_PER_STEP = 1
CHUNK_BATCH = 1


def _core_prompt_kernel(q_ref, k_ref, v_ref, la_ref, xa_ref, dt_ref, alog3_ref, dskip_ref,
                        btril3_ref, e3_ref, itile_ref, cmask_ref, bmask_ref, *rest, rows, nj):
    o_ref, y_ref, gla_ref, ssm_ref, s_ref, ht_ref = rest[-6:]
    j = pl.program_id(1)

    @pl.when(j == 0)
    def _():
        s_ref[...] = jnp.zeros(s_ref.shape, F32)
        ht_ref[...] = jnp.zeros(ht_ref.shape, F32)

    btril3 = btril3_ref[...]
    causal = btril3[0:CHUNK, 0:CHUNK] > 0.5
    ones16 = jnp.ones((BF16_ROWS, LANES), BF16)
    row16 = lax.broadcasted_iota(jnp.int32, (BF16_ROWS, 1), 0)
    itile = itile_ref[...]
    cmask = cmask_ref[...]
    bmask = bmask_ref[...]
    e3 = e3_ref[...]
    a3 = -jnp.exp(alog3_ref[...]) * LOG2E

    bcum_all, dtx_all, acx_all = [], [], []
    for sq in range(SEQS_PER_STEP):
        bcum_all.append(_dot(btril3, _stack3(la_ref[sq] * LOG2E)))
        dt3 = dt_ref[sq]
        acum3 = _dot(btril3, _stack3(dt3 * a3))
        dtx_all.append(_dot(_pieces_by_lane(dt3), e3))
        acx_all.append(_dot(_pieces_by_lane(acum3), e3))

    all_chunks = [slice(c * CHUNK, (c + 1) * CHUNK) for c in range(rows // CHUNK)]
    batches = [all_chunks[b:b + CHUNK_BATCH] for b in range(0, len(all_chunks), CHUNK_BATCH)]
    kls = [slice(h * GLA_DK, (h + 1) * GLA_DK) for h in range(GLA_HEADS)]
    vls = [slice(h * GLA_DV, (h + 1) * GLA_DV) for h in range(GLA_HEADS)]
    gls = [slice(g * GROUP_LANES, (g + 1) * GROUP_LANES) for g in range(SSM_GROUPS)]

    for chunks, sq in [(batch, sq) for batch in batches for sq in range(SEQS_PER_STEP)]:
        prep = []
        for r in chunks:
            bcum = bcum_all[sq][r, :]
            blast = bcum[CHUNK - 1:CHUNK, :]
            q_in, k_in, k_out = _gla_rows(q_ref[sq, r, :], k_ref[sq, r, :], bcum, blast)
            bl_pieces = _pieces_by_row(jnp.broadcast_to(blast, (BF16_ROWS, GLA_DK_TOTAL)), row16)
            prep.append((q_in, k_in, k_out, bl_pieces, v_ref[sq, r, :]))
        pairs = [(c, h) for h in range(GLA_HEADS) for c in range(len(chunks))]
        qk = {(c, h): _dot_nt(prep[c][0][:, kls[h]], prep[c][1][:, kls[h]]) for c, h in pairs}
        sc = {ch: jnp.where(causal, qk[ch], 0.0).astype(BF16) for ch in pairs}
        intra = {}
        for c, h in pairs:
            _, _, k_out, bl_pieces, v = prep[c]
            dec = jnp.exp2(_dot_tn(bl_pieces[:, kls[h]], ones16))
            intra[c, h] = (_dot(sc[c, h], v[:, vls[h]]), _dot_tn(k_out[:, kls[h]], v[:, vls[h]]), dec)
        for c, r in enumerate(chunks):
            for h, (kl, vl) in enumerate(zip(kls, vls)):
                o_intra, ds, dec = intra[c, h]
                s_old = s_ref[sq, h]
                o_ref[sq, r, vl] = (o_intra + _dot(prep[c][0][:, kl], s_old.astype(BF16))).astype(BF16)
                s_ref[sq, h] = s_old * jnp.concatenate([dec, dec], axis=1) + ds
        prep = []
        for r in chunks:
            acx = acx_all[sq][r, :]
            alast = acx[CHUNK - 1:CHUNK, :]
            xa = xa_ref[sq, r, :]
            xs = xa[:, 0:SSM_INNER].astype(F32)
            xg = xs * dtx_all[sq][r, :]
            xw = (xg * jnp.exp2(alast - acx)).astype(BF16)
            seg = _segment_weights(acx, itile, cmask)
            prep.append((xa, xs, xg.astype(BF16), xw, jnp.exp2(acx), seg, jnp.exp2(alast)))
        pairs = [(c, g) for g in range(SSM_GROUPS) for c in range(len(chunks))]
        b_of = lambda c, g: prep[c][0][:, SSM_INNER + g * SSM_DSTATE:SSM_INNER + (g + 1) * SSM_DSTATE]
        c_of = lambda c, g: prep[c][0][:, SSM_INNER + (SSM_GROUPS + g) * SSM_DSTATE:
                                       SSM_INNER + (SSM_GROUPS + g + 1) * SSM_DSTATE]
        cbx = {(c, g): _dot_nt(c_of(c, g), jnp.concatenate([b_of(c, g)] * SSM_HPG, axis=0)) for c, g in pairs}
        w = {(c, g): (cbx[c, g] * prep[c][5][:, gls[g]]).astype(BF16) for c, g in pairs}
        intra = {}
        for c, g in pairs:
            xs, xgb, xw = prep[c][1], prep[c][2], prep[c][3]
            yd = []
            for pr in range(SSM_HPG // PAIR):
                lo = g * GROUP_LANES + pr * LANES
                xblk = jnp.concatenate([xgb[:, lo:lo + LANES]] * PAIR, axis=0) * bmask
                yd.append(_dot(w[c, g][:, pr * LANES:(pr + 1) * LANES], xblk))
            y_part = jnp.concatenate(yd, axis=1) + dskip_ref[:, gls[g]] * xs[:, gls[g]]
            intra[c, g] = (c_of(c, g), y_part, _dot_tn(b_of(c, g), xw[:, gls[g]]))
        for c, r in enumerate(chunks):
            eac, dec_row = prep[c][4], prep[c][6]
            for g, gl in enumerate(gls):
                cg, y_part, ds_t = intra[c, g]
                h_old = ht_ref[sq, :, gl]
                y_ref[sq, r, gl] = (y_part + _dot(cg, h_old.astype(BF16)) * eac[:, gl]).astype(BF16)
                ht_ref[sq, :, gl] = h_old * dec_row[:, gl] + ds_t

    @pl.when(j == nj - 1)
    def _():
        for sq in range(SEQS_PER_STEP):
            gla_ref[sq] = s_ref[sq]
            ssm_ref[sq] = ht_ref[sq].T


def _core_prompt(q, k, v, la, xa, dt, alog3, dskip_x, consts, *, nb, rows, l, depth, final_states):
    t = q.shape[0]
    nq = SEQS_PER_STEP
    half, length = nb // nq, t // nb
    nj = length // rows
    split = lambda a: a.reshape(nq, half, length, a.shape[-1])
    row = lambda width: pl.BlockSpec((nq, None, rows, width), lambda b, j: (0, b, j, 0))
    inputs = [split(q), split(k), split(v), split(la), split(xa), split(dt), alog3, dskip_x, *consts]
    in_specs = ([row(512), row(512), row(1024), row(512), row(CONV_DIM), row(LANES),
                 _const_spec((1, LANES)), _const_spec((1, SSM_INNER))]
                + [_const_spec(c.shape) for c in consts])
    aliases = {}
    if final_states is not None:
        aliases = {len(inputs): 2, len(inputs) + 1: 3}
        inputs += list(final_states)
        in_specs += [pl.BlockSpec(memory_space=pl.ANY)] * 2
    o, y, s_fin, h_fin = pl.pallas_call(
        functools.partial(_core_prompt_kernel, rows=rows, nj=nj),
        grid=(half, nj),
        in_specs=in_specs,
        out_specs=[row(GLA_DV_TOTAL), row(SSM_INNER),
                   pl.BlockSpec((None, nq, None, GLA_HEADS, GLA_DK, GLA_DV), lambda b, j: (l, 0, b, 0, 0, 0)),
                   pl.BlockSpec((None, nq, None, SSM_INNER, SSM_DSTATE), lambda b, j: (l, 0, b, 0, 0))],
        out_shape=[jax.ShapeDtypeStruct((nq, half, length, GLA_DV_TOTAL), BF16),
                   jax.ShapeDtypeStruct((nq, half, length, SSM_INNER), BF16),
                   jax.ShapeDtypeStruct((depth, nq, half, GLA_HEADS, GLA_DK, GLA_DV), F32),
                   jax.ShapeDtypeStruct((depth, nq, half, SSM_INNER, SSM_DSTATE), F32)],
        input_output_aliases=aliases,
        scratch_shapes=[pltpu.VMEM((nq, GLA_HEADS, GLA_DK, GLA_DV), F32),
                        pltpu.VMEM((nq, SSM_DSTATE, SSM_INNER), F32)],
        compiler_params=_params(n_arbitrary=2),
        name="core_prompt",
    )(*inputs)
    return o.reshape(t, GLA_DV_TOTAL), y.reshape(t, SSM_INNER), (s_fin, h_fin)


SEQ_PER_STEP = 4
SAMPLE_LEN = 4
SAMPLE_ROWS = SEQ_PER_STEP * SAMPLE_LEN
STATE_RING = 3


def _core_sample_kernel(q_ref, k_ref, v_ref, la_ref, xr_ref, c4_ref, dt_ref, s0_ref, h0_ref,
                        cw_ref, cb_ref, alog3_ref, dskip_ref,
                        bs3_ref, shmat_ref, e3_ref, es3_ref, itile_ref, cmask_ref, bmask_ref,
                        qmask_ref, smask_ref, pos_ref, *rest, layer, n_steps):
    o_ref, y_ref, s1_ref, h1_ref, s_ring, h_ring, sems = rest[-7:]
    i = pl.program_id(0)

    def state_copies(step):
        seqs = pl.ds(step * SEQ_PER_STEP, SEQ_PER_STEP)
        slot = step % STATE_RING
        return (pltpu.make_async_copy(s0_ref.at[layer, seqs], s_ring.at[slot], sems.at[0, slot]),
                pltpu.make_async_copy(h0_ref.at[layer, seqs], h_ring.at[slot], sems.at[1, slot]))

    @pl.when(i == 0)
    def _():
        for step in range(min(STATE_RING - 1, n_steps)):
            for copy in state_copies(step):
                copy.start()

    @pl.when(i + STATE_RING - 1 < n_steps)
    def _():
        for copy in state_copies(i + STATE_RING - 1):
            copy.start()

    for copy in state_copies(i):
        copy.wait()
    slot = i % STATE_RING
    rs = SAMPLE_ROWS
    bs3 = bs3_ref[...]
    same_causal = bs3[0:rs, 0:rs] > 0.5
    ones16 = jnp.ones((BF16_ROWS, LANES), BF16)
    pos = pos_ref[...]
    xr = xr_ref[...]
    shifted = _dot(shmat_ref[...], jnp.concatenate(_split3(xr) + _split3(c4_ref[...]), axis=0))
    acc = cb_ref[...] + xr * cw_ref[CONV_W - 1:CONV_W, :]
    for i in range(CONV_W - 1):
        back = CONV_W - 1 - i
        acc = acc + shifted[(back - 1) * rs:back * rs, :] * cw_ref[i:i + 1, :]
    xa = _silu(acc)
    sums = _dot(bs3, _stack3(la_ref[...] * LOG2E))
    bcum, blast = sums[0:rs, :], sums[rs:2 * rs, :]
    q_in, k_in, k_out = _gla_rows(q_ref[...], k_ref[...], bcum, blast)
    bl_pieces = _pieces_by_row(blast, pos)
    v = v_ref[...]
    qmask = qmask_ref[...]
    seq_blk = lambda a: jnp.concatenate([a] * SEQ_PER_STEP, axis=1)
    for h in range(GLA_HEADS):
        kl = slice(h * GLA_DK, (h + 1) * GLA_DK)
        vl = slice(h * GLA_DV, (h + 1) * GLA_DV)
        sc = jnp.where(same_causal, _dot_nt(q_in[:, kl], k_in[:, kl]), 0.0).astype(BF16)
        s_old = s_ring[slot, :, h].reshape(SEQ_PER_STEP * GLA_DK, GLA_DV)
        o_h = _dot(sc, v[:, vl]) + _dot(seq_blk(q_in[:, kl]) * qmask, s_old.astype(BF16))
        o_ref[:, vl] = o_h.astype(BF16)
        ds = _dot_tn(seq_blk(k_out[:, kl]) * qmask, v[:, vl])
        dec = jnp.exp2(_dot_tn(seq_blk(bl_pieces[:, kl]) * qmask, ones16))
        s_new = s_old * jnp.concatenate([dec, dec], axis=1) + ds
        s1_ref[:, h] = s_new.reshape(SEQ_PER_STEP, GLA_DK, GLA_DV)
    dt3 = dt_ref[...]
    sums = _dot(bs3, _stack3(dt3 * (-jnp.exp(alog3_ref[...]) * LOG2E)))
    acum3, alast3 = sums[0:rs, :], sums[rs:2 * rs, :]
    p_ac = _pieces_by_lane(acum3)
    wide = _dot(jnp.concatenate([_pieces_by_lane(dt3), p_ac, _pieces_by_lane(alast3)], axis=0), e3_ref[...])
    dtx, acx, alx = wide[0:rs, :], wide[rs:2 * rs, :], wide[2 * rs:3 * rs, :]
    xs = xa[:, 0:SSM_INNER]
    xg = xs * dtx
    xgb = xg.astype(BF16)
    xw = xg * jnp.exp2(alx - acx)
    eac = jnp.exp2(acx)
    seg = _segment_weights(_dot(p_ac, es3_ref[...]), itile_ref[...], cmask_ref[...])
    al_pieces = _pieces_by_row(alx, pos)
    smask = smask_ref[...]
    sl = SSM_HPG * rs
    for g in range(SSM_GROUPS):
        gl = slice(g * GROUP_LANES, (g + 1) * GROUP_LANES)
        bg = xa[:, SSM_INNER + g * SSM_DSTATE:SSM_INNER + (g + 1) * SSM_DSTATE].astype(BF16)
        cg = xa[:, SSM_INNER + (SSM_GROUPS + g) * SSM_DSTATE:
                SSM_INNER + (SSM_GROUPS + g + 1) * SSM_DSTATE].astype(BF16)
        cbx = _dot_nt(cg, jnp.concatenate([bg] * SSM_HPG, axis=0))
        w = (cbx * seg[:, g * sl:(g + 1) * sl]).astype(BF16)
        xblk = jnp.concatenate([xgb[:, gl]] * SSM_HPG, axis=0) * bmask_ref[...]
        yd = _dot(w, xblk)
        h_old = [h_ring[slot, s, gl, :] for s in range(SEQ_PER_STEP)]
        yo = _dot_nt(seq_blk(cg) * qmask, jnp.concatenate([hs.astype(BF16) for hs in h_old], axis=1))
        y = yd + yo * eac[:, gl] + dskip_ref[:, gl] * xs[:, gl]
        y_ref[:, gl] = y.astype(BF16)
        ds = _dot_tn(seq_blk(xw[:, gl]).astype(BF16) * smask, bg)
        dec = jnp.exp2(_dot_tn(seq_blk(al_pieces[:, gl]) * smask, ones16))
        for s in range(SEQ_PER_STEP):
            sr = slice(s * GROUP_LANES, (s + 1) * GROUP_LANES)
            h1_ref[s, gl, :] = h_old[s] * dec[sr, :] + ds[sr, :]


def _core_sample(q, k, v, la, xr, c4, dt, s_all, h_all, cw, cb, alog3, dskip_x, consts, *, l, new_states):
    t = q.shape[0]
    rs = SAMPLE_ROWS
    nseq = t // SAMPLE_LEN
    row = lambda width: pl.BlockSpec((rs, width), lambda i: (i, 0))
    s_spec = pl.BlockSpec((None, SEQ_PER_STEP, GLA_HEADS, GLA_DK, GLA_DV), lambda i: (l, i, 0, 0, 0))
    h_spec = pl.BlockSpec((None, SEQ_PER_STEP, SSM_INNER, SSM_DSTATE), lambda i: (l, i, 0, 0))
    inputs = [q, k, v, la, xr, c4, dt, s_all, h_all, cw, cb, alog3, dskip_x, *consts]
    in_specs = ([row(512), row(512), row(1024), row(512), row(CONV_DIM), row(CONV_DIM), row(LANES),
                 pl.BlockSpec(memory_space=pl.ANY), pl.BlockSpec(memory_space=pl.ANY),
                 _const_spec((CONV_W, CONV_DIM)), _const_spec((1, CONV_DIM)),
                 _const_spec((1, LANES)), _const_spec((1, SSM_INNER))]
                + [_const_spec(c.shape) for c in consts])
    aliases = {}
    if new_states is not None:
        aliases = {len(inputs): 2, len(inputs) + 1: 3}
        inputs += list(new_states)
        in_specs += [pl.BlockSpec(memory_space=pl.ANY)] * 2
    return pl.pallas_call(
        functools.partial(_core_sample_kernel, layer=l, n_steps=nseq // SEQ_PER_STEP),
        grid=(nseq // SEQ_PER_STEP,),
        in_specs=in_specs,
        out_specs=[row(GLA_DV_TOTAL), row(SSM_INNER), s_spec, h_spec],
        out_shape=[jax.ShapeDtypeStruct((t, GLA_DV_TOTAL), BF16),
                   jax.ShapeDtypeStruct((t, SSM_INNER), BF16),
                   jax.ShapeDtypeStruct(s_all.shape, F32),
                   jax.ShapeDtypeStruct(h_all.shape, F32)],
        input_output_aliases=aliases,
        scratch_shapes=[pltpu.VMEM((STATE_RING, SEQ_PER_STEP, GLA_HEADS, GLA_DK, GLA_DV), F32),
                        pltpu.VMEM((STATE_RING, SEQ_PER_STEP, SSM_INNER, SSM_DSTATE), F32),
                        pltpu.SemaphoreType.DMA((2, STATE_RING))],
        compiler_params=_params(),
        name="core_sample",
    )(*inputs)


def _merge_kernel(x_ref, o_ref, sg_ref, y_ref, sz_ref, ga_ref, gb_ref,
                  gn_ref, sn_ref, wpg_ref, wps_ref, wo_ref, out_ref):
    o = o_ref[...].astype(F32)
    parts = []
    for h in range(GLA_HEADS):
        vl = slice(h * GLA_DV, (h + 1) * GLA_DV)
        parts.append(_rms(o[:, vl]) * gn_ref[:, vl])
    og = (jnp.concatenate(parts, axis=1) * sg_ref[...].astype(F32)).astype(BF16)
    m_gla = ga_ref[...].astype(F32) * _dot(og, wpg_ref[...])
    yz = y_ref[...].astype(F32) * sz_ref[...].astype(F32)
    yn = (_rms(yz) * sn_ref[...]).astype(BF16)
    m = m_gla + gb_ref[...].astype(F32) * _dot(yn, wps_ref[...])
    out_ref[...] = x_ref[...] + _dot(m.astype(BF16), wo_ref[...])


def _merge(rows_p, rows_s, gn, sn, wpg, wps, wo, *, l, tm):
    body = lambda rows, sh, outs, scratch: _merge_kernel(*rows, *sh, outs[0])
    (xp,), (xs,) = _two_group_call(
        body, "merge", list(rows_p), list(rows_s), [gn, sn, wpg, wps, wo],
        [_const_spec((1, GLA_DV_TOTAL)), _const_spec((1, SSM_INNER)), _layer_spec(l, (GLA_DV_TOTAL, D_MODEL)),
         _layer_spec(l, (SSM_INNER, D_MODEL)), _layer_spec(l, (D_MODEL, D_MODEL))],
        [(D_MODEL, F32)], tm=tm)
    return xp, xs


def _expand_table(width):
    t = np.zeros((LANES, SSM_HEADS * width), np.float32)
    for k in range(PIECES):
        for h in range(SSM_HEADS):
            t[k * SSM_HEADS + h, h * width:(h + 1) * width] = 1.0
    return jnp.asarray(t, dtype=BF16)


def _segment_tables(rows, allowed):
    s_of_lane = np.arange(SSM_HPG * rows) % rows
    itile = (np.arange(rows)[:, None] == s_of_lane[None, :]).astype(np.float32)
    cmask = allowed[:, s_of_lane].astype(np.float32)
    return itile, cmask


def _head_block_mask(heads, rows):
    hr = np.arange(heads * rows) // rows
    hc = np.arange(heads * SSM_HEADDIM) // SSM_HEADDIM
    return jnp.asarray((hr[:, None] == hc[None, :]).astype(np.float32), dtype=BF16)


def _prompt_consts(rows):
    idx = np.arange(rows)
    btril = ((idx[:, None] // CHUNK == idx[None, :] // CHUNK) & (idx[None, :] <= idx[:, None])).astype(np.float32)
    tril = np.tril(np.ones((CHUNK, CHUNK), np.float32))
    itile, cmask = _segment_tables(CHUNK, tril)
    tile4 = lambda a: jnp.asarray(np.tile(a, (1, SSM_GROUPS)))
    return (jnp.asarray(np.tile(btril, (1, PIECES)), dtype=BF16), _expand_table(SSM_HEADDIM),
            tile4(itile), tile4(cmask), _head_block_mask(PAIR, CHUNK))


def _sample_consts():
    rs = SAMPLE_ROWS
    seq = np.arange(rs) // SAMPLE_LEN
    pos = np.arange(rs) % SAMPLE_LEN
    same = (seq[:, None] == seq[None, :]).astype(np.float32)
    btril = same * (pos[None, :] <= pos[:, None])
    bs3 = np.tile(np.concatenate([btril, same], axis=0), (1, PIECES))
    shmat = np.zeros((CONV_W - 1, rs, 2, rs), np.float32)
    for back in range(1, CONV_W):
        for r in range(rs):
            if pos[r] >= back:
                shmat[back - 1, r, 0, r - back] = 1.0
            else:
                shmat[back - 1, r, 1, seq[r] * SAMPLE_LEN + SAMPLE_LEN + pos[r] - back] = 1.0
    shmat = np.repeat(shmat[:, :, :, None, :], PIECES, axis=3).reshape((CONV_W - 1) * rs, 2 * PIECES * rs)
    itile, cmask = _segment_tables(rs, btril)
    tile4 = lambda a: jnp.asarray(np.tile(a, (1, SSM_GROUPS)))
    assert GLA_DK == SSM_DSTATE == LANES
    qmask = (seq[:, None] == (np.arange(SEQ_PER_STEP * LANES) // LANES)[None, :]).astype(np.float32)
    smask = (seq[:, None] == (np.arange(SEQ_PER_STEP * GROUP_LANES) // GROUP_LANES)[None, :]).astype(np.float32)
    return (jnp.asarray(bs3, dtype=BF16), jnp.asarray(shmat, dtype=BF16),
            _expand_table(SSM_HEADDIM), _expand_table(rs), tile4(itile), tile4(cmask),
            _head_block_mask(SSM_HPG, rs), jnp.asarray(qmask, dtype=BF16), jnp.asarray(smask, dtype=BF16),
            jnp.asarray(pos.reshape(rs, 1).astype(np.int32)))


def _prep_weights(p):
    win = jnp.swapaxes(p["w_in"], 1, 2).astype(BF16)
    piece = lambda i, j: jnp.swapaxes(p["w_in"][:, :, IN_OFFS[i]:IN_OFFS[j]], 1, 2).astype(BF16)
    lane_pad = lambda w: jnp.pad(w, ((0, 0), (0, 0), (0, LANES - w.shape[-1])))
    row_pad = lambda w: jnp.pad(w, ((0, 0), (0, LANES - w.shape[1]), (0, 0)))
    rep3 = lambda a: lane_pad(jnp.concatenate([a] * PIECES, axis=-1))
    return dict(
        win=win.reshape(-1, D_MODEL), wa1=row_pad(piece(4, 5)),
        wdt3=row_pad(jnp.concatenate([piece(7, 8)] * PIECES, axis=1)),
        wa2=jnp.pad(p["w_gla_a2"], ((0, 0), (0, LANES - GLA_GATE_RANK), (0, 0))).astype(BF16),
        dtb3=rep3(p["dt_bias"][:, None, :]), alog3=rep3(p["a_log"][:, None, :]),
        dskip_x=jnp.repeat(p["d_skip"], SSM_HEADDIM, axis=-1)[:, None, :],
        wpg=p["w_proj_gla"].astype(BF16), wps=p["w_proj_ssm"].astype(BF16), wo=p["w_out"].astype(BF16),
        gu1=p["w_ffn1_gu"], d1=p["w_ffn1_down"], gu2=p["w_ffn2_gu"], d2=p["w_ffn2_down"],
    )


PROMPT_TM = 512
FFN_TM = 512
IN_PROJ_TM = 1024
CORE_ROWS = 256


def _trunk(xp, xs, state_gla, state_ssm, state_conv, w, p, fw, nb):
    depth = p["w_in"].shape[0]
    ts = xs.shape[0]
    nseq = ts // SAMPLE_LEN
    row = lambda name, l: p[name][l].reshape(1, -1)
    pconsts, sconsts = _prompt_consts(CORE_ROWS), _sample_consts()
    h_all = state_ssm.reshape(depth, nseq, SSM_INNER, SSM_DSTATE)
    finals_p, finals_s = None, None
    convs_p, convs_s = [], []
    for l in range(depth):
        xp, xs = _ffn(xp, xs, row("norm_ffn1", l), w["gu1"], w["d1"], fw, l=l, tm=FFN_TM, final=False)
        (q, k, v, sg, ga, gb, la), (q_s, k_s, v_s, sg_s, ga_s, gb_s, la_s) = _gla_in(
            xp, xs, row("norm_mix", l), w["win"], w["wa1"], w["wa2"], row("b_gla_a2", l), l=l, tm=IN_PROJ_TM)
        sz, xa, dt, tail = _ssm_in_prompt(xp, row("norm_mix", l), w["win"], w["wdt3"], w["dtb3"][l],
                                          p["conv_w"][l], row("conv_b", l), l=l, nb=nb, tm=IN_PROJ_TM)
        o, y, finals_p = _core_prompt(q, k, v, la, xa, dt, w["alog3"][l], w["dskip_x"][l], pconsts,
                                      nb=nb, rows=CORE_ROWS, l=l, depth=depth, final_states=finals_p)
        convs_p.append(tail)
        sz_s, xr, dt_s = _ssm_in_sample(xs, row("norm_mix", l), w["win"], w["wdt3"], w["dtb3"][l], l=l, tm=ts)
        c4 = jnp.pad(state_conv[l], ((0, 0), (SAMPLE_LEN - (CONV_W - 1), 0), (0, 0))).reshape(ts, CONV_DIM)
        o_s, y_s, s1, h1 = _core_sample(q_s, k_s, v_s, la_s, xr, c4, dt_s, state_gla, h_all, p["conv_w"][l],
                                        row("conv_b", l), w["alog3"][l], w["dskip_x"][l], sconsts,
                                        l=l, new_states=finals_s)
        finals_s = (s1, h1)
        convs_s.append(xr.reshape(nseq, SAMPLE_LEN, CONV_DIM)[:, SAMPLE_LEN - (CONV_W - 1):])
        xp, xs = _merge((xp, o, sg, y, sz, ga, gb), (xs, o_s, sg_s, y_s, sz_s, ga_s, gb_s),
                        row("gla_norm", l), row("ssm_norm", l), w["wpg"], w["wps"], w["wo"], l=l, tm=PROMPT_TM)
        xp, xs = _ffn(xp, xs, row("norm_ffn2", l), w["gu2"], w["d2"], fw, l=l, tm=FFN_TM,
                      final=l == depth - 1)
    s_fin, h_fin = finals_p
    s1, h1 = finals_s
    return (xp, xs,
            s_fin.reshape(depth, nb, GLA_HEADS, GLA_DK, GLA_DV),
            h_fin.reshape(depth, nb, SSM_HEADS, SSM_HEADDIM, SSM_DSTATE), jnp.stack(convs_p),
            s1, h1.reshape(state_ssm.shape), jnp.stack(convs_s))


def kernel(x_prompt, x_sample, state_gla, state_ssm, state_conv, norm_ffn1, w_ffn1_gu, w_ffn1_down, norm_mix, w_in, w_gla_a2, b_gla_a2, gla_norm, conv_w, conv_b, dt_bias, a_log, d_skip, ssm_norm, w_proj_gla, w_proj_ssm, w_out, norm_ffn2, w_ffn2_gu, w_ffn2_down, norm_final):
    p = dict(norm_ffn1=norm_ffn1, w_ffn1_gu=w_ffn1_gu, w_ffn1_down=w_ffn1_down, norm_mix=norm_mix,
             w_in=w_in, w_gla_a2=w_gla_a2, b_gla_a2=b_gla_a2, gla_norm=gla_norm, conv_w=conv_w,
             conv_b=conv_b, dt_bias=dt_bias, a_log=a_log, d_skip=d_skip, ssm_norm=ssm_norm,
             w_proj_gla=w_proj_gla, w_proj_ssm=w_proj_ssm, w_out=w_out, norm_ffn2=norm_ffn2,
             w_ffn2_gu=w_ffn2_gu, w_ffn2_down=w_ffn2_down)
    w = _prep_weights(p)
    fw = norm_final.reshape(1, D_MODEL)
    bp, lp, _ = x_prompt.shape
    bs, ls, _ = x_sample.shape
    assert ls == SAMPLE_LEN and lp % CORE_ROWS == 0 and all(lp % tm == 0 for tm in (PROMPT_TM, FFN_TM, IN_PROJ_TM))
    assert bs % SEQ_PER_STEP == 0 and bp % SEQS_PER_STEP == 0
    yp, ys, gla_p, ssm_p, conv_p, gla_s, ssm_s, conv_s = _trunk(
        x_prompt.reshape(bp * lp, D_MODEL), x_sample.reshape(bs * ls, D_MODEL),
        state_gla, state_ssm, state_conv, w, p, fw, bp)
    return (yp.reshape(bp, lp, D_MODEL), ys.reshape(bs, ls, D_MODEL),
            gla_p, ssm_p, conv_p, gla_s, ssm_s, conv_s)
```
